```python
import math
import jax
import jax.numpy as jnp
from jax import lax
import numpy as np

D_MODEL = 4096
BATCH = 1
SEQ = 8192
DEPTH = 2

CTX_LEN = 256
GRID_W = 64
ROPE_THETA = 10000.0
NORM_EPS = 1e-6
QBLK = 128

DIFF_HEADS = 8
DIFF_QK_DIM = 64
DIFF_V_DIM = 128
GQA_HEADS = 12
GQA_KV_HEADS = 4
GQA_DIM = 128
MLSTM_HEADS = 12
MLSTM_DIM = 128
MLSTM_CHUNK = 64
MLSTM_CONV = 3

DIFF_WIDTH = DIFF_HEADS * DIFF_V_DIM
GQA_WIDTH = GQA_HEADS * GQA_DIM
MLSTM_WIDTH = MLSTM_HEADS * MLSTM_DIM
MIX_WIDTH = DIFF_WIDTH + GQA_WIDTH + MLSTM_WIDTH
IN_SPLITS = (
    DIFF_HEADS * 2 * DIFF_QK_DIM,
    DIFF_HEADS * 2 * DIFF_QK_DIM,
    DIFF_WIDTH,
    GQA_HEADS * GQA_DIM,
    GQA_KV_HEADS * GQA_DIM,
    GQA_KV_HEADS * GQA_DIM,
    MLSTM_WIDTH,
    MLSTM_WIDTH,
    MLSTM_WIDTH,
    MLSTM_WIDTH,
    4 * MLSTM_HEADS,
)
IN_WIDTH = sum(IN_SPLITS)

D_FF = 11008
N_EXPERTS = 8
TOP_K = 2
D_EXPERT = 4096
MOE_BLK = 128
N_DENSE = (DEPTH + 1) // 2
N_MOE = DEPTH // 2

kernel_name = 'hybrid_diffusion_trunk_diffattn_gqa_mlstm_moe'


def rmsnorm(x, g):
    x32 = x.astype(jnp.float32)
    y = x32 * lax.rsqrt(jnp.mean(x32 * x32, axis=-1, keepdims=True) + NORM_EPS)
    return (y * g.astype(jnp.float32)).astype(x.dtype)


def modulate(x, g, shift, scale):
    return rmsnorm(x, g) * (1 + scale) + shift


def axial_rope(rows, dim, dtype):
    row = jnp.broadcast_to(jnp.arange(rows)[:, None], (rows, GRID_W)).reshape(-1).astype(jnp.float32)
    col = jnp.broadcast_to(jnp.arange(GRID_W)[None, :], (rows, GRID_W)).reshape(-1).astype(jnp.float32)
    quarter = dim // 4
    inv = ROPE_THETA ** (-jnp.arange(quarter, dtype=jnp.float32) / quarter)
    ang = jnp.concatenate([row[:, None] * inv, col[:, None] * inv], axis=-1)
    return jnp.cos(ang).astype(dtype), jnp.sin(ang).astype(dtype)


def apply_rope(x, cos, sin):
    half = x.shape[-1] // 2
    shape = (1, x.shape[1]) + (1,) * (x.ndim - 3) + (half,)
    c = cos.reshape(shape)
    s = sin.reshape(shape)
    x1, x2 = x[..., :half], x[..., half:]
    return jnp.concatenate([x1 * c - x2 * s, x2 * c + x1 * s], axis=-1)


def sweep_query_blocks(fn, q):
    b, s = q.shape[:2]
    nb = s // QBLK
    qb = jnp.moveaxis(q.reshape((b, nb, QBLK) + q.shape[2:]), 1, 0)
    out = jnp.moveaxis(lax.map(fn, qb), 0, 1)
    return out.reshape((b, s) + out.shape[3:])


def split_in(p):
    idx = [int(i) for i in np.cumsum(IN_SPLITS)[:-1]]
    return jnp.split(p, idx, axis=-1)


def centred_dwconv(x, w, bias):
    r = w.shape[0] // 2
    t = x.shape[1]
    xp = jnp.pad(x, ((0, 0), (r, r), (0, 0)))
    y = bias
    for j in range(w.shape[0]):
        y = y + xp[:, j:j + t] * w[j]
    return y


def diff_attention(q_lat, k_lat, v_lat, q_ctx, k_ctx, v_ctx, q_gain, k_gain, lam_p, out_gain,
                   lam_init, rope, need_ctx_out):
    b, s = q_lat.shape[:2]
    n_ctx = k_ctx.shape[1]
    cos, sin = rope

    def qk_heads(a):
        return a.reshape(a.shape[:2] + (DIFF_HEADS, 2, DIFF_QK_DIM))

    ql = apply_rope(rmsnorm(qk_heads(q_lat), q_gain), cos, sin)
    kl = apply_rope(rmsnorm(qk_heads(k_lat), k_gain), cos, sin)
    kc = rmsnorm(qk_heads(k_ctx), k_gain)
    vl = v_lat.reshape(b, s, DIFF_HEADS, DIFF_V_DIM)
    vc = v_ctx.reshape(b, n_ctx, DIFF_HEADS, DIFF_V_DIM)
    lp = lam_p.astype(jnp.float32)
    lam = jnp.exp(jnp.sum(lp[0] * lp[1])) - jnp.exp(jnp.sum(lp[2] * lp[3])) + lam_init
    scale = DIFF_QK_DIM ** -0.5

    def attend(k, v):
        def fn(qb):
            sc = jnp.einsum('bqhjd,bkhjd->bhjqk', qb, k).astype(jnp.float32) * scale
            p = jax.nn.softmax(sc, axis=-1)
            a = p[:, :, 0] - lam * p[:, :, 1]
            o = jnp.einsum('bhqk,bkhd->bqhd', a.astype(v.dtype), v)
            return rmsnorm(o, out_gain) * (1 - lam_init)
        return fn

    k_all = jnp.concatenate([kc, kl], axis=1)
    v_all = jnp.concatenate([vc, vl], axis=1)
    out_lat = sweep_query_blocks(attend(k_all, v_all), ql).reshape(b, s, DIFF_WIDTH)
    out_ctx = None
    if need_ctx_out:
        qc = rmsnorm(qk_heads(q_ctx), q_gain)
        out_ctx = attend(kc, vc)(qc).reshape(b, n_ctx, DIFF_WIDTH)
    return out_lat, out_ctx


def gqa_attention(q_lat, k_lat, v_lat, q_ctx, k_ctx, v_ctx, q_gain, k_gain, rope, need_ctx_out):
    b, s = q_lat.shape[:2]
    n_ctx = k_ctx.shape[1]
    grp = GQA_HEADS // GQA_KV_HEADS
    cos, sin = rope

    def q_heads(a):
        return a.reshape(a.shape[:2] + (GQA_HEADS, GQA_DIM))

    def kv_heads(a):
        return a.reshape(a.shape[:2] + (GQA_KV_HEADS, GQA_DIM))

    ql = apply_rope(rmsnorm(q_heads(q_lat), q_gain), cos, sin).reshape(b, s, GQA_KV_HEADS, grp, GQA_DIM)
    kl = apply_rope(rmsnorm(kv_heads(k_lat), k_gain), cos, sin)
    kc = rmsnorm(kv_heads(k_ctx), k_gain)
    vl, vc = kv_heads(v_lat), kv_heads(v_ctx)
    scale = GQA_DIM ** -0.5

    def attend(k, v):
        def fn(qb):
            sc = jnp.einsum('bqhgd,bkhd->bhgqk', qb, k).astype(jnp.float32) * scale
            p = jax.nn.softmax(sc, axis=-1)
            return jnp.einsum('bhgqk,bkhd->bqhgd', p.astype(v.dtype), v)
        return fn

    k_all = jnp.concatenate([kc, kl], axis=1)
    v_all = jnp.concatenate([vc, vl], axis=1)
    out_lat = sweep_query_blocks(attend(k_all, v_all), ql).reshape(b, s, GQA_WIDTH)
    out_ctx = None
    if need_ctx_out:
        qc = rmsnorm(q_heads(q_ctx), q_gain).reshape(b, n_ctx, GQA_KV_HEADS, grp, GQA_DIM)
        out_ctx = attend(kc, vc)(qc).reshape(b, n_ctx, GQA_WIDTH)
    return out_lat, out_ctx


def mlstm_chunkwise(q, k, v, log_i, log_f, state, return_h):
    b, t, h, d = q.shape
    nc = t // MLSTM_CHUNK

    def chunks(a):
        a = a.reshape((b, nc, MLSTM_CHUNK) + a.shape[2:])
        return jnp.moveaxis(a, (1, 3), (0, 2))

    tril = jnp.tril(jnp.ones((MLSTM_CHUNK, MLSTM_CHUNK), dtype=bool))

    def step(carry, xs):
        c_st, n_st, m_st = carry
        qc, kc, vc, li, lf = xs
        bcum = jnp.cumsum(lf, axis=-1)
        b_end = bcum[..., -1]
        log_w_end = b_end[..., None] - bcum + li
        m_new = jnp.maximum(b_end + m_st, jnp.max(log_w_end, axis=-1))
        w_end = jnp.exp(log_w_end - m_new[..., None])
        decay = jnp.exp(b_end + m_st - m_new)
        c_new = decay[..., None, None] * c_st + jnp.einsum('bhs,bhsd,bhse->bhde', w_end, kc, vc)
        n_new = decay[..., None] * n_st + jnp.einsum('bhs,bhsd->bhd', w_end, kc)
        if not return_h:
            return (c_new, n_new, m_new), None
        log_d = jnp.where(tril, bcum[..., :, None] - bcum[..., None, :] + li[..., None, :], -jnp.inf)
        m_t = jnp.maximum(bcum + m_st[..., None], jnp.max(log_d, axis=-1))
        dmat = jnp.exp(log_d - m_t[..., None])
        inter = jnp.exp(bcum + m_st[..., None] - m_t)
        sc = jnp.einsum('bhtd,bhsd->bhts', qc, kc) * dmat
        num = jnp.einsum('bhts,bhse->bhte', sc, vc) + inter[..., None] * jnp.einsum('bhtd,bhde->bhte', qc, c_st)
        den = jnp.sum(sc, axis=-1) + inter * jnp.einsum('bhtd,bhd->bht', qc, n_st)
        hout = num / jnp.maximum(jnp.abs(den), jnp.exp(-m_t))[..., None]
        return (c_new, n_new, m_new), hout

    state, hs = lax.scan(step, state, tuple(chunks(a) for a in (q, k, v, log_i, log_f)))
    if return_h:
        hs = jnp.moveaxis(hs, (0, 2), (1, 3)).reshape(b, t, h, d)
    return state, hs


def mlstm_prep(parts, conv_w, conv_b, gate_b):
    q, k, v, _, g = parts
    b, t = q.shape[:2]
    qk = jax.nn.silu(centred_dwconv(jnp.concatenate([q, k], axis=-1), conv_w, conv_b))
    q, k = jnp.split(qk, 2, axis=-1)
    q = q.reshape(b, t, MLSTM_HEADS, MLSTM_DIM).astype(jnp.float32)
    k = (k.reshape(b, t, MLSTM_HEADS, MLSTM_DIM) * MLSTM_DIM ** -0.5).astype(jnp.float32)
    v = v.reshape(b, t, MLSTM_HEADS, MLSTM_DIM).astype(jnp.float32)
    g = g.reshape(b, t, 4, MLSTM_HEADS).astype(jnp.float32) + gate_b.astype(jnp.float32)
    return q, k, v, g


def mlstm_mixer(lat_parts, ctx_parts, conv_w, conv_b, gate_b, out_gain, need_ctx_out):
    lat = mlstm_prep(lat_parts, conv_w, conv_b, gate_b)
    ctx = mlstm_prep(ctx_parts, conv_w, conv_b, gate_b)
    b = lat[0].shape[0]
    h_lat_dirs, h_ctx_dirs = [], []
    for direction in range(2):
        def seq_inputs(p):
            q, k, v, g = p
            li = g[:, :, 2 * direction]
            lf = jax.nn.log_sigmoid(g[:, :, 2 * direction + 1])
            arrs = (q, k, v, li, lf)
            if direction == 1:
                arrs = tuple(jnp.flip(a, axis=1) for a in arrs)
            return arrs
        state0 = (jnp.zeros((b, MLSTM_HEADS, MLSTM_DIM, MLSTM_DIM), jnp.float32),
                  jnp.zeros((b, MLSTM_HEADS, MLSTM_DIM), jnp.float32),
                  jnp.zeros((b, MLSTM_HEADS), jnp.float32))
        ctx_state, hc = mlstm_chunkwise(*seq_inputs(ctx), state0, need_ctx_out)
        _, hl = mlstm_chunkwise(*seq_inputs(lat), ctx_state, True)
        if direction == 1:
            hl = jnp.flip(hl, axis=1)
            if need_ctx_out:
                hc = jnp.flip(hc, axis=1)
        h_lat_dirs.append(hl)
        h_ctx_dirs.append(hc)

    def finish(dirs, o):
        hsum = rmsnorm(dirs[0] + dirs[1], out_gain.reshape(MLSTM_HEADS, MLSTM_DIM))
        return jax.nn.sigmoid(o) * hsum.reshape(o.shape).astype(o.dtype)

    out_lat = finish(h_lat_dirs, lat_parts[3])
    out_ctx = finish(h_ctx_dirs, ctx_parts[3]) if need_ctx_out else None
    return out_lat, out_ctx


def hybrid_mixer(h_lat, h_ctx, w_in, diff_q_norm, diff_k_norm, diff_lambda, diff_out_norm,
                 gqa_q_norm, gqa_k_norm, mlstm_conv_w, mlstm_conv_b, mlstm_gate_b, mlstm_out_norm,
                 lam_init, rope_diff, rope_gqa, need_ctx_out):
    pl = split_in(h_lat @ w_in)
    pc = split_in(h_ctx @ w_in)
    d_lat, d_ctx = diff_attention(pl[0], pl[1], pl[2], pc[0], pc[1], pc[2], diff_q_norm, diff_k_norm,
                                  diff_lambda, diff_out_norm, lam_init, rope_diff, need_ctx_out)
    g_lat, g_ctx = gqa_attention(pl[3], pl[4], pl[5], pc[3], pc[4], pc[5], gqa_q_norm, gqa_k_norm,
                                 rope_gqa, need_ctx_out)
    m_lat, m_ctx = mlstm_mixer(pl[6:11], pc[6:11], mlstm_conv_w, mlstm_conv_b, mlstm_gate_b,
                               mlstm_out_norm, need_ctx_out)
    out_lat = jnp.concatenate([d_lat, g_lat, m_lat], axis=-1)
    out_ctx = jnp.concatenate([d_ctx, g_ctx, m_ctx], axis=-1) if need_ctx_out else None
    return out_lat, out_ctx


def swiglu(t, w_gate, w_up, w_down):
    return (jax.nn.silu(t @ w_gate) * (t @ w_up)) @ w_down


def moe_swiglu(t, w_router, b_router, w_gate, w_up, w_down):
    n_tok, d = t.shape
    n_assign = n_tok * TOP_K
    logits = (t @ w_router).astype(jnp.float32) + b_router.astype(jnp.float32)
    top_logit, top_idx = lax.top_k(logits, TOP_K)
    top_w = jax.nn.softmax(top_logit, axis=-1)
    flat_e = top_idx.reshape(-1)
    flat_tok = jnp.repeat(jnp.arange(n_tok, dtype=jnp.int32), TOP_K)
    flat_w = top_w.reshape(-1)
    order = jnp.argsort(flat_e)
    e_sorted = flat_e[order]
    counts = jnp.bincount(flat_e, length=N_EXPERTS)
    starts = jnp.cumsum(counts) - counts
    padded = (counts + MOE_BLK - 1) // MOE_BLK * MOE_BLK
    pad_ends = jnp.cumsum(padded)
    pad_starts = pad_ends - padded
    dest = pad_starts[e_sorted] + (jnp.arange(n_assign, dtype=jnp.int32) - starts[e_sorted])
    n_blocks = -(-(n_assign + N_EXPERTS * (MOE_BLK - 1)) // MOE_BLK)
    n_slots = n_blocks * MOE_BLK
    slot_tok = jnp.zeros((n_slots,), jnp.int32).at[dest].set(flat_tok[order])
    slot_w = jnp.zeros((n_slots,), jnp.float32).at[dest].set(flat_w[order])
    block_e = jnp.minimum(jnp.searchsorted(pad_ends, jnp.arange(n_blocks, dtype=jnp.int32) * MOE_BLK,
                                           side='right'), N_EXPERTS - 1)
    x_blocks = t[slot_tok].reshape(n_blocks, MOE_BLK, d)

    def expert_block(args):
        xb, e = args
        return (jax.nn.silu(xb @ w_gate[e]) * (xb @ w_up[e])) @ w_down[e]

    y = lax.map(expert_block, (x_blocks, block_e)).reshape(n_slots, d)
    return jnp.zeros_like(t).at[slot_tok].add(y * slot_w[:, None].astype(t.dtype))


def setup_inputs(seed: int = 0) -> dict:
    key = jax.random.key(seed)
    ks = iter(jax.random.split(key, 32))

    def nrm(shape, scale):
        return jax.random.normal(next(ks), shape, jnp.float32) * scale

    def gain(shape):
        return 1.0 + nrm(shape, 0.1)

    lin = jnp.linspace(3.0, 6.0, MLSTM_HEADS)
    gate_base = jnp.zeros((4, MLSTM_HEADS), jnp.float32).at[1].set(lin).at[3].set(lin)
    return {
        'x': nrm((BATCH, SEQ, D_MODEL), 1.0),
        'c': nrm((BATCH, D_MODEL), 1.0),
        'ctx': nrm((BATCH, CTX_LEN, D_MODEL), 1.0),
        'c_ctx': nrm((D_MODEL,), 1.0),
        'w_mod': nrm((DEPTH, D_MODEL, 6 * D_MODEL), D_MODEL ** -0.5),
        'b_mod': nrm((DEPTH, 6 * D_MODEL), 0.02),
        'attn_norm': gain((DEPTH, D_MODEL)),
        'ffn_norm': gain((DEPTH, D_MODEL)),
        'w_in': nrm((DEPTH, D_MODEL, IN_WIDTH), D_MODEL ** -0.5),
        'w_out': nrm((DEPTH, MIX_WIDTH, D_MODEL), MIX_WIDTH ** -0.5),
        'diff_q_norm': gain((DEPTH, DIFF_QK_DIM)),
        'diff_k_norm': gain((DEPTH, DIFF_QK_DIM)),
        'diff_lambda': nrm((DEPTH, 4, DIFF_QK_DIM), 0.1),
        'diff_out_norm': gain((DEPTH, DIFF_V_DIM)),
        'gqa_q_norm': gain((DEPTH, GQA_DIM)),
        'gqa_k_norm': gain((DEPTH, GQA_DIM)),
        'mlstm_conv_w': nrm((DEPTH, MLSTM_CONV, 2 * MLSTM_WIDTH), MLSTM_CONV ** -0.5),
        'mlstm_conv_b': nrm((DEPTH, 2 * MLSTM_WIDTH), 0.01),
        'mlstm_gate_b': gate_base + nrm((DEPTH, 4, MLSTM_HEADS), 0.1),
        'mlstm_out_norm': gain((DEPTH, MLSTM_WIDTH)),
        'ffn_w_gate': nrm((N_DENSE, D_MODEL, D_FF), D_MODEL ** -0.5),
        'ffn_w_up': nrm((N_DENSE, D_MODEL, D_FF), D_MODEL ** -0.5),
        'ffn_w_down': nrm((N_DENSE, D_FF, D_MODEL), D_FF ** -0.5),
        'router_w': nrm((N_MOE, D_MODEL, N_EXPERTS), D_MODEL ** -0.5),
        'router_b': nrm((N_MOE, N_EXPERTS), 0.01),
        'exp_w_gate': nrm((N_MOE, N_EXPERTS, D_MODEL, D_EXPERT), D_MODEL ** -0.5),
        'exp_w_up': nrm((N_MOE, N_EXPERTS, D_MODEL, D_EXPERT), D_MODEL ** -0.5),
        'exp_w_down': nrm((N_MOE, N_EXPERTS, D_EXPERT, D_MODEL), D_EXPERT ** -0.5),
    }


def reference(x, c, ctx, c_ctx, w_mod, b_mod, attn_norm, ffn_norm, w_in, w_out, diff_q_norm, diff_k_norm,
              diff_lambda, diff_out_norm, gqa_q_norm, gqa_k_norm, mlstm_conv_w, mlstm_conv_b, mlstm_gate_b,
              mlstm_out_norm, ffn_w_gate, ffn_w_up, ffn_w_down, router_w, router_b, exp_w_gate, exp_w_up,
              exp_w_down):
    b, s, d = x.shape
    n_ctx = ctx.shape[1]
    rows = s // GRID_W
    rope_diff = axial_rope(rows, DIFF_QK_DIM, x.dtype)
    rope_gqa = axial_rope(rows, GQA_DIM, x.dtype)
    x_lat, x_ctx = x, ctx
    for l in range(DEPTH):
        need_ctx = l < DEPTH - 1
        lam_init = 0.8 - 0.6 * math.exp(-0.3 * l)
        sh_a, sc_a, g_a, sh_f, sc_f, g_f = jnp.split(
            (jax.nn.silu(c) @ w_mod[l] + b_mod[l])[:, None, :], 6, axis=-1)
        csh_a, csc_a, cg_a, csh_f, csc_f, cg_f = jnp.split(
            jax.nn.silu(c_ctx) @ w_mod[l] + b_mod[l], 6, axis=-1)
        h_lat = modulate(x_lat, attn_norm[l], sh_a, sc_a)
        h_ctx = modulate(x_ctx, attn_norm[l], csh_a, csc_a)
        mix_lat, mix_ctx = hybrid_mixer(h_lat, h_ctx, w_in[l], diff_q_norm[l], diff_k_norm[l], diff_lambda[l],
                                        diff_out_norm[l], gqa_q_norm[l], gqa_k_norm[l], mlstm_conv_w[l],
                                        mlstm_conv_b[l], mlstm_gate_b[l], mlstm_out_norm[l], lam_init,
                                        rope_diff, rope_gqa, need_ctx)
        x_lat = x_lat + g_a * (mix_lat @ w_out[l])
        if need_ctx:
            x_ctx = x_ctx + cg_a * (mix_ctx @ w_out[l])
        tok = modulate(x_lat, ffn_norm[l], sh_f, sc_f).reshape(b * s, d)
        if need_ctx:
            tok = jnp.concatenate([modulate(x_ctx, ffn_norm[l], csh_f, csc_f).reshape(b * n_ctx, d), tok], axis=0)
        if l % 2 == 0:
            y = swiglu(tok, ffn_w_gate[l // 2], ffn_w_up[l // 2], ffn_w_down[l // 2])
        else:
            y = moe_swiglu(tok, router_w[l // 2], router_b[l // 2], exp_w_gate[l // 2], exp_w_up[l // 2],
                           exp_w_down[l // 2])
        if need_ctx:
            x_ctx = x_ctx + cg_f * y[:b * n_ctx].reshape(b, n_ctx, d)
            y = y[b * n_ctx:]
        x_lat = x_lat + g_f * y.reshape(b, s, d)
    return x_lat
```

```python
import functools
import math

import jax
import jax.numpy as jnp
from jax import lax
from jax.experimental import pallas as pl
from jax.experimental.pallas import tpu as pltpu

F32 = jnp.float32
BF16 = jnp.bfloat16
HIGHEST = lax.Precision.HIGHEST

GRID_W = 64
ROPE_THETA = 10000.0
NORM_EPS = 1e-6

DIFF_HEADS = 8
DIFF_QK_DIM = 64
GQA_HEADS = 12
GQA_KV_HEADS = 4
GQA_DIM = 128
MLSTM_HEADS = 12
MLSTM_DIM = 128
N_EXPERTS = 8
TOP_K = 2

LANES = 128
ROW_TILE = 256
KEY_TILE = 512
MLSTM_CHUNK = 256
MOE_BLK = 512
GATHER_ROWS = 256
VMEM_CAP = 60 * 1024 * 1024

_DQ, _DK, _DV = 0, 1024, 2048
_GQ, _GK, _GV = 3072, 4608, 5120
_MQ, _MK, _MV, _MO, _MG = 5632, 7168, 8704, 10240, 11776
_MAIN_WIDTH = 11776


def _cparams(sem, vmem_bytes):
    limit = int(min(VMEM_CAP, max(32 * 1024 * 1024, vmem_bytes + (8 << 20))))
    return pltpu.CompilerParams(dimension_semantics=sem, vmem_limit_bytes=limit)


def _row_tile(m, cands=(1024, 768, 512, 256)):
    for t in cands:
        if m % t == 0:
            return t
    raise ValueError(f"no row tile for {m}")


def _log_sigmoid(x):
    return jnp.minimum(x, 0.0) - jnp.log1p(jnp.exp(-jnp.abs(x)))


def _silu(x):
    return x * jax.nn.sigmoid(x)


def _mod_kernel(c_ref, w_ref, b_ref, o_ref):
    s = _silu(c_ref[...])
    o_ref[0] = jnp.dot(s, w_ref[0], preferred_element_type=F32) + b_ref[0]


def _mods(cc, w_mod, b_mod):
    n_layers, d, n6 = w_mod.shape
    tn = 512
    return pl.pallas_call(
        _mod_kernel,
        grid=(n_layers, n6 // tn),
        in_specs=[pl.BlockSpec((8, d), lambda l, j: (0, 0)),
                  pl.BlockSpec((1, d, tn), lambda l, j: (l, 0, j)),
                  pl.BlockSpec((1, 1, tn), lambda l, j: (l, 0, j))],
        out_specs=pl.BlockSpec((1, 8, tn), lambda l, j: (l, 0, j)),
        out_shape=jax.ShapeDtypeStruct((n_layers, 8, n6), F32),
        compiler_params=_cparams(("parallel", "parallel"), 2 * d * tn * 4),
        name="adaln_mods",
    )(cc, w_mod, b_mod.reshape(n_layers, 1, n6))


def _modnorm_kernel(x_ref, g_ref, sh_ref, sc_ref, *rest, n_lat, tm, with_router):
    x = x_ref[...]
    ms = jnp.mean(x * x, axis=-1, keepdims=True)
    y = x * lax.rsqrt(ms + NORM_EPS) * g_ref[...]
    rows = pl.program_id(0) * tm + lax.broadcasted_iota(jnp.int32, (tm, 1), 0)
    is_ctx = rows >= n_lat
    sc = jnp.where(is_ctx, sc_ref[1:2, :], sc_ref[0:1, :])
    sh = jnp.where(is_ctx, sh_ref[1:2, :], sh_ref[0:1, :])
    h = y * (1.0 + sc) + sh
    if with_router:
        rw_ref, rb_ref, o_ref, lg_ref = rest
        lg_ref[...] = jnp.dot(h, rw_ref[...], preferred_element_type=F32, precision=HIGHEST) + rb_ref[...]
    else:
        (o_ref,) = rest
    o_ref[...] = h.astype(o_ref.dtype)


def _modnorm(x, gain, mods, shift_idx, n_lat, router=None):
    m, d = x.shape
    tm = ROW_TILE
    in_specs = [pl.BlockSpec((tm, d), lambda i: (i, 0)),
                pl.BlockSpec((1, d), lambda i: (0, 0)),
                pl.BlockSpec((8, d), lambda i: (0, shift_idx)),
                pl.BlockSpec((8, d), lambda i: (0, shift_idx + 1))]
    args = [x, gain.reshape(1, d), mods, mods]
    out_specs = pl.BlockSpec((tm, d), lambda i: (i, 0))
    out_shape = jax.ShapeDtypeStruct((m, d), BF16)
    if router is not None:
        rw, rb = router
        in_specs += [pl.BlockSpec((d, LANES), lambda i: (0, 0)),
                     pl.BlockSpec((1, LANES), lambda i: (0, 0))]
        args += [rw, rb]
        out_specs = [out_specs, pl.BlockSpec((tm, LANES), lambda i: (i, 0))]
        out_shape = [out_shape, jax.ShapeDtypeStruct((m, LANES), F32)]
    return pl.pallas_call(
        functools.partial(_modnorm_kernel, n_lat=n_lat, tm=tm, with_router=router is not None),
        grid=(m // tm,),
        in_specs=in_specs, out_specs=out_specs, out_shape=out_shape,
        compiler_params=_cparams(("parallel",), 2 * tm * d * 6 + 4 * tm * d * 4 + d * LANES * 8),
        name="modnorm",
    )(*args)


def _mm_kernel(*refs, n_a, n_w, kind, n_lat, tm):
    a_refs, w_refs, rest = refs[:n_a], refs[n_a:n_a + n_w], refs[n_a + n_w:]

    def w_of(r):
        return r[0] if len(r.shape) == 3 else r[...]

    if kind == "swiglu":
        a = a_refs[0][...]
        g = jnp.dot(a, w_of(w_refs[0]), preferred_element_type=F32)
        u = jnp.dot(a, w_of(w_refs[1]), preferred_element_type=F32)
        (o_ref,) = rest
        o_ref[...] = (_silu(g) * u).astype(o_ref.dtype)
        return
    acc = jnp.dot(a_refs[0][...], w_of(w_refs[0]), preferred_element_type=F32)
    for a_ref, w_ref in zip(a_refs[1:], w_refs[1:]):
        acc = acc + jnp.dot(a_ref[...], w_of(w_ref), preferred_element_type=F32)
    if kind == "plain":
        (o_ref,) = rest
        o_ref[...] = acc.astype(o_ref.dtype)
    else:
        x_ref, g_ref, o_ref = rest
        rows = pl.program_id(0) * tm + lax.broadcasted_iota(jnp.int32, (tm, 1), 0)
        gate = jnp.where(rows >= n_lat, g_ref[1:2, :], g_ref[0:1, :])
        o_ref[...] = x_ref[...] + gate * acc


def _matmul(a_list, w_list, *, rows, tm, tn, kind="plain", out_dtype=F32, resid=None, n_lat=0, name="matmul"):
    n = w_list[0].shape[1]
    in_specs = [pl.BlockSpec((tm, a.shape[1]), lambda i, j: (i, 0)) for a in a_list]
    in_specs += [pl.BlockSpec((w.shape[0], tn), lambda i, j: (0, j)) for w in w_list]
    args = list(a_list) + list(w_list)
    vmem = sum(2 * tm * a.shape[1] * 2 for a in a_list) + sum(2 * w.shape[0] * tn * 2 for w in w_list)
    vmem += 6 * tm * tn * 4
    if kind == "resid":
        x, mods, gate_idx = resid
        d = x.shape[1]
        in_specs += [pl.BlockSpec((tm, tn), lambda i, j: (i, j)),
                     pl.BlockSpec((8, tn), lambda i, j: (0, gate_idx * (d // tn) + j))]
        args += [x, mods]
    return pl.pallas_call(
        functools.partial(_mm_kernel, n_a=len(a_list), n_w=len(w_list), kind=kind, n_lat=n_lat, tm=tm),
        grid=(rows // tm, n // tn),
        in_specs=in_specs,
        out_specs=pl.BlockSpec((tm, tn), lambda i, j: (i, j)),
        out_shape=jax.ShapeDtypeStruct((rows, n), out_dtype),
        compiler_params=_cparams(("parallel", "arbitrary"), vmem),
        name=name,
    )(*args)


def _qkprep_kernel(x_ref, g_ref, c_ref, s_ref, o_ref, *, seg, scale, split):
    x = x_ref[...].astype(F32)
    lane = lax.broadcasted_iota(jnp.int32, x.shape, 1)
    x2 = x * x
    if seg == LANES:
        ms = jnp.sum(x2, axis=-1, keepdims=True) * (1.0 / seg)
    else:
        lo = lane < seg
        s_lo = jnp.sum(jnp.where(lo, x2, 0.0), axis=-1, keepdims=True)
        s_hi = jnp.sum(jnp.where(lo, 0.0, x2), axis=-1, keepdims=True)
        ms = jnp.where(lo, s_lo, s_hi) * (1.0 / seg)
    y = x * lax.rsqrt(ms + NORM_EPS) * g_ref[...]
    half = seg // 2
    if seg == LANES:
        partner = pltpu.roll(y, half, 1)
    else:
        partner = jnp.where((lane % seg) < half, pltpu.roll(y, LANES - half, 1), pltpu.roll(y, half, 1))
    y = (y * c_ref[...] + partner * s_ref[...]) * scale
    if split:
        o_ref[:, :LANES] = jnp.where(lane < seg, y, 0.0).astype(o_ref.dtype)
        o_ref[:, LANES:] = jnp.where(lane < seg, 0.0, y).astype(o_ref.dtype)
    else:
        o_ref[...] = y.astype(o_ref.dtype)


def _qkprep(proj, col0, n_tiles, gain, rope_c, rope_s, *, seg, scale, split):
    t = proj.shape[0]
    tm = ROW_TILE
    cb = col0 // LANES
    g = jnp.tile(gain.astype(F32), LANES // seg).reshape(1, LANES)
    ow = 2 * LANES if split else LANES
    return pl.pallas_call(
        functools.partial(_qkprep_kernel, seg=seg, scale=scale, split=split),
        grid=(t // tm, n_tiles),
        in_specs=[pl.BlockSpec((tm, LANES), lambda i, j: (i, cb + j)),
                  pl.BlockSpec((1, LANES), lambda i, j: (0, 0)),
                  pl.BlockSpec((tm, LANES), lambda i, j: (i, 0)),
                  pl.BlockSpec((tm, LANES), lambda i, j: (i, 0))],
        out_specs=pl.BlockSpec((tm, ow), lambda i, j: (i, j)),
        out_shape=jax.ShapeDtypeStruct((t, n_tiles * ow), BF16),
        compiler_params=_cparams(("parallel", "arbitrary"), 0),
        name="qk_prep",
    )(proj, g, rope_c, rope_s)


def _attn_kernel(q_ref, k_ref, v_ref, o_ref, *, n_q_heads, n_lat, n_ctx, tq, tk):
    is_ctx = pl.program_id(1) * tq >= n_lat
    q = jnp.concatenate([q_ref[:, g * LANES:(g + 1) * LANES] for g in range(n_q_heads)], axis=0)
    rows = n_q_heads * tq

    def step(kc, vc, carry):
        m, l, acc = carry
        s = lax.dot_general(q, kc, (((1,), (1,)), ((), ())), preferred_element_type=F32)
        m_new = jnp.maximum(m, jnp.max(s, axis=-1, keepdims=True))
        alpha = jnp.exp(m - m_new)
        p = jnp.exp(s - m_new)
        l = alpha * l + jnp.sum(p, axis=-1, keepdims=True)
        acc = alpha * acc + jnp.dot(p.astype(BF16), vc, preferred_element_type=F32)
        return m_new, l, acc

    carry = (jnp.full((rows, 1), -jnp.inf, F32), jnp.zeros((rows, 1), F32), jnp.zeros((rows, LANES), F32))
    carry = step(k_ref[n_lat:n_lat + n_ctx, :], v_ref[n_lat:n_lat + n_ctx, :], carry)

    def body(c, carry):
        start = pl.multiple_of(c * tk, tk)
        return step(k_ref[pl.ds(start, tk), :], v_ref[pl.ds(start, tk), :], carry)

    _, l, acc = lax.fori_loop(0, jnp.where(is_ctx, 0, n_lat // tk), body, carry)
    out = acc / l
    for g in range(n_q_heads):
        o_ref[:, g * LANES:(g + 1) * LANES] = out[g * tq:(g + 1) * tq].astype(o_ref.dtype)


def _attention(q, k, v, v_col0, *, n_kv, n_q_heads, n_lat, n_ctx, q_rows, out_dtype):
    t = k.shape[0]
    tq = ROW_TILE
    tk = min(KEY_TILE, n_lat)
    qw = n_q_heads * LANES
    vb = v_col0 // LANES
    vmem = 2 * 2 * t * LANES * 2 + 4 * tq * qw * 4 + 6 * n_q_heads * tq * tk * 4
    return pl.pallas_call(
        functools.partial(_attn_kernel, n_q_heads=n_q_heads, n_lat=n_lat, n_ctx=n_ctx, tq=tq, tk=tk),
        grid=(n_kv, q_rows // tq),
        in_specs=[pl.BlockSpec((tq, qw), lambda h, i: (i, h)),
                  pl.BlockSpec((t, LANES), lambda h, i: (0, h)),
                  pl.BlockSpec((t, LANES), lambda h, i: (0, vb + h))],
        out_specs=pl.BlockSpec((tq, qw), lambda h, i: (i, h)),
        out_shape=jax.ShapeDtypeStruct((q_rows, n_kv * qw), out_dtype),
        compiler_params=_cparams(("parallel", "arbitrary"), vmem),
        name="attention",
    )(q, k, v)


def _diff_finish_kernel(o_ref, lam_ref, g_ref, out_ref, *, lam_init):
    lp = lam_ref[...]
    lam = (jnp.exp(jnp.sum(lp[0:1, :] * lp[1:2, :], axis=-1, keepdims=True))
           - jnp.exp(jnp.sum(lp[2:3, :] * lp[3:4, :], axis=-1, keepdims=True)) + lam_init)
    o = o_ref[:, :LANES] - lam * o_ref[:, LANES:]
    ms = jnp.mean(o * o, axis=-1, keepdims=True)
    out_ref[...] = (o * lax.rsqrt(ms + NORM_EPS) * g_ref[...] * (1.0 - lam_init)).astype(out_ref.dtype)


def _diff_finish(o2, lam_p, out_gain, lam_init):
    rows = o2.shape[0]
    tm = ROW_TILE
    return pl.pallas_call(
        functools.partial(_diff_finish_kernel, lam_init=lam_init),
        grid=(rows // tm, DIFF_HEADS),
        in_specs=[pl.BlockSpec((tm, 2 * LANES), lambda i, h: (i, h)),
                  pl.BlockSpec((4, DIFF_QK_DIM), lambda i, h: (0, 0)),
                  pl.BlockSpec((1, LANES), lambda i, h: (0, 0))],
        out_specs=pl.BlockSpec((tm, LANES), lambda i, h: (i, h)),
        out_shape=jax.ShapeDtypeStruct((rows, DIFF_HEADS * LANES), BF16),
        compiler_params=_cparams(("parallel", "arbitrary"), 0),
        name="diff_finish",
    )(o2, lam_p.astype(F32), out_gain.astype(F32).reshape(1, LANES))


def _conv_kernel(x_ref, xp_ref, xn_ref, w_ref, b_ref, o_ref, *, n_lat, t_all, tm, k_col_tile):
    i, j = pl.program_id(0), pl.program_id(1)
    x = x_ref[...].astype(F32)
    row = lax.broadcasted_iota(jnp.int32, (tm, 1), 0)
    grow = i * tm + row
    prev = jnp.where(row == 0, xp_ref[15:16, :].astype(F32), pltpu.roll(x, 1, 0))
    nxt = jnp.where(row == tm - 1, xn_ref[0:1, :].astype(F32), pltpu.roll(x, tm - 1, 0))
    prev = jnp.where((grow == 0) | (grow == n_lat), 0.0, prev)
    nxt = jnp.where((grow == n_lat - 1) | (grow == t_all - 1), 0.0, nxt)
    y = b_ref[...] + prev * w_ref[0:1, :] + x * w_ref[1:2, :] + nxt * w_ref[2:3, :]
    y = _silu(y) * jnp.where(j >= k_col_tile, MLSTM_DIM ** -0.5, 1.0)
    o_ref[...] = y.astype(o_ref.dtype)


def _mlstm_conv(proj, conv_w, conv_b, n_lat):
    t = proj.shape[0]
    tm, tc, hb = ROW_TILE, 512, 16
    width = 2 * MLSTM_HEADS * MLSTM_DIM
    cb = _MQ // tc
    nrb = t // hb
    return pl.pallas_call(
        functools.partial(_conv_kernel, n_lat=n_lat, t_all=t, tm=tm, k_col_tile=(width // 2) // tc),
        grid=(t // tm, width // tc),
        in_specs=[pl.BlockSpec((tm, tc), lambda i, j: (i, cb + j)),
                  pl.BlockSpec((hb, tc), lambda i, j: (jnp.maximum(i * (tm // hb) - 1, 0), cb + j)),
                  pl.BlockSpec((hb, tc), lambda i, j: (jnp.minimum((i + 1) * (tm // hb), nrb - 1), cb + j)),
                  pl.BlockSpec((3, tc), lambda i, j: (0, j)),
                  pl.BlockSpec((1, tc), lambda i, j: (0, j))],
        out_specs=pl.BlockSpec((tm, tc), lambda i, j: (i, j)),
        out_shape=jax.ShapeDtypeStruct((t, width), BF16),
        compiler_params=_cparams(("parallel", "arbitrary"), 0),
        name="mlstm_conv",
    )(proj, proj, proj, conv_w.astype(F32), conv_b.astype(F32).reshape(1, width))


def _mlstm_kernel(q_ref, k_ref, kt_ref, v_ref, gi_ref, gf_ref, bi_ref, bf_ref, mask_ref, h_ref,
                  c_sc, n_sc, m_sc, *, n_heads):
    @pl.when(pl.program_id(0) == 0)
    def _():
        c_sc[...] = jnp.zeros_like(c_sc)
        n_sc[...] = jnp.zeros_like(n_sc)
        m_sc[...] = jnp.zeros_like(m_sc)

    vis = mask_ref[...]
    li_all = gi_ref[...] + bi_ref[...]
    lf_all = _log_sigmoid(gf_ref[...] + bf_ref[...])
    br_all = lax.dot_general(lf_all, vis, (((1,), (1,)), ((), ())), preferred_element_type=F32, precision=HIGHEST)
    bend_all = jnp.sum(lf_all, axis=1, keepdims=True)
    d = MLSTM_DIM
    for h in range(n_heads):
        lir, lfr, br = li_all[h:h + 1, :], lf_all[h:h + 1, :], br_all[h:h + 1, :]
        b_end = bend_all[h:h + 1, :]
        m_prev = m_sc[h:h + 1, 0:1]
        bc = jnp.sum(vis * lfr, axis=1, keepdims=True)
        lw = b_end - br + lir
        m_new = jnp.maximum(b_end + m_prev, jnp.max(lw, axis=1, keepdims=True))
        w_end = jnp.exp(lw - m_new)
        decay = jnp.exp(b_end + m_prev - m_new)
        log_d = jnp.where(vis > 0.0, bc - br + lir, -jnp.inf)
        m_t = jnp.maximum(bc + m_prev, jnp.max(log_d, axis=1, keepdims=True))
        dmat = jnp.exp(log_d - m_t)
        inter = jnp.exp(bc + m_prev - m_t)
        qh = q_ref[:, h * d:(h + 1) * d]
        kh = k_ref[:, h * d:(h + 1) * d]
        vh = v_ref[:, h * d:(h + 1) * d]
        kth = kt_ref[h * d:(h + 1) * d, :]
        c_prev = c_sc[h]
        n_prev = n_sc[h:h + 1, :]
        sc = jnp.dot(qh, kth, preferred_element_type=F32) * dmat
        num = (jnp.dot(sc.astype(BF16), vh, preferred_element_type=F32)
               + inter * jnp.dot(qh, c_prev.astype(BF16), preferred_element_type=F32))
        den = (jnp.sum(sc, axis=1, keepdims=True)
               + inter * jnp.sum(qh.astype(F32) * n_prev, axis=1, keepdims=True))
        h_ref[:, h * d:(h + 1) * d] = num / jnp.maximum(jnp.abs(den), jnp.exp(-m_t))
        ktw = (kth.astype(F32) * w_end).astype(BF16)
        c_sc[h] = decay * c_prev + jnp.dot(ktw, vh, preferred_element_type=F32)
        w8 = jnp.broadcast_to(w_end, (8, w_end.shape[1])).astype(BF16)
        n_sc[h:h + 1, :] = decay * n_prev + jnp.dot(w8, kh, preferred_element_type=F32)[0:1, :]
        m_sc[h:h + 1, :] = jnp.broadcast_to(m_new, (1, LANES))


def _mlstm_direction(qk, kt, v, gi, gf, bi, bf, vis, *, n_lat, reverse):
    t = qk.shape[0]
    lc = MLSTM_CHUNK
    n_chunks = t // lc
    ctx_blk = n_lat // lc
    hd = MLSTM_HEADS * MLSTM_DIM

    def blk(c):
        lat = (n_chunks - 1 - c) if reverse else (c - 1)
        return jnp.where(c == 0, ctx_blk, lat)

    return pl.pallas_call(
        functools.partial(_mlstm_kernel, n_heads=MLSTM_HEADS),
        grid=(n_chunks,),
        in_specs=[pl.BlockSpec((lc, hd), lambda c: (blk(c), 0)),
                  pl.BlockSpec((lc, hd), lambda c: (blk(c), 1)),
                  pl.BlockSpec((hd, lc), lambda c: (0, blk(c))),
                  pl.BlockSpec((lc, hd), lambda c: (blk(c), 0)),
                  pl.BlockSpec((16, lc), lambda c: (0, blk(c))),
                  pl.BlockSpec((16, lc), lambda c: (0, blk(c))),
                  pl.BlockSpec((16, 1), lambda c: (0, 0)),
                  pl.BlockSpec((16, 1), lambda c: (0, 0)),
                  pl.BlockSpec((lc, lc), lambda c: (0, 0))],
        out_specs=pl.BlockSpec((lc, hd), lambda c: (blk(c), 0)),
        out_shape=jax.ShapeDtypeStruct((t, hd), F32),
        scratch_shapes=[pltpu.VMEM((MLSTM_HEADS, MLSTM_DIM, MLSTM_DIM), F32),
                        pltpu.VMEM((16, LANES), F32),
                        pltpu.VMEM((16, LANES), F32)],
        compiler_params=_cparams(("arbitrary",), 16 << 20),
        name="mlstm_bwd" if reverse else "mlstm_fwd",
    )(qk, qk, kt, v, gi, gf, bi, bf, vis)


def _mlstm_finish_kernel(hf_ref, hb_ref, o_ref, g_ref, out_ref, *, n_heads):
    d = MLSTM_DIM
    for h in range(n_heads):
        sl = slice(h * d, (h + 1) * d)
        s = hf_ref[:, sl] + hb_ref[:, sl]
        ms = jnp.mean(s * s, axis=-1, keepdims=True)
        y = s * lax.rsqrt(ms + NORM_EPS) * g_ref[:, sl]
        out_ref[:, sl] = (jax.nn.sigmoid(o_ref[:, sl].astype(F32)) * y).astype(out_ref.dtype)


def _mlstm_finish(hf, hb, o_gate, out_gain, rows):
    hd = MLSTM_HEADS * MLSTM_DIM
    tm = ROW_TILE
    return pl.pallas_call(
        functools.partial(_mlstm_finish_kernel, n_heads=MLSTM_HEADS),
        grid=(rows // tm,),
        in_specs=[pl.BlockSpec((tm, hd), lambda i: (i, 0)),
                  pl.BlockSpec((tm, hd), lambda i: (i, 0)),
                  pl.BlockSpec((tm, hd), lambda i: (i, 0)),
                  pl.BlockSpec((1, hd), lambda i: (0, 0))],
        out_specs=pl.BlockSpec((tm, hd), lambda i: (i, 0)),
        out_shape=jax.ShapeDtypeStruct((rows, hd), BF16),
        compiler_params=_cparams(("parallel",), 0),
        name="mlstm_finish",
    )(hf, hb, o_gate, out_gain.astype(F32).reshape(1, hd))


def _gather_kernel(idx_ref, src_ref, dst_ref, sem, *, rows):
    base = pl.program_id(0) * rows

    def copy(r):
        return pltpu.make_async_copy(src_ref.at[idx_ref[base + r]], dst_ref.at[base + r], sem)

    def start(r, carry):
        copy(r).start()
        return carry

    def wait(r, carry):
        copy(r).wait()
        return carry

    lax.fori_loop(0, rows, start, 0)
    lax.fori_loop(0, rows, wait, 0)


def _moe_gather(tok3, slot_tok):
    n_slots = slot_tok.shape[0]
    return pl.pallas_call(
        functools.partial(_gather_kernel, rows=GATHER_ROWS),
        grid_spec=pltpu.PrefetchScalarGridSpec(
            num_scalar_prefetch=1,
            grid=(n_slots // GATHER_ROWS,),
            in_specs=[pl.BlockSpec(memory_space=pl.ANY)],
            out_specs=pl.BlockSpec(memory_space=pl.ANY),
            scratch_shapes=[pltpu.SemaphoreType.DMA(())]),
        out_shape=jax.ShapeDtypeStruct((n_slots,) + tok3.shape[1:], tok3.dtype),
        compiler_params=pltpu.CompilerParams(dimension_semantics=("arbitrary",), has_side_effects=True),
        name="moe_gather",
    )(slot_tok, tok3)


def _expert_kernel(be_ref, nu_ref, *refs, kind):
    o_ref = refs[-1]
    b = pl.program_id(0)

    @pl.when(b < nu_ref[0])
    def _():
        _mm_kernel(*refs, n_a=1, n_w=len(refs) - 2, kind=kind, n_lat=0, tm=0)

    @pl.when(b >= nu_ref[0])
    def _():
        o_ref[...] = jnp.zeros_like(o_ref)


def _expert_matmul(a, w_list, block_e, n_used, *, tn, kind, out_dtype, name):
    n_slots, k = a.shape
    n = w_list[0].shape[2]
    nb, nj = n_slots // MOE_BLK, n // tn

    def a_map(b, j, be, nu):
        return (jnp.minimum(b, nu[0] - 1), 0)

    def w_map(b, j, be, nu):
        return (be[b], 0, jnp.where(b < nu[0], j, nj - 1))

    return pl.pallas_call(
        functools.partial(_expert_kernel, kind=kind),
        grid_spec=pltpu.PrefetchScalarGridSpec(
            num_scalar_prefetch=2,
            grid=(nb, nj),
            in_specs=[pl.BlockSpec((MOE_BLK, k), a_map)] + [pl.BlockSpec((1, k, tn), w_map) for _ in w_list],
            out_specs=pl.BlockSpec((MOE_BLK, tn), lambda b, j, be, nu: (b, j))),
        out_shape=jax.ShapeDtypeStruct((n_slots, n), out_dtype),
        compiler_params=_cparams(("arbitrary", "arbitrary"),
                                 2 * MOE_BLK * k * 2 + len(w_list) * 2 * k * tn * 2 + 6 * MOE_BLK * tn * 4),
        name=name,
    )(block_e, n_used, a, *w_list)


def _combine_kernel(d0_ref, d1_ref, y_ref, x_ref, w0_ref, w1_ref, g_ref, o_ref, buf, sem, *, rows):
    base = pl.program_id(0) * rows

    def copy(r, which):
        idx = d0_ref[base + r] if which == 0 else d1_ref[base + r]
        return pltpu.make_async_copy(y_ref.at[pl.ds(idx, 1), :], buf.at[which, pl.ds(r, 1), :], sem)

    def start(r, carry):
        copy(r, 0).start()
        copy(r, 1).start()
        return carry

    def wait(r, carry):
        copy(r, 0).wait()
        copy(r, 1).wait()
        return carry

    lax.fori_loop(0, rows, start, 0)
    lax.fori_loop(0, rows, wait, 0)
    y = w0_ref[...] * buf[0] + w1_ref[...] * buf[1]
    o_ref[...] = x_ref[...] + g_ref[0:1, :] * y


def _moe_combine(y_slots, x, d0, d1, w0, w1, mods, gate_idx):
    n_tok, d = x.shape
    rows = GATHER_ROWS // 2
    return pl.pallas_call(
        functools.partial(_combine_kernel, rows=rows),
        grid_spec=pltpu.PrefetchScalarGridSpec(
            num_scalar_prefetch=2,
            grid=(n_tok // rows,),
            in_specs=[pl.BlockSpec(memory_space=pl.ANY),
                      pl.BlockSpec((rows, d), lambda i, a, b: (i, 0)),
                      pl.BlockSpec((rows, 1), lambda i, a, b: (i, 0)),
                      pl.BlockSpec((rows, 1), lambda i, a, b: (i, 0)),
                      pl.BlockSpec((8, d), lambda i, a, b: (0, gate_idx))],
            out_specs=pl.BlockSpec((rows, d), lambda i, a, b: (i, 0)),
            scratch_shapes=[pltpu.VMEM((2, rows, d), F32), pltpu.SemaphoreType.DMA(())]),
        out_shape=jax.ShapeDtypeStruct((n_tok, d), F32),
        compiler_params=_cparams(("arbitrary",), 8 * rows * d * 4),
        name="moe_combine",
    )(d0, d1, y_slots, x, w0, w1, mods)


def _moe_routing(logits, n_tok):
    top_logit, top_idx = lax.top_k(logits, TOP_K)
    top_w = jax.nn.softmax(top_logit, axis=-1)
    flat_e = top_idx.reshape(-1).astype(jnp.int32)
    n_assign = n_tok * TOP_K
    onehot = (flat_e[:, None] == jnp.arange(N_EXPERTS, dtype=jnp.int32)[None, :]).astype(jnp.int32)
    csum = jnp.cumsum(onehot, axis=0)
    rank = jnp.take_along_axis(csum, flat_e[:, None], axis=1)[:, 0] - 1
    counts = csum[-1]
    padded = (counts + MOE_BLK - 1) // MOE_BLK * MOE_BLK
    pad_ends = jnp.cumsum(padded)
    pad_starts = pad_ends - padded
    dest = (pad_starts[flat_e] + rank).astype(jnp.int32)
    n_blocks = -(-(n_assign + N_EXPERTS * (MOE_BLK - 1)) // MOE_BLK)
    n_slots = n_blocks * MOE_BLK
    slot_tok = jnp.zeros((n_slots,), jnp.int32).at[dest].set(jnp.arange(n_assign, dtype=jnp.int32) // TOP_K)
    block_e = jnp.minimum(jnp.searchsorted(pad_ends, jnp.arange(n_blocks, dtype=jnp.int32) * MOE_BLK, side="right"),
                          N_EXPERTS - 1).astype(jnp.int32)
    n_used = (pad_ends[-1] // MOE_BLK).astype(jnp.int32).reshape(1)
    dest2 = dest.reshape(n_tok, TOP_K)
    return dest2[:, 0], dest2[:, 1], top_w[:, 0:1], top_w[:, 1:2], slot_tok, block_e, n_used


def _rope_tables(n_lat, n_ctx, dim):
    rows = n_lat // GRID_W
    row = jnp.broadcast_to(jnp.arange(rows)[:, None], (rows, GRID_W)).reshape(-1).astype(F32)
    col = jnp.broadcast_to(jnp.arange(GRID_W)[None, :], (rows, GRID_W)).reshape(-1).astype(F32)
    quarter = dim // 4
    inv = ROPE_THETA ** (-jnp.arange(quarter, dtype=F32) / quarter)
    ang = jnp.concatenate([row[:, None] * inv, col[:, None] * inv], axis=-1)
    cos, sin = jnp.cos(ang), jnp.sin(ang)
    reps = LANES // dim
    c = jnp.tile(jnp.concatenate([cos, cos], axis=-1), (1, reps))
    s = jnp.tile(jnp.concatenate([-sin, sin], axis=-1), (1, reps))
    c = jnp.concatenate([c, jnp.ones((n_ctx, LANES), F32)], axis=0)
    s = jnp.concatenate([s, jnp.zeros((n_ctx, LANES), F32)], axis=0)
    return c, s


def kernel(x, c, ctx, c_ctx, w_mod, b_mod, attn_norm, ffn_norm, w_in, w_out, diff_q_norm, diff_k_norm, diff_lambda, diff_out_norm, gqa_q_norm, gqa_k_norm, mlstm_conv_w, mlstm_conv_b, mlstm_gate_b, mlstm_out_norm, ffn_w_gate, ffn_w_up, ffn_w_down, router_w, router_b, exp_w_gate, exp_w_up, exp_w_down):
    batch, n_lat, d = x.shape
    n_ctx = ctx.shape[1]
    depth = w_in.shape[0]
    assert batch == 1 and n_ctx == ROW_TILE and n_lat % MLSTM_CHUNK == 0 and n_lat % GRID_W == 0
    t_all = n_lat + n_ctx
    hd = MLSTM_HEADS * MLSTM_DIM

    cc = jnp.zeros((8, d), F32).at[0].set(c[0]).at[1].set(c_ctx)
    mods_all = _mods(cc, w_mod, b_mod)
    rope_diff = _rope_tables(n_lat, n_ctx, DIFF_QK_DIM)
    rope_gqa = _rope_tables(n_lat, n_ctx, GQA_DIM)
    pos = jnp.arange(MLSTM_CHUNK)
    vis_fwd = (pos[None, :] <= pos[:, None]).astype(F32)
    vis_bwd = (pos[None, :] >= pos[:, None]).astype(F32)

    xa = jnp.concatenate([x[0], ctx[0]], axis=0)
    for l in range(depth):
        need_ctx = l < depth - 1
        rows = t_all if need_ctx else n_lat
        lam_init = 0.8 - 0.6 * math.exp(-0.3 * l)
        mods = mods_all[l]

        h = _modnorm(xa, attn_norm[l], mods, 0, n_lat)
        w_in_l = w_in[l]
        proj = _matmul([h], [w_in_l[:, :_MAIN_WIDTH].astype(BF16)], rows=t_all, tm=_row_tile(t_all), tn=512,
                       out_dtype=BF16, name="in_proj")
        w_gate_cols = jnp.pad(w_in_l[:, _MAIN_WIDTH:], ((0, 0), (0, LANES - 4 * MLSTM_HEADS))).astype(BF16)
        gates = _matmul([h], [w_gate_cols], rows=t_all, tm=_row_tile(t_all), tn=LANES, name="gate_proj")

        dq = _qkprep(proj, _DQ, DIFF_HEADS, diff_q_norm[l], *rope_diff, seg=DIFF_QK_DIM,
                     scale=DIFF_QK_DIM ** -0.5, split=True)
        dk = _qkprep(proj, _DK, DIFF_HEADS, diff_k_norm[l], *rope_diff, seg=DIFF_QK_DIM, scale=1.0, split=False)
        o2 = _attention(dq, dk, proj, _DV, n_kv=DIFF_HEADS, n_q_heads=2, n_lat=n_lat, n_ctx=n_ctx,
                        q_rows=rows, out_dtype=F32)
        mix_d = _diff_finish(o2, diff_lambda[l], diff_out_norm[l], lam_init)

        gq = _qkprep(proj, _GQ, GQA_HEADS, gqa_q_norm[l], *rope_gqa, seg=GQA_DIM, scale=GQA_DIM ** -0.5, split=False)
        gk = _qkprep(proj, _GK, GQA_KV_HEADS, gqa_k_norm[l], *rope_gqa, seg=GQA_DIM, scale=1.0, split=False)
        mix_g = _attention(gq, gk, proj, _GV, n_kv=GQA_KV_HEADS, n_q_heads=GQA_HEADS // GQA_KV_HEADS,
                           n_lat=n_lat, n_ctx=n_ctx, q_rows=rows, out_dtype=BF16)

        qk = _mlstm_conv(proj, mlstm_conv_w[l], mlstm_conv_b[l], n_lat)
        kt = qk[:, hd:].T
        g_t = gates[:, :4 * MLSTM_HEADS].T.reshape(4, MLSTM_HEADS, t_all)
        g_t = jnp.pad(g_t, ((0, 0), (0, 16 - MLSTM_HEADS), (0, 0)))
        gb = jnp.pad(mlstm_gate_b[l].astype(F32), ((0, 0), (0, 16 - MLSTM_HEADS)))[:, :, None]
        mv, mo = proj[:, _MV:_MO], proj[:, _MO:_MAIN_WIDTH]
        h_f = _mlstm_direction(qk, kt, mv, g_t[0], g_t[1], gb[0], gb[1], vis_fwd, n_lat=n_lat, reverse=False)
        h_b = _mlstm_direction(qk, kt, mv, g_t[2], g_t[3], gb[2], gb[3], vis_bwd, n_lat=n_lat, reverse=True)
        mix_m = _mlstm_finish(h_f, h_b, mo, mlstm_out_norm[l], rows)

        w_out_l = w_out[l]
        w_parts = [w_out_l[:1024].astype(BF16), w_out_l[1024:2560].astype(BF16), w_out_l[2560:].astype(BF16)]
        x1 = _matmul([mix_d, mix_g, mix_m], w_parts, rows=rows, tm=_row_tile(rows), tn=512, kind="resid",
                     resid=(xa, mods, 2), n_lat=n_lat, name="out_proj")

        if l % 2 == 0:
            tok = _modnorm(x1, ffn_norm[l], mods, 3, n_lat)
            hid = _matmul([tok], [ffn_w_gate[l // 2].astype(BF16), ffn_w_up[l // 2].astype(BF16)], rows=rows,
                          tm=_row_tile(rows), tn=256, kind="swiglu", out_dtype=BF16, name="ffn_up")
            xa = _matmul([hid], [ffn_w_down[l // 2].astype(BF16)], rows=rows, tm=_row_tile(rows), tn=256,
                         kind="resid", resid=(x1, mods, 5), n_lat=n_lat, name="ffn_down")
        else:
            assert not need_ctx
            rw = jnp.pad(router_w[l // 2].astype(F32), ((0, 0), (0, LANES - N_EXPERTS)))
            rb = jnp.pad(router_b[l // 2].astype(F32), (0, LANES - N_EXPERTS)).reshape(1, LANES)
            tok, logits = _modnorm(x1, ffn_norm[l], mods, 3, n_lat, router=(rw, rb))
            d0, d1, w0, w1, slot_tok, block_e, n_used = _moe_routing(logits[:, :N_EXPERTS], rows)
            xs = _moe_gather(tok.reshape(rows, 16, d // 16), slot_tok).reshape(-1, d)
            hid = _expert_matmul(xs, [exp_w_gate[l // 2].astype(BF16), exp_w_up[l // 2].astype(BF16)], block_e,
                                 n_used, tn=512, kind="swiglu", out_dtype=BF16, name="moe_up")
            ys = _expert_matmul(hid, [exp_w_down[l // 2].astype(BF16)], block_e, n_used, tn=512, kind="plain",
                                out_dtype=F32, name="moe_down")
            xa = _moe_combine(ys, x1, d0, d1, w0, w1, mods, 5)
    return xa[:n_lat].reshape(batch, n_lat, d)
```

```python
import functools
import math

import jax
import jax.numpy as jnp
from jax import lax
from jax.experimental import pallas as pl
from jax.experimental.pallas import tpu as pltpu

F32 = jnp.float32
BF16 = jnp.bfloat16
HIGHEST = lax.Precision.HIGHEST

GRID_W = 64
ROPE_THETA = 10000.0
NORM_EPS = 1e-6
LOG2E = math.log2(math.e)

DIFF_HEADS = 8
DIFF_QK_DIM = 64
GQA_HEADS = 12
GQA_KV_HEADS = 4
GQA_DIM = 128
MLSTM_HEADS = 12
MLSTM_DIM = 128
N_EXPERTS = 8
TOP_K = 2

LANES = 128
ROW_TILE = 256
KEY_TILE = 512
KEY_UNROLL = 8
MLSTM_CHUNK = 256
MOE_BLK = 512
GATHER_ROWS = 256
VMEM_CAP = 60 * 1024 * 1024

_DQ, _DK, _DV = 0, 1024, 2048
_GQ, _GK, _GV = 3072, 4608, 5120
_MQ, _MK, _MV, _MO, _MG = 5632, 7168, 8704, 10240, 11776
_MAIN_WIDTH = 11776


def _cparams(sem, vmem_bytes):
    limit = int(min(VMEM_CAP, max(32 * 1024 * 1024, vmem_bytes + (8 << 20))))
    return pltpu.CompilerParams(dimension_semantics=sem, vmem_limit_bytes=limit)


def _row_tile(m, cands=(1024, 768, 512, 256)):
    for t in cands:
        if m % t == 0:
            return t
    raise ValueError(f"no row tile for {m}")


def _log_sigmoid(x):
    return jnp.minimum(x, 0.0) - jnp.log1p(jnp.exp(-jnp.abs(x)))


def _silu(x):
    return x * jax.nn.sigmoid(x)


def _mod_kernel(c_ref, w_ref, b_ref, o_ref):
    s = _silu(c_ref[...])
    o_ref[0] = jnp.dot(s, w_ref[0], preferred_element_type=F32) + b_ref[0]


def _mods(cc, w_mod, b_mod):
    n_layers, d, n6 = w_mod.shape
    tn = 512
    return pl.pallas_call(
        _mod_kernel,
        grid=(n_layers, n6 // tn),
        in_specs=[pl.BlockSpec((8, d), lambda l, j: (0, 0)),
                  pl.BlockSpec((1, d, tn), lambda l, j: (l, 0, j)),
                  pl.BlockSpec((1, 1, tn), lambda l, j: (l, 0, j))],
        out_specs=pl.BlockSpec((1, 8, tn), lambda l, j: (l, 0, j)),
        out_shape=jax.ShapeDtypeStruct((n_layers, 8, n6), F32),
        compiler_params=_cparams(("parallel", "parallel"), 2 * d * tn * 4),
        name="adaln_mods",
    )(cc, w_mod, b_mod.reshape(n_layers, 1, n6))


def _modnorm_kernel(x_ref, g_ref, sh_ref, sc_ref, *rest, n_lat, tm, with_router):
    x = x_ref[...]
    ms = jnp.mean(x * x, axis=-1, keepdims=True)
    y = x * lax.rsqrt(ms + NORM_EPS) * g_ref[...]
    rows = pl.program_id(0) * tm + lax.broadcasted_iota(jnp.int32, (tm, 1), 0)
    is_ctx = rows >= n_lat
    sc = jnp.where(is_ctx, sc_ref[1:2, :], sc_ref[0:1, :])
    sh = jnp.where(is_ctx, sh_ref[1:2, :], sh_ref[0:1, :])
    h = y * (1.0 + sc) + sh
    if with_router:
        rw_ref, rb_ref, o_ref, lg_ref = rest
        lg_ref[...] = jnp.dot(h, rw_ref[...], preferred_element_type=F32, precision=HIGHEST) + rb_ref[...]
    else:
        (o_ref,) = rest
    o_ref[...] = h.astype(o_ref.dtype)


def _modnorm(x, gain, mods, shift_idx, n_lat, router=None, out_dtype=BF16):
    m, d = x.shape
    tm = ROW_TILE
    in_specs = [pl.BlockSpec((tm, d), lambda i: (i, 0)),
                pl.BlockSpec((1, d), lambda i: (0, 0)),
                pl.BlockSpec((8, d), lambda i: (0, shift_idx)),
                pl.BlockSpec((8, d), lambda i: (0, shift_idx + 1))]
    args = [x, gain.reshape(1, d), mods, mods]
    out_specs = pl.BlockSpec((tm, d), lambda i: (i, 0))
    out_shape = jax.ShapeDtypeStruct((m, d), out_dtype)
    if router is not None:
        rw, rb = router
        in_specs += [pl.BlockSpec((d, LANES), lambda i: (0, 0)),
                     pl.BlockSpec((1, LANES), lambda i: (0, 0))]
        args += [rw, rb]
        out_specs = [out_specs, pl.BlockSpec((tm, LANES), lambda i: (i, 0))]
        out_shape = [out_shape, jax.ShapeDtypeStruct((m, LANES), F32)]
    return pl.pallas_call(
        functools.partial(_modnorm_kernel, n_lat=n_lat, tm=tm, with_router=router is not None),
        grid=(m // tm,),
        in_specs=in_specs, out_specs=out_specs, out_shape=out_shape,
        compiler_params=_cparams(("parallel",), 2 * tm * d * 6 + 4 * tm * d * 4 + d * LANES * 8),
        name="modnorm",
    )(*args)


def _mm_kernel(*refs, n_a, n_w, kind, n_lat, tm):
    a_refs, w_refs, rest = refs[:n_a], refs[n_a:n_a + n_w], refs[n_a + n_w:]

    def w_of(r):
        return r[0] if len(r.shape) == 3 else r[...]

    if kind == "swiglu":
        a = a_refs[0][...]
        g = jnp.dot(a, w_of(w_refs[0]), preferred_element_type=F32)
        u = jnp.dot(a, w_of(w_refs[1]), preferred_element_type=F32)
        (o_ref,) = rest
        o_ref[...] = (_silu(g) * u).astype(o_ref.dtype)
        return
    acc = jnp.dot(a_refs[0][...], w_of(w_refs[0]), preferred_element_type=F32)
    for a_ref, w_ref in zip(a_refs[1:], w_refs[1:]):
        acc = acc + jnp.dot(a_ref[...], w_of(w_ref), preferred_element_type=F32)
    if kind == "plain":
        (o_ref,) = rest
        o_ref[...] = acc.astype(o_ref.dtype)
    else:
        x_ref, g_ref, o_ref = rest
        rows = pl.program_id(0) * tm + lax.broadcasted_iota(jnp.int32, (tm, 1), 0)
        gate = jnp.where(rows >= n_lat, g_ref[1:2, :], g_ref[0:1, :])
        o_ref[...] = x_ref[...] + gate * acc


def _matmul(a_list, w_list, *, rows, tm, tn, kind="plain", out_dtype=F32, resid=None, n_lat=0, name="matmul"):
    n = w_list[0].shape[1]
    in_specs = [pl.BlockSpec((tm, a.shape[1]), lambda i, j: (i, 0)) for a in a_list]
    in_specs += [pl.BlockSpec((w.shape[0], tn), lambda i, j: (0, j)) for w in w_list]
    args = list(a_list) + list(w_list)
    vmem = sum(2 * tm * a.shape[1] * 2 for a in a_list) + sum(2 * w.shape[0] * tn * 2 for w in w_list)
    vmem += 6 * tm * tn * 4
    if kind == "resid":
        x, mods, gate_idx = resid
        d = x.shape[1]
        in_specs += [pl.BlockSpec((tm, tn), lambda i, j: (i, j)),
                     pl.BlockSpec((8, tn), lambda i, j: (0, gate_idx * (d // tn) + j))]
        args += [x, mods]
    return pl.pallas_call(
        functools.partial(_mm_kernel, n_a=len(a_list), n_w=len(w_list), kind=kind, n_lat=n_lat, tm=tm),
        grid=(rows // tm, n // tn),
        in_specs=in_specs,
        out_specs=pl.BlockSpec((tm, tn), lambda i, j: (i, j)),
        out_shape=jax.ShapeDtypeStruct((rows, n), out_dtype),
        compiler_params=_cparams(("parallel", "arbitrary"), vmem),
        name=name,
    )(*args)


def _qkprep_kernel(x_ref, g_ref, c_ref, s_ref, o_ref, *, n_tiles, seg, scale, split):
    lane = lax.broadcasted_iota(jnp.int32, c_ref.shape, 1)
    gain, cos, sin = g_ref[...], c_ref[...], s_ref[...]
    half = seg // 2
    ow = 2 * LANES if split else LANES
    for j in range(n_tiles):
        x = x_ref[:, j * LANES:(j + 1) * LANES].astype(F32)
        x2 = x * x
        if seg == LANES:
            ms = jnp.sum(x2, axis=-1, keepdims=True) * (1.0 / seg)
        else:
            lo = lane < seg
            s_lo = jnp.sum(jnp.where(lo, x2, 0.0), axis=-1, keepdims=True)
            s_hi = jnp.sum(jnp.where(lo, 0.0, x2), axis=-1, keepdims=True)
            ms = jnp.where(lo, s_lo, s_hi) * (1.0 / seg)
        y = x * lax.rsqrt(ms + NORM_EPS) * gain
        if seg == LANES:
            partner = pltpu.roll(y, half, 1)
        else:
            partner = jnp.where((lane % seg) < half, pltpu.roll(y, LANES - half, 1), pltpu.roll(y, half, 1))
        y = (y * cos + partner * sin) * scale
        if split:
            o_ref[:, j * ow:j * ow + LANES] = jnp.where(lane < seg, y, 0.0).astype(o_ref.dtype)
            o_ref[:, j * ow + LANES:(j + 1) * ow] = jnp.where(lane < seg, 0.0, y).astype(o_ref.dtype)
        else:
            o_ref[:, j * ow:(j + 1) * ow] = y.astype(o_ref.dtype)


def _qkprep(proj, col0, n_tiles, gain, rope_c, rope_s, *, seg, scale, split):
    t = proj.shape[0]
    tm = ROW_TILE
    width = n_tiles * LANES
    assert col0 % width == 0
    cb = col0 // width
    g = jnp.tile(gain.astype(F32), LANES // seg).reshape(1, LANES)
    ow = (2 if split else 1) * width
    return pl.pallas_call(
        functools.partial(_qkprep_kernel, n_tiles=n_tiles, seg=seg, scale=scale, split=split),
        grid=(t // tm,),
        in_specs=[pl.BlockSpec((tm, width), lambda i: (i, cb)),
                  pl.BlockSpec((1, LANES), lambda i: (0, 0)),
                  pl.BlockSpec((tm, LANES), lambda i: (i, 0)),
                  pl.BlockSpec((tm, LANES), lambda i: (i, 0))],
        out_specs=pl.BlockSpec((tm, ow), lambda i: (i, 0)),
        out_shape=jax.ShapeDtypeStruct((t, ow), BF16),
        compiler_params=_cparams(("parallel",), 0),
        name="qk_prep",
    )(proj, g, rope_c, rope_s)


def _attn_kernel(q_ref, k_ref, vt_ref, *rest, n_q_heads, n_lat, n_ctx, tq, tk, lam_init):
    is_ctx = pl.program_id(1) * tq >= n_lat
    qs = [q_ref[:, g * LANES:(g + 1) * LANES] for g in range(n_q_heads)]

    def scores(kc):
        return tuple(lax.dot_general(kc, q, (((1,), (1,)), ((), ())), preferred_element_type=F32) for q in qs)

    def update(ss, vtc, carry):
        new = []
        for s, (m, l, acc) in zip(ss, carry):
            m_new = jnp.maximum(m, jnp.max(s, axis=0, keepdims=True))
            alpha = jnp.exp2(m - m_new)
            p = jnp.exp2(s - m_new)
            pv = jnp.dot(vtc, p.astype(BF16), preferred_element_type=F32)
            new.append((m_new, alpha * l + jnp.sum(p, axis=0, keepdims=True), alpha * acc + pv))
        return tuple(new)

    carry = tuple((jnp.full((1, tq), -jnp.inf, F32), jnp.zeros((1, tq), F32), jnp.zeros((LANES, tq), F32))
                  for _ in range(n_q_heads))
    carry = update(scores(k_ref[n_lat:n_lat + n_ctx, :]), vt_ref[:, n_lat:n_lat + n_ctx], carry)

    n_chunks = n_lat // tk
    unroll = math.gcd(n_chunks, KEY_UNROLL)

    def body(c, state):
        ss, carry = state
        for u in range(unroll):
            idx = c * unroll + u
            nxt = pl.multiple_of(jnp.minimum(idx + 1, n_chunks - 1) * tk, tk)
            ss_next = scores(k_ref[pl.ds(nxt, tk), :])
            carry = update(ss, vt_ref[:, pl.ds(pl.multiple_of(idx * tk, tk), tk)], carry)
            ss = ss_next
        return ss, carry

    _, carry = lax.fori_loop(0, jnp.where(is_ctx, 0, n_chunks // unroll), body, (scores(k_ref[0:tk, :]), carry))
    outs_t = [acc / l for _, l, acc in carry]
    if lam_init is None:
        (o_ref,) = rest
        for g in range(n_q_heads):
            o_ref[:, g * LANES:(g + 1) * LANES] = outs_t[g].T.astype(o_ref.dtype)
    else:
        lam_ref, g_ref, o_ref = rest
        lp = lam_ref[...]
        lam = (jnp.exp(jnp.sum(lp[0:1, :] * lp[1:2, :], axis=-1, keepdims=True))
               - jnp.exp(jnp.sum(lp[2:3, :] * lp[3:4, :], axis=-1, keepdims=True)) + lam_init)
        o = (outs_t[0] - lam * outs_t[1]).T
        ms = jnp.mean(o * o, axis=-1, keepdims=True)
        o_ref[...] = (o * lax.rsqrt(ms + NORM_EPS) * g_ref[...] * (1.0 - lam_init)).astype(o_ref.dtype)


def _attention(q, k, vt, *, n_kv, n_q_heads, n_lat, n_ctx, q_rows, diff=None):
    t = k.shape[0]
    tq = ROW_TILE
    tk = min(KEY_TILE, n_lat)
    assert n_lat % tk == 0
    qw = n_q_heads * LANES
    ow = LANES if diff is not None else qw
    in_specs = [pl.BlockSpec((tq, qw), lambda h, i: (i, h)),
                pl.BlockSpec((t, LANES), lambda h, i: (0, h)),
                pl.BlockSpec((LANES, t), lambda h, i: (h, 0))]
    args = [q, k, vt]
    lam_init = None
    if diff is not None:
        lam_p, gain, lam_init = diff
        in_specs += [pl.BlockSpec((4, DIFF_QK_DIM), lambda h, i: (0, 0)),
                     pl.BlockSpec((1, LANES), lambda h, i: (0, 0))]
        args += [lam_p.astype(F32), gain.astype(F32).reshape(1, LANES)]
    vmem = 2 * 2 * t * LANES * 2 + 4 * tq * qw * 4 + 6 * n_q_heads * tq * tk * 4
    return pl.pallas_call(
        functools.partial(_attn_kernel, n_q_heads=n_q_heads, n_lat=n_lat, n_ctx=n_ctx, tq=tq, tk=tk,
                          lam_init=lam_init),
        grid=(n_kv, q_rows // tq),
        in_specs=in_specs,
        out_specs=pl.BlockSpec((tq, ow), lambda h, i: (i, h)),
        out_shape=jax.ShapeDtypeStruct((q_rows, n_kv * ow), BF16),
        compiler_params=_cparams(("parallel", "arbitrary"), vmem),
        name="attention",
    )(*args)


def _conv_kernel(x_ref, xp_ref, xn_ref, w_ref, b_ref, o_ref, *, n_lat, t_all, tm, k_col_tile):
    i, j = pl.program_id(0), pl.program_id(1)
    x = x_ref[...].astype(F32)
    row = lax.broadcasted_iota(jnp.int32, (tm, 1), 0)
    grow = i * tm + row
    prev = jnp.where(row == 0, xp_ref[15:16, :].astype(F32), pltpu.roll(x, 1, 0))
    nxt = jnp.where(row == tm - 1, xn_ref[0:1, :].astype(F32), pltpu.roll(x, tm - 1, 0))
    prev = jnp.where((grow == 0) | (grow == n_lat), 0.0, prev)
    nxt = jnp.where((grow == n_lat - 1) | (grow == t_all - 1), 0.0, nxt)
    y = b_ref[...] + prev * w_ref[0:1, :] + x * w_ref[1:2, :] + nxt * w_ref[2:3, :]
    y = _silu(y) * jnp.where(j >= k_col_tile, MLSTM_DIM ** -0.5, 1.0)
    o_ref[...] = y.astype(o_ref.dtype)


def _mlstm_conv(proj, conv_w, conv_b, n_lat):
    t = proj.shape[0]
    tm, tc, hb = ROW_TILE, 512, 16
    width = 2 * MLSTM_HEADS * MLSTM_DIM
    cb = _MQ // tc
    nrb = t // hb
    return pl.pallas_call(
        functools.partial(_conv_kernel, n_lat=n_lat, t_all=t, tm=tm, k_col_tile=(width // 2) // tc),
        grid=(t // tm, width // tc),
        in_specs=[pl.BlockSpec((tm, tc), lambda i, j: (i, cb + j)),
                  pl.BlockSpec((hb, tc), lambda i, j: (jnp.maximum(i * (tm // hb) - 1, 0), cb + j)),
                  pl.BlockSpec((hb, tc), lambda i, j: (jnp.minimum((i + 1) * (tm // hb), nrb - 1), cb + j)),
                  pl.BlockSpec((3, tc), lambda i, j: (0, j)),
                  pl.BlockSpec((1, tc), lambda i, j: (0, j))],
        out_specs=pl.BlockSpec((tm, tc), lambda i, j: (i, j)),
        out_shape=jax.ShapeDtypeStruct((t, width), BF16),
        compiler_params=_cparams(("parallel", "arbitrary"), 0),
        name="mlstm_conv",
    )(proj, proj, proj, conv_w.astype(F32), conv_b.astype(F32).reshape(1, width))


def _mlstm_kernel(q_ref, k_ref, kt_ref, v_ref, gi_ref, gf_ref, bi_ref, bf_ref, mask_ref, h_ref,
                  c_sc, n_sc, m_sc, *, n_heads):
    @pl.when(pl.program_id(0) == 0)
    def _():
        c_sc[...] = jnp.zeros_like(c_sc)
        n_sc[...] = jnp.zeros_like(n_sc)
        m_sc[...] = jnp.zeros_like(m_sc)

    vis = mask_ref[...]
    li_all = gi_ref[...] + bi_ref[...]
    lf_all = _log_sigmoid(gf_ref[...] + bf_ref[...])
    br_all = lax.dot_general(lf_all, vis, (((1,), (1,)), ((), ())), preferred_element_type=F32, precision=HIGHEST)
    bend_all = jnp.sum(lf_all, axis=1, keepdims=True)
    d = MLSTM_DIM
    for h in range(n_heads):
        lir, lfr, br = li_all[h:h + 1, :], lf_all[h:h + 1, :], br_all[h:h + 1, :]
        b_end = bend_all[h:h + 1, :]
        m_prev = m_sc[h:h + 1, 0:1]
        bc = jnp.sum(vis * lfr, axis=1, keepdims=True)
        lw = b_end - br + lir
        m_new = jnp.maximum(b_end + m_prev, jnp.max(lw, axis=1, keepdims=True))
        w_end = jnp.exp(lw - m_new)
        decay = jnp.exp(b_end + m_prev - m_new)
        log_d = jnp.where(vis > 0.0, bc - br + lir, -jnp.inf)
        m_t = jnp.maximum(bc + m_prev, jnp.max(log_d, axis=1, keepdims=True))
        dmat = jnp.exp(log_d - m_t)
        inter = jnp.exp(bc + m_prev - m_t)
        qh = q_ref[:, h * d:(h + 1) * d]
        kh = k_ref[:, h * d:(h + 1) * d]
        vh = v_ref[:, h * d:(h + 1) * d]
        kth = kt_ref[h * d:(h + 1) * d, :]
        c_prev = c_sc[h]
        n_prev = n_sc[h:h + 1, :]
        sc = jnp.dot(qh, kth, preferred_element_type=F32) * dmat
        num = (jnp.dot(sc.astype(BF16), vh, preferred_element_type=F32)
               + inter * jnp.dot(qh, c_prev.astype(BF16), preferred_element_type=F32))
        den = (jnp.sum(sc, axis=1, keepdims=True)
               + inter * jnp.sum(qh.astype(F32) * n_prev, axis=1, keepdims=True))
        h_ref[:, h * d:(h + 1) * d] = num / jnp.maximum(jnp.abs(den), jnp.exp(-m_t))
        ktw = (kth.astype(F32) * w_end).astype(BF16)
        c_sc[h] = decay * c_prev + jnp.dot(ktw, vh, preferred_element_type=F32)
        w8 = jnp.broadcast_to(w_end, (8, w_end.shape[1])).astype(BF16)
        n_sc[h:h + 1, :] = decay * n_prev + jnp.dot(w8, kh, preferred_element_type=F32)[0:1, :]
        m_sc[h:h + 1, :] = jnp.broadcast_to(m_new, (1, LANES))


def _mlstm_direction(qk, kt, v, gi, gf, bi, bf, vis, *, n_lat, reverse):
    t = qk.shape[0]
    lc = MLSTM_CHUNK
    n_chunks = t // lc
    ctx_blk = n_lat // lc
    hd = MLSTM_HEADS * MLSTM_DIM

    def blk(c):
        lat = (n_chunks - 1 - c) if reverse else (c - 1)
        return jnp.where(c == 0, ctx_blk, lat)

    return pl.pallas_call(
        functools.partial(_mlstm_kernel, n_heads=MLSTM_HEADS),
        grid=(n_chunks,),
        in_specs=[pl.BlockSpec((lc, hd), lambda c: (blk(c), 0)),
                  pl.BlockSpec((lc, hd), lambda c: (blk(c), 1)),
                  pl.BlockSpec((hd, lc), lambda c: (0, blk(c))),
                  pl.BlockSpec((lc, hd), lambda c: (blk(c), 0)),
                  pl.BlockSpec((16, lc), lambda c: (0, blk(c))),
                  pl.BlockSpec((16, lc), lambda c: (0, blk(c))),
                  pl.BlockSpec((16, 1), lambda c: (0, 0)),
                  pl.BlockSpec((16, 1), lambda c: (0, 0)),
                  pl.BlockSpec((lc, lc), lambda c: (0, 0))],
        out_specs=pl.BlockSpec((lc, hd), lambda c: (blk(c), 0)),
        out_shape=jax.ShapeDtypeStruct((t, hd), F32),
        scratch_shapes=[pltpu.VMEM((MLSTM_HEADS, MLSTM_DIM, MLSTM_DIM), F32),
                        pltpu.VMEM((16, LANES), F32),
                        pltpu.VMEM((16, LANES), F32)],
        compiler_params=_cparams(("arbitrary",), 16 << 20),
        name="mlstm_bwd" if reverse else "mlstm_fwd",
    )(qk, qk, kt, v, gi, gf, bi, bf, vis)


def _mlstm_finish_kernel(hf_ref, hb_ref, o_ref, g_ref, out_ref, *, n_heads):
    d = MLSTM_DIM
    for h in range(n_heads):
        sl = slice(h * d, (h + 1) * d)
        s = hf_ref[:, sl] + hb_ref[:, sl]
        ms = jnp.mean(s * s, axis=-1, keepdims=True)
        y = s * lax.rsqrt(ms + NORM_EPS) * g_ref[:, sl]
        out_ref[:, sl] = (jax.nn.sigmoid(o_ref[:, sl].astype(F32)) * y).astype(out_ref.dtype)


def _mlstm_finish(hf, hb, o_gate, out_gain, rows):
    hd = MLSTM_HEADS * MLSTM_DIM
    tm = ROW_TILE
    return pl.pallas_call(
        functools.partial(_mlstm_finish_kernel, n_heads=MLSTM_HEADS),
        grid=(rows // tm,),
        in_specs=[pl.BlockSpec((tm, hd), lambda i: (i, 0)),
                  pl.BlockSpec((tm, hd), lambda i: (i, 0)),
                  pl.BlockSpec((tm, hd), lambda i: (i, 0)),
                  pl.BlockSpec((1, hd), lambda i: (0, 0))],
        out_specs=pl.BlockSpec((tm, hd), lambda i: (i, 0)),
        out_shape=jax.ShapeDtypeStruct((rows, hd), BF16),
        compiler_params=_cparams(("parallel",), 0),
        name="mlstm_finish",
    )(hf, hb, o_gate, out_gain.astype(F32).reshape(1, hd))


def _moe_up_kernel(be_ref, nu_ref, st_ref, tok_ref, wg_ref, wu_ref, o_ref, xbuf, xb, sem):
    b, j = pl.program_id(0), pl.program_id(1)
    n_used = nu_ref[0]

    def copy(blk, slot, r):
        src = tok_ref.at[pl.ds(st_ref[blk * MOE_BLK + r], 1), :]
        return pltpu.make_async_copy(src, xbuf.at[slot, pl.ds(r, 1), :], sem.at[slot])

    def start_block(blk, slot):
        def body(r, carry):
            copy(blk, slot, r).start()
            return carry
        lax.fori_loop(0, MOE_BLK, body, 0)

    def wait_block(blk, slot):
        def body(r, carry):
            copy(blk, slot, r).wait()
            return carry
        lax.fori_loop(0, MOE_BLK, body, 0)

    @pl.when((b < n_used) & (j == 0))
    def _():
        slot = b % 2

        @pl.when(b == 0)
        def _():
            start_block(0, 0)

        wait_block(b, slot)
        xb[...] = xbuf[slot].astype(BF16)

        @pl.when(b + 1 < n_used)
        def _():
            start_block(b + 1, 1 - slot)

    @pl.when(b < n_used)
    def _():
        a = xb[...]
        g = jnp.dot(a, wg_ref[0], preferred_element_type=F32)
        u = jnp.dot(a, wu_ref[0], preferred_element_type=F32)
        o_ref[...] = (_silu(g) * u).astype(o_ref.dtype)

    @pl.when(b >= n_used)
    def _():
        o_ref[...] = jnp.zeros_like(o_ref)


def _moe_up(tok, w_gate, w_up, block_e, n_used, slot_tok, *, tn):
    n_slots = slot_tok.shape[0]
    k = tok.shape[1]
    n = w_gate.shape[2]
    nb, nj = n_slots // MOE_BLK, n // tn

    def w_map(b, j, be, nu, st):
        return (be[b], 0, jnp.where(b < nu[0], j, nj - 1))

    return pl.pallas_call(
        _moe_up_kernel,
        grid_spec=pltpu.PrefetchScalarGridSpec(
            num_scalar_prefetch=3,
            grid=(nb, nj),
            in_specs=[pl.BlockSpec(memory_space=pl.ANY),
                      pl.BlockSpec((1, k, tn), w_map),
                      pl.BlockSpec((1, k, tn), w_map)],
            out_specs=pl.BlockSpec((MOE_BLK, tn), lambda b, j, be, nu, st: (b, j)),
            scratch_shapes=[pltpu.VMEM((2, MOE_BLK, k), F32),
                            pltpu.VMEM((MOE_BLK, k), BF16),
                            pltpu.SemaphoreType.DMA((2,))]),
        out_shape=jax.ShapeDtypeStruct((n_slots, n), BF16),
        compiler_params=_cparams(("arbitrary", "arbitrary"),
                                 2 * MOE_BLK * k * 4 + MOE_BLK * k * 2 + 4 * k * tn * 2 + 6 * MOE_BLK * tn * 4),
        name="moe_up",
    )(block_e, n_used, slot_tok, tok, w_gate, w_up)


def _expert_kernel(be_ref, nu_ref, *refs, kind):
    o_ref = refs[-1]
    b = pl.program_id(0)

    @pl.when(b < nu_ref[0])
    def _():
        _mm_kernel(*refs, n_a=1, n_w=len(refs) - 2, kind=kind, n_lat=0, tm=0)

    @pl.when(b >= nu_ref[0])
    def _():
        o_ref[...] = jnp.zeros_like(o_ref)


def _expert_matmul(a, w_list, block_e, n_used, *, tn, kind, out_dtype, name):
    n_slots, k = a.shape
    n = w_list[0].shape[2]
    nb, nj = n_slots // MOE_BLK, n // tn

    def a_map(b, j, be, nu):
        return (jnp.minimum(b, nu[0] - 1), 0)

    def w_map(b, j, be, nu):
        return (be[b], 0, jnp.where(b < nu[0], j, nj - 1))

    return pl.pallas_call(
        functools.partial(_expert_kernel, kind=kind),
        grid_spec=pltpu.PrefetchScalarGridSpec(
            num_scalar_prefetch=2,
            grid=(nb, nj),
            in_specs=[pl.BlockSpec((MOE_BLK, k), a_map)] + [pl.BlockSpec((1, k, tn), w_map) for _ in w_list],
            out_specs=pl.BlockSpec((MOE_BLK, tn), lambda b, j, be, nu: (b, j))),
        out_shape=jax.ShapeDtypeStruct((n_slots, n), out_dtype),
        compiler_params=_cparams(("arbitrary", "arbitrary"),
                                 2 * MOE_BLK * k * 2 + len(w_list) * 2 * k * tn * 2 + 6 * MOE_BLK * tn * 4),
        name=name,
    )(block_e, n_used, a, *w_list)


def _combine_kernel(d0_ref, d1_ref, y_ref, x_ref, w0_ref, w1_ref, g_ref, o_ref, buf, sem, *, rows):
    base = pl.program_id(0) * rows

    def copy(r, which):
        idx = d0_ref[base + r] if which == 0 else d1_ref[base + r]
        return pltpu.make_async_copy(y_ref.at[pl.ds(idx, 1), :], buf.at[which, pl.ds(r, 1), :], sem)

    def start(r, carry):
        copy(r, 0).start()
        copy(r, 1).start()
        return carry

    def wait(r, carry):
        copy(r, 0).wait()
        copy(r, 1).wait()
        return carry

    lax.fori_loop(0, rows, start, 0)
    lax.fori_loop(0, rows, wait, 0)
    y = w0_ref[...] * buf[0] + w1_ref[...] * buf[1]
    o_ref[...] = x_ref[...] + g_ref[0:1, :] * y


def _moe_combine(y_slots, x, d0, d1, w0, w1, mods, gate_idx):
    n_tok, d = x.shape
    rows = GATHER_ROWS // 2
    return pl.pallas_call(
        functools.partial(_combine_kernel, rows=rows),
        grid_spec=pltpu.PrefetchScalarGridSpec(
            num_scalar_prefetch=2,
            grid=(n_tok // rows,),
            in_specs=[pl.BlockSpec(memory_space=pl.ANY),
                      pl.BlockSpec((rows, d), lambda i, a, b: (i, 0)),
                      pl.BlockSpec((rows, 1), lambda i, a, b: (i, 0)),
                      pl.BlockSpec((rows, 1), lambda i, a, b: (i, 0)),
                      pl.BlockSpec((8, d), lambda i, a, b: (0, gate_idx))],
            out_specs=pl.BlockSpec((rows, d), lambda i, a, b: (i, 0)),
            scratch_shapes=[pltpu.VMEM((2, rows, d), F32), pltpu.SemaphoreType.DMA(())]),
        out_shape=jax.ShapeDtypeStruct((n_tok, d), F32),
        compiler_params=_cparams(("arbitrary",), 8 * rows * d * 4),
        name="moe_combine",
    )(d0, d1, y_slots, x, w0, w1, mods)


def _moe_routing(logits, n_tok):
    top_logit, top_idx = lax.top_k(logits, TOP_K)
    top_w = jax.nn.softmax(top_logit, axis=-1)
    flat_e = top_idx.reshape(-1).astype(jnp.int32)
    n_assign = n_tok * TOP_K
    onehot = (flat_e[:, None] == jnp.arange(N_EXPERTS, dtype=jnp.int32)[None, :]).astype(jnp.int32)
    csum = jnp.cumsum(onehot, axis=0)
    rank = jnp.take_along_axis(csum, flat_e[:, None], axis=1)[:, 0] - 1
    counts = csum[-1]
    padded = (counts + MOE_BLK - 1) // MOE_BLK * MOE_BLK
    pad_ends = jnp.cumsum(padded)
    pad_starts = pad_ends - padded
    dest = (pad_starts[flat_e] + rank).astype(jnp.int32)
    n_blocks = -(-(n_assign + N_EXPERTS * (MOE_BLK - 1)) // MOE_BLK)
    n_slots = n_blocks * MOE_BLK
    slot_tok = jnp.zeros((n_slots,), jnp.int32).at[dest].set(jnp.arange(n_assign, dtype=jnp.int32) // TOP_K)
    block_e = jnp.minimum(jnp.searchsorted(pad_ends, jnp.arange(n_blocks, dtype=jnp.int32) * MOE_BLK, side="right"),
                          N_EXPERTS - 1).astype(jnp.int32)
    n_used = (pad_ends[-1] // MOE_BLK).astype(jnp.int32).reshape(1)
    dest2 = dest.reshape(n_tok, TOP_K)
    return dest2[:, 0], dest2[:, 1], top_w[:, 0:1], top_w[:, 1:2], slot_tok, block_e, n_used


def _rope_tables(n_lat, n_ctx, dim):
    rows = n_lat // GRID_W
    row = jnp.broadcast_to(jnp.arange(rows)[:, None], (rows, GRID_W)).reshape(-1).astype(F32)
    col = jnp.broadcast_to(jnp.arange(GRID_W)[None, :], (rows, GRID_W)).reshape(-1).astype(F32)
    quarter = dim // 4
    inv = ROPE_THETA ** (-jnp.arange(quarter, dtype=F32) / quarter)
    ang = jnp.concatenate([row[:, None] * inv, col[:, None] * inv], axis=-1)
    cos, sin = jnp.cos(ang), jnp.sin(ang)
    reps = LANES // dim
    c = jnp.tile(jnp.concatenate([cos, cos], axis=-1), (1, reps))
    s = jnp.tile(jnp.concatenate([-sin, sin], axis=-1), (1, reps))
    c = jnp.concatenate([c, jnp.ones((n_ctx, LANES), F32)], axis=0)
    s = jnp.concatenate([s, jnp.zeros((n_ctx, LANES), F32)], axis=0)
    return c, s


def kernel(x, c, ctx, c_ctx, w_mod, b_mod, attn_norm, ffn_norm, w_in, w_out, diff_q_norm, diff_k_norm, diff_lambda, diff_out_norm, gqa_q_norm, gqa_k_norm, mlstm_conv_w, mlstm_conv_b, mlstm_gate_b, mlstm_out_norm, ffn_w_gate, ffn_w_up, ffn_w_down, router_w, router_b, exp_w_gate, exp_w_up, exp_w_down):
    batch, n_lat, d = x.shape
    n_ctx = ctx.shape[1]
    depth = w_in.shape[0]
    assert batch == 1 and n_ctx == ROW_TILE and n_lat % MLSTM_CHUNK == 0 and n_lat % GRID_W == 0
    t_all = n_lat + n_ctx
    hd = MLSTM_HEADS * MLSTM_DIM

    cc = jnp.zeros((8, d), F32).at[0].set(c[0]).at[1].set(c_ctx)
    mods_all = _mods(cc, w_mod, b_mod)
    rope_diff = _rope_tables(n_lat, n_ctx, DIFF_QK_DIM)
    rope_gqa = _rope_tables(n_lat, n_ctx, GQA_DIM)
    pos = jnp.arange(MLSTM_CHUNK)
    vis_fwd = (pos[None, :] <= pos[:, None]).astype(F32)
    vis_bwd = (pos[None, :] >= pos[:, None]).astype(F32)

    xa = jnp.concatenate([x[0], ctx[0]], axis=0)
    for l in range(depth):
        need_ctx = l < depth - 1
        rows = t_all if need_ctx else n_lat
        lam_init = 0.8 - 0.6 * math.exp(-0.3 * l)
        mods = mods_all[l]

        h = _modnorm(xa, attn_norm[l], mods, 0, n_lat)
        w_in_l = w_in[l]
        proj = _matmul([h], [w_in_l[:, :_MAIN_WIDTH].astype(BF16)], rows=t_all, tm=_row_tile(t_all), tn=512,
                       out_dtype=BF16, name="in_proj")
        w_gate_cols = jnp.pad(w_in_l[:, _MAIN_WIDTH:], ((0, 0), (0, LANES - 4 * MLSTM_HEADS))).astype(BF16)
        gates = _matmul([h], [w_gate_cols], rows=t_all, tm=_row_tile(t_all), tn=LANES, name="gate_proj")

        dq = _qkprep(proj, _DQ, DIFF_HEADS, diff_q_norm[l], *rope_diff, seg=DIFF_QK_DIM,
                     scale=DIFF_QK_DIM ** -0.5 * LOG2E, split=True)
        dk = _qkprep(proj, _DK, DIFF_HEADS, diff_k_norm[l], *rope_diff, seg=DIFF_QK_DIM, scale=1.0, split=False)
        mix_d = _attention(dq, dk, proj[:, _DV:_GQ].T, n_kv=DIFF_HEADS, n_q_heads=2, n_lat=n_lat, n_ctx=n_ctx,
                           q_rows=rows, diff=(diff_lambda[l], diff_out_norm[l], lam_init))

        gq = _qkprep(proj, _GQ, GQA_HEADS, gqa_q_norm[l], *rope_gqa, seg=GQA_DIM, scale=GQA_DIM ** -0.5 * LOG2E,
                     split=False)
        gk = _qkprep(proj, _GK, GQA_KV_HEADS, gqa_k_norm[l], *rope_gqa, seg=GQA_DIM, scale=1.0, split=False)
        mix_g = _attention(gq, gk, proj[:, _GV:_MQ].T, n_kv=GQA_KV_HEADS, n_q_heads=GQA_HEADS // GQA_KV_HEADS,
                           n_lat=n_lat, n_ctx=n_ctx, q_rows=rows)

        qk = _mlstm_conv(proj, mlstm_conv_w[l], mlstm_conv_b[l], n_lat)
        kt = qk[:, hd:].T
        g_t = gates[:, :4 * MLSTM_HEADS].T.reshape(4, MLSTM_HEADS, t_all)
        g_t = jnp.pad(g_t, ((0, 0), (0, 16 - MLSTM_HEADS), (0, 0)))
        gb = jnp.pad(mlstm_gate_b[l].astype(F32), ((0, 0), (0, 16 - MLSTM_HEADS)))[:, :, None]
        mv, mo = proj[:, _MV:_MO], proj[:, _MO:_MAIN_WIDTH]
        h_f = _mlstm_direction(qk, kt, mv, g_t[0], g_t[1], gb[0], gb[1], vis_fwd, n_lat=n_lat, reverse=False)
        h_b = _mlstm_direction(qk, kt, mv, g_t[2], g_t[3], gb[2], gb[3], vis_bwd, n_lat=n_lat, reverse=True)
        mix_m = _mlstm_finish(h_f, h_b, mo, mlstm_out_norm[l], rows)

        w_out_l = w_out[l]
        w_parts = [w_out_l[:1024].astype(BF16), w_out_l[1024:2560].astype(BF16), w_out_l[2560:].astype(BF16)]
        x1 = _matmul([mix_d, mix_g, mix_m], w_parts, rows=rows, tm=_row_tile(rows), tn=512, kind="resid",
                     resid=(xa, mods, 2), n_lat=n_lat, name="out_proj")

        if l % 2 == 0:
            tok = _modnorm(x1, ffn_norm[l], mods, 3, n_lat)
            hid = _matmul([tok], [ffn_w_gate[l // 2].astype(BF16), ffn_w_up[l // 2].astype(BF16)], rows=rows,
                          tm=_row_tile(rows), tn=256, kind="swiglu", out_dtype=BF16, name="ffn_up")
            xa = _matmul([hid], [ffn_w_down[l // 2].astype(BF16)], rows=rows, tm=_row_tile(rows), tn=256,
                         kind="resid", resid=(x1, mods, 5), n_lat=n_lat, name="ffn_down")
        else:
            assert not need_ctx
            rw = jnp.pad(router_w[l // 2].astype(F32), ((0, 0), (0, LANES - N_EXPERTS)))
            rb = jnp.pad(router_b[l // 2].astype(F32), (0, LANES - N_EXPERTS)).reshape(1, LANES)
            tok, logits = _modnorm(x1, ffn_norm[l], mods, 3, n_lat, router=(rw, rb), out_dtype=F32)
            d0, d1, w0, w1, slot_tok, block_e, n_used = _moe_routing(logits[:, :N_EXPERTS], rows)
            hid = _moe_up(tok, exp_w_gate[l // 2].astype(BF16), exp_w_up[l // 2].astype(BF16), block_e, n_used,
                          slot_tok, tn=512)
            ys = _expert_matmul(hid, [exp_w_down[l // 2].astype(BF16)], block_e, n_used, tn=512, kind="plain",
                                out_dtype=F32, name="moe_down")
            xa = _moe_combine(ys, x1, d0, d1, w0, w1, mods, 5)
    return xa[:n_lat].reshape(batch, n_lat, d)
```

```python
import functools
import math

import jax
import jax.numpy as jnp
from jax import lax
from jax.experimental import pallas as pl
from jax.experimental.pallas import tpu as pltpu

F32 = jnp.float32
BF16 = jnp.bfloat16
HIGHEST = lax.Precision.HIGHEST

GRID_W = 64
ROPE_THETA = 10000.0
NORM_EPS = 1e-6
LOG2E = math.log2(math.e)

DIFF_HEADS = 8
DIFF_QK_DIM = 64
GQA_HEADS = 12
GQA_KV_HEADS = 4
GQA_DIM = 128
MLSTM_HEADS = 12
MLSTM_DIM = 128
N_EXPERTS = 8
TOP_K = 2

LANES = 128
ROW_TILE = 256
KEY_TILE = 512
ONES_ROWS = 16
KEY_UNROLL = 8
MLSTM_CHUNK = 256
MOE_BLK = 1024
MOE_SLAB = 256
GATHER_ROWS = 256
VMEM_CAP = 60 * 1024 * 1024

_DQ, _DK, _DV = 0, 1024, 2048
_GQ, _GK, _GV = 3072, 4608, 5120
_MQ, _MK, _MV, _MO, _MG = 5632, 7168, 8704, 10240, 11776
_MAIN_WIDTH = 11776


def _cparams(sem, vmem_bytes):
    limit = int(min(VMEM_CAP, max(32 * 1024 * 1024, vmem_bytes + (8 << 20))))
    return pltpu.CompilerParams(dimension_semantics=sem, vmem_limit_bytes=limit)


def _row_tile(m, cands=(1408, 1024, 768, 512, 256)):
    for t in cands:
        if m % t == 0:
            return t
    raise ValueError(f"no row tile for {m}")


def _log_sigmoid(x):
    return jnp.minimum(x, 0.0) - jnp.log1p(jnp.exp(-jnp.abs(x)))


def _silu(x):
    return x * jax.nn.sigmoid(x)


def _mod_kernel(c_ref, w_ref, b_ref, o_ref):
    s = _silu(c_ref[...])
    o_ref[0] = jnp.dot(s, w_ref[0], preferred_element_type=F32) + b_ref[0]


def _mods(cc, w_mod, b_mod):
    n_layers, d, n6 = w_mod.shape
    tn = 512
    return pl.pallas_call(
        _mod_kernel,
        grid=(n_layers, n6 // tn),
        in_specs=[pl.BlockSpec((8, d), lambda l, j: (0, 0)),
                  pl.BlockSpec((1, d, tn), lambda l, j: (l, 0, j)),
                  pl.BlockSpec((1, 1, tn), lambda l, j: (l, 0, j))],
        out_specs=pl.BlockSpec((1, 8, tn), lambda l, j: (l, 0, j)),
        out_shape=jax.ShapeDtypeStruct((n_layers, 8, n6), F32),
        compiler_params=_cparams(("parallel", "parallel"), 2 * d * tn * 4),
        name="adaln_mods",
    )(cc, w_mod, b_mod.reshape(n_layers, 1, n6))


def _modnorm_kernel(x_ref, g_ref, sh_ref, sc_ref, *rest, n_lat, tm, with_router):
    x = x_ref[...]
    ms = jnp.mean(x * x, axis=-1, keepdims=True)
    y = x * lax.rsqrt(ms + NORM_EPS) * g_ref[...]
    rows = pl.program_id(0) * tm + lax.broadcasted_iota(jnp.int32, (tm, 1), 0)
    is_ctx = rows >= n_lat
    sc = jnp.where(is_ctx, sc_ref[1:2, :], sc_ref[0:1, :])
    sh = jnp.where(is_ctx, sh_ref[1:2, :], sh_ref[0:1, :])
    h = y * (1.0 + sc) + sh
    if with_router:
        rw_ref, rb_ref, o_ref, lg_ref = rest
        lg_ref[...] = jnp.dot(h, rw_ref[...], preferred_element_type=F32, precision=HIGHEST) + rb_ref[...]
    else:
        (o_ref,) = rest
    o_ref[...] = h.astype(o_ref.dtype)


def _modnorm(x, gain, mods, shift_idx, n_lat, router=None, out_dtype=BF16):
    m, d = x.shape
    tm = ROW_TILE
    in_specs = [pl.BlockSpec((tm, d), lambda i: (i, 0)),
                pl.BlockSpec((1, d), lambda i: (0, 0)),
                pl.BlockSpec((8, d), lambda i: (0, shift_idx)),
                pl.BlockSpec((8, d), lambda i: (0, shift_idx + 1))]
    args = [x, gain.reshape(1, d), mods, mods]
    out_specs = pl.BlockSpec((tm, d), lambda i: (i, 0))
    out_shape = jax.ShapeDtypeStruct((m, d), out_dtype)
    if router is not None:
        rw, rb = router
        in_specs += [pl.BlockSpec((d, LANES), lambda i: (0, 0)),
                     pl.BlockSpec((1, LANES), lambda i: (0, 0))]
        args += [rw, rb]
        out_specs = [out_specs, pl.BlockSpec((tm, LANES), lambda i: (i, 0))]
        out_shape = [out_shape, jax.ShapeDtypeStruct((m, LANES), F32)]
    return pl.pallas_call(
        functools.partial(_modnorm_kernel, n_lat=n_lat, tm=tm, with_router=router is not None),
        grid=(m // tm,),
        in_specs=in_specs, out_specs=out_specs, out_shape=out_shape,
        compiler_params=_cparams(("parallel",), 2 * tm * d * 6 + 4 * tm * d * 4 + d * LANES * 8),
        name="modnorm",
    )(*args)


def _mxu_operand(w):
    return w if w.dtype == BF16 else w.astype(BF16)


def _mm_kernel(*refs, a_of_w, kind, n_lat, tm, valid_cols=None):
    n_a = 1 + max(ai for ai, _ in a_of_w)
    n_w = len(a_of_w)
    a_refs, w_refs, rest = refs[:n_a], refs[n_a:n_a + n_w], refs[n_a + n_w:]

    def lhs(i):
        ai, off = a_of_w[i]
        return a_refs[ai][:, off:off + w_refs[i].shape[0]]

    def rhs(i):
        w = w_refs[i][...]
        if valid_cols is not None:
            w = jnp.where(lax.broadcasted_iota(jnp.int32, w.shape, 1) < valid_cols, w, 0.0)
        return _mxu_operand(w)

    if kind == "swiglu":
        g = jnp.dot(lhs(0), rhs(0), preferred_element_type=F32)
        u = jnp.dot(lhs(1), rhs(1), preferred_element_type=F32)
        (o_ref,) = rest
        o_ref[...] = (_silu(g) * u).astype(o_ref.dtype)
        return
    acc = jnp.dot(lhs(0), rhs(0), preferred_element_type=F32)
    for i in range(1, n_w):
        acc = acc + jnp.dot(lhs(i), rhs(i), preferred_element_type=F32)
    if kind == "plain":
        (o_ref,) = rest
        o_ref[...] = acc.astype(o_ref.dtype)
    else:
        x_ref, g_ref, o_ref = rest
        rows = pl.program_id(0) * tm + lax.broadcasted_iota(jnp.int32, (tm, 1), 0)
        gate = jnp.where(rows >= n_lat, g_ref[1:2, :], g_ref[0:1, :])
        o_ref[...] = x_ref[...] + gate * acc


def _matmul(a_list, w_list, *, rows, n, tm, tn, col0=0, kind="plain", out_dtype=F32, resid=None, n_lat=0,
            valid_cols=None, name="matmul"):
    assert col0 % tn == 0
    cb = col0 // tn
    in_specs = [pl.BlockSpec((tm, a.shape[1]), lambda i, j: (i, 0)) for a in a_list]
    args = list(a_list)
    vmem = sum(2 * tm * a.shape[1] * 2 for a in a_list) + 6 * tm * tn * 4
    for w3d, lead, _, _, k_rows, k_blk in w_list:
        in_specs.append(pl.BlockSpec((None, k_rows, tn), lambda i, j, lead=lead, k_blk=k_blk: (lead, k_blk, cb + j)))
        args.append(w3d)
        vmem += k_rows * tn * (2 * w3d.dtype.itemsize + 2)
    if kind == "resid":
        x, mods, gate_idx = resid
        d = x.shape[1]
        in_specs += [pl.BlockSpec((tm, tn), lambda i, j: (i, j)),
                     pl.BlockSpec((8, tn), lambda i, j: (0, gate_idx * (d // tn) + j))]
        args += [x, mods]
    a_of_w = tuple((a_idx, a_off) for _, _, a_idx, a_off, _, _ in w_list)
    return pl.pallas_call(
        functools.partial(_mm_kernel, a_of_w=a_of_w, kind=kind, n_lat=n_lat, tm=tm, valid_cols=valid_cols),
        grid=(rows // tm, n // tn),
        in_specs=in_specs,
        out_specs=pl.BlockSpec((tm, tn), lambda i, j: (i, j)),
        out_shape=jax.ShapeDtypeStruct((rows, n), out_dtype),
        compiler_params=_cparams(("parallel", "arbitrary"), vmem),
        name=name,
    )(*args)


def _qkprep_kernel(x_ref, g_ref, c_ref, s_ref, o_ref, *, n_tiles, seg, scale, split):
    lane = lax.broadcasted_iota(jnp.int32, c_ref.shape, 1)
    gain, cos, sin = g_ref[...], c_ref[...], s_ref[...]
    half = seg // 2
    ow = 2 * LANES if split else LANES
    for j in range(n_tiles):
        x = x_ref[:, j * LANES:(j + 1) * LANES].astype(F32)
        x2 = x * x
        if seg == LANES:
            ms = jnp.sum(x2, axis=-1, keepdims=True) * (1.0 / seg)
        else:
            lo = lane < seg
            s_lo = jnp.sum(jnp.where(lo, x2, 0.0), axis=-1, keepdims=True)
            s_hi = jnp.sum(jnp.where(lo, 0.0, x2), axis=-1, keepdims=True)
            ms = jnp.where(lo, s_lo, s_hi) * (1.0 / seg)
        y = x * lax.rsqrt(ms + NORM_EPS) * gain
        if seg == LANES:
            partner = pltpu.roll(y, half, 1)
        else:
            partner = jnp.where((lane % seg) < half, pltpu.roll(y, LANES - half, 1), pltpu.roll(y, half, 1))
        y = (y * cos + partner * sin) * scale
        if split:
            o_ref[:, j * ow:j * ow + LANES] = jnp.where(lane < seg, y, 0.0).astype(o_ref.dtype)
            o_ref[:, j * ow + LANES:(j + 1) * ow] = jnp.where(lane < seg, 0.0, y).astype(o_ref.dtype)
        else:
            o_ref[:, j * ow:(j + 1) * ow] = y.astype(o_ref.dtype)


def _qkprep(proj, col0, n_tiles, gain, rope_c, rope_s, *, seg, scale, split):
    t = proj.shape[0]
    tm = ROW_TILE
    width = n_tiles * LANES
    assert col0 % width == 0
    cb = col0 // width
    g = jnp.tile(gain.astype(F32), LANES // seg).reshape(1, LANES)
    ow = (2 if split else 1) * width
    return pl.pallas_call(
        functools.partial(_qkprep_kernel, n_tiles=n_tiles, seg=seg, scale=scale, split=split),
        grid=(t // tm,),
        in_specs=[pl.BlockSpec((tm, width), lambda i: (i, cb)),
                  pl.BlockSpec((1, LANES), lambda i: (0, 0)),
                  pl.BlockSpec((tm, LANES), lambda i: (i, 0)),
                  pl.BlockSpec((tm, LANES), lambda i: (i, 0))],
        out_specs=pl.BlockSpec((tm, ow), lambda i: (i, 0)),
        out_shape=jax.ShapeDtypeStruct((t, ow), BF16),
        compiler_params=_cparams(("parallel",), 0),
        name="qk_prep",
    )(proj, g, rope_c, rope_s)


def _attn_kernel(q_ref, k_ref, vt_ref, *rest, n_q_heads, n_lat, n_ctx, tq, tk, lam_init):
    is_ctx = pl.program_id(1) * tq >= n_lat
    qs = [q_ref[:, g * LANES:(g + 1) * LANES] for g in range(n_q_heads)]

    def scores(kc):
        return tuple(lax.dot_general(kc, q, (((1,), (1,)), ((), ())), preferred_element_type=F32) for q in qs)

    def update(ss, vtc, carry):
        new = []
        for s, (m, acc) in zip(ss, carry):
            m_new = jnp.maximum(m, jnp.max(s, axis=0, keepdims=True))
            p = jnp.exp2(s - m_new).astype(BF16)
            new.append((m_new, jnp.exp2(m - m_new) * acc + jnp.dot(vtc, p, preferred_element_type=F32)))
        return tuple(new)

    carry = tuple((jnp.full((1, tq), -jnp.inf, F32), jnp.zeros((vt_ref.shape[0], tq), F32))
                  for _ in range(n_q_heads))
    carry = update(scores(k_ref[n_lat:n_lat + n_ctx, :]), vt_ref[:, n_lat:n_lat + n_ctx], carry)

    n_chunks = n_lat // tk
    unroll = math.gcd(n_chunks, KEY_UNROLL)

    def body(c, state):
        ss, carry = state
        for u in range(unroll):
            idx = c * unroll + u
            nxt = pl.multiple_of(jnp.minimum(idx + 1, n_chunks - 1) * tk, tk)
            ss_next = scores(k_ref[pl.ds(nxt, tk), :])
            carry = update(ss, vt_ref[:, pl.ds(pl.multiple_of(idx * tk, tk), tk)], carry)
            ss = ss_next
        return ss, carry

    _, carry = lax.fori_loop(0, jnp.where(is_ctx, 0, n_chunks // unroll), body, (scores(k_ref[0:tk, :]), carry))
    outs_t = [acc[:LANES] / acc[LANES:LANES + 1] for _, acc in carry]
    if lam_init is None:
        (o_ref,) = rest
        for g in range(n_q_heads):
            o_ref[:, g * LANES:(g + 1) * LANES] = outs_t[g].T.astype(o_ref.dtype)
    else:
        lam_ref, g_ref, o_ref = rest
        lp = lam_ref[...]
        lam = (jnp.exp(jnp.sum(lp[0:1, :] * lp[1:2, :], axis=-1, keepdims=True))
               - jnp.exp(jnp.sum(lp[2:3, :] * lp[3:4, :], axis=-1, keepdims=True)) + lam_init)
        o = (outs_t[0] - lam * outs_t[1]).T
        ms = jnp.mean(o * o, axis=-1, keepdims=True)
        o_ref[...] = (o * lax.rsqrt(ms + NORM_EPS) * g_ref[...] * (1.0 - lam_init)).astype(o_ref.dtype)


def _attention(q, k, v, *, n_kv, n_q_heads, n_lat, n_ctx, q_rows, diff=None):
    t = k.shape[0]
    tq = ROW_TILE
    tk = min(KEY_TILE, n_lat)
    assert n_lat % tk == 0
    qw = n_q_heads * LANES
    ow = LANES if diff is not None else qw
    vt = jnp.concatenate([v.T.reshape(n_kv, LANES, t), jnp.ones((n_kv, ONES_ROWS, t), v.dtype)], axis=1)
    vt = vt.reshape(n_kv * (LANES + ONES_ROWS), t)
    in_specs = [pl.BlockSpec((tq, qw), lambda h, i: (i, h)),
                pl.BlockSpec((t, LANES), lambda h, i: (0, h)),
                pl.BlockSpec((LANES + ONES_ROWS, t), lambda h, i: (h, 0))]
    args = [q, k, vt]
    lam_init = None
    if diff is not None:
        lam_p, gain, lam_init = diff
        in_specs += [pl.BlockSpec((4, DIFF_QK_DIM), lambda h, i: (0, 0)),
                     pl.BlockSpec((1, LANES), lambda h, i: (0, 0))]
        args += [lam_p.astype(F32), gain.astype(F32).reshape(1, LANES)]
    vmem = 2 * 2 * t * LANES * 2 + 4 * tq * qw * 4 + 6 * n_q_heads * tq * tk * 4
    return pl.pallas_call(
        functools.partial(_attn_kernel, n_q_heads=n_q_heads, n_lat=n_lat, n_ctx=n_ctx, tq=tq, tk=tk,
                          lam_init=lam_init),
        grid=(n_kv, q_rows // tq),
        in_specs=in_specs,
        out_specs=pl.BlockSpec((tq, ow), lambda h, i: (i, h)),
        out_shape=jax.ShapeDtypeStruct((q_rows, n_kv * ow), BF16),
        compiler_params=_cparams(("parallel", "arbitrary"), vmem),
        name="attention",
    )(*args)


def _conv_kernel(x_ref, xp_ref, xn_ref, w_ref, b_ref, o_ref, *, n_lat, t_all, tm, k_col_tile):
    i, j = pl.program_id(0), pl.program_id(1)
    x = x_ref[...].astype(F32)
    row = lax.broadcasted_iota(jnp.int32, (tm, 1), 0)
    grow = i * tm + row
    prev = jnp.where(row == 0, xp_ref[15:16, :].astype(F32), pltpu.roll(x, 1, 0))
    nxt = jnp.where(row == tm - 1, xn_ref[0:1, :].astype(F32), pltpu.roll(x, tm - 1, 0))
    prev = jnp.where((grow == 0) | (grow == n_lat), 0.0, prev)
    nxt = jnp.where((grow == n_lat - 1) | (grow == t_all - 1), 0.0, nxt)
    y = b_ref[...] + prev * w_ref[0:1, :] + x * w_ref[1:2, :] + nxt * w_ref[2:3, :]
    y = _silu(y) * jnp.where(j >= k_col_tile, MLSTM_DIM ** -0.5, 1.0)
    o_ref[...] = y.astype(o_ref.dtype)


def _mlstm_conv(proj, conv_w, conv_b, n_lat):
    t = proj.shape[0]
    tm, tc, hb = ROW_TILE, 512, 16
    width = 2 * MLSTM_HEADS * MLSTM_DIM
    cb = _MQ // tc
    nrb = t // hb
    return pl.pallas_call(
        functools.partial(_conv_kernel, n_lat=n_lat, t_all=t, tm=tm, k_col_tile=(width // 2) // tc),
        grid=(t // tm, width // tc),
        in_specs=[pl.BlockSpec((tm, tc), lambda i, j: (i, cb + j)),
                  pl.BlockSpec((hb, tc), lambda i, j: (jnp.maximum(i * (tm // hb) - 1, 0), cb + j)),
                  pl.BlockSpec((hb, tc), lambda i, j: (jnp.minimum((i + 1) * (tm // hb), nrb - 1), cb + j)),
                  pl.BlockSpec((3, tc), lambda i, j: (0, j)),
                  pl.BlockSpec((1, tc), lambda i, j: (0, j))],
        out_specs=pl.BlockSpec((tm, tc), lambda i, j: (i, j)),
        out_shape=jax.ShapeDtypeStruct((t, width), BF16),
        compiler_params=_cparams(("parallel", "arbitrary"), 0),
        name="mlstm_conv",
    )(proj, proj, proj, conv_w.astype(F32), conv_b.astype(F32).reshape(1, width))


def _mlstm_kernel(q_ref, k_ref, kt_ref, v_ref, gi_ref, gf_ref, bi_ref, bf_ref, mask_ref, h_ref,
                  c_sc, n_sc, m_sc, *, n_heads):
    @pl.when(pl.program_id(0) == 0)
    def _():
        c_sc[...] = jnp.zeros_like(c_sc)
        n_sc[...] = jnp.zeros_like(n_sc)
        m_sc[...] = jnp.zeros_like(m_sc)

    vis = mask_ref[...]
    li_all = gi_ref[...] + bi_ref[...]
    lf_all = _log_sigmoid(gf_ref[...] + bf_ref[...])
    br_all = lax.dot_general(lf_all, vis, (((1,), (1,)), ((), ())), preferred_element_type=F32, precision=HIGHEST)
    bend_all = jnp.sum(lf_all, axis=1, keepdims=True)
    d = MLSTM_DIM
    for h in range(n_heads):
        lir, lfr, br = li_all[h:h + 1, :], lf_all[h:h + 1, :], br_all[h:h + 1, :]
        b_end = bend_all[h:h + 1, :]
        m_prev = m_sc[h:h + 1, 0:1]
        bc = jnp.sum(vis * lfr, axis=1, keepdims=True)
        lw = b_end - br + lir
        m_new = jnp.maximum(b_end + m_prev, jnp.max(lw, axis=1, keepdims=True))
        w_end = jnp.exp(lw - m_new)
        decay = jnp.exp(b_end + m_prev - m_new)
        log_d = jnp.where(vis > 0.0, bc - br + lir, -jnp.inf)
        m_t = jnp.maximum(bc + m_prev, jnp.max(log_d, axis=1, keepdims=True))
        dmat = jnp.exp(log_d - m_t)
        inter = jnp.exp(bc + m_prev - m_t)
        qh = q_ref[:, h * d:(h + 1) * d]
        kh = k_ref[:, h * d:(h + 1) * d]
        vh = v_ref[:, h * d:(h + 1) * d]
        kth = kt_ref[h * d:(h + 1) * d, :]
        c_prev = c_sc[h]
        n_prev = n_sc[h:h + 1, :]
        sc = jnp.dot(qh, kth, preferred_element_type=F32) * dmat
        num = (jnp.dot(sc.astype(BF16), vh, preferred_element_type=F32)
               + inter * jnp.dot(qh, c_prev.astype(BF16), preferred_element_type=F32))
        den = (jnp.sum(sc, axis=1, keepdims=True)
               + inter * jnp.sum(qh.astype(F32) * n_prev, axis=1, keepdims=True))
        h_ref[:, h * d:(h + 1) * d] = num / jnp.maximum(jnp.abs(den), jnp.exp(-m_t))
        ktw = (kth.astype(F32) * w_end).astype(BF16)
        c_sc[h] = decay * c_prev + jnp.dot(ktw, vh, preferred_element_type=F32)
        w8 = jnp.broadcast_to(w_end, (8, w_end.shape[1])).astype(BF16)
        n_sc[h:h + 1, :] = decay * n_prev + jnp.dot(w8, kh, preferred_element_type=F32)[0:1, :]
        m_sc[h:h + 1, :] = jnp.broadcast_to(m_new, (1, LANES))


def _mlstm_direction(qk, kt, v, gi, gf, bi, bf, vis, *, n_lat, reverse):
    t = qk.shape[0]
    lc = MLSTM_CHUNK
    n_chunks = t // lc
    ctx_blk = n_lat // lc
    hd = MLSTM_HEADS * MLSTM_DIM

    def blk(c):
        lat = (n_chunks - 1 - c) if reverse else (c - 1)
        return jnp.where(c == 0, ctx_blk, lat)

    return pl.pallas_call(
        functools.partial(_mlstm_kernel, n_heads=MLSTM_HEADS),
        grid=(n_chunks,),
        in_specs=[pl.BlockSpec((lc, hd), lambda c: (blk(c), 0)),
                  pl.BlockSpec((lc, hd), lambda c: (blk(c), 1)),
                  pl.BlockSpec((hd, lc), lambda c: (0, blk(c))),
                  pl.BlockSpec((lc, hd), lambda c: (blk(c), 0)),
                  pl.BlockSpec((16, lc), lambda c: (0, blk(c))),
                  pl.BlockSpec((16, lc), lambda c: (0, blk(c))),
                  pl.BlockSpec((16, 1), lambda c: (0, 0)),
                  pl.BlockSpec((16, 1), lambda c: (0, 0)),
                  pl.BlockSpec((lc, lc), lambda c: (0, 0))],
        out_specs=pl.BlockSpec((lc, hd), lambda c: (blk(c), 0)),
        out_shape=jax.ShapeDtypeStruct((t, hd), F32),
        scratch_shapes=[pltpu.VMEM((MLSTM_HEADS, MLSTM_DIM, MLSTM_DIM), F32),
                        pltpu.VMEM((16, LANES), F32),
                        pltpu.VMEM((16, LANES), F32)],
        compiler_params=_cparams(("arbitrary",), 16 << 20),
        name="mlstm_bwd" if reverse else "mlstm_fwd",
    )(qk, qk, kt, v, gi, gf, bi, bf, vis)


def _mlstm_finish_kernel(hf_ref, hb_ref, o_ref, g_ref, out_ref, *, n_heads):
    d = MLSTM_DIM
    for h in range(n_heads):
        sl = slice(h * d, (h + 1) * d)
        s = hf_ref[:, sl] + hb_ref[:, sl]
        ms = jnp.mean(s * s, axis=-1, keepdims=True)
        y = s * lax.rsqrt(ms + NORM_EPS) * g_ref[:, sl]
        out_ref[:, sl] = (jax.nn.sigmoid(o_ref[:, sl].astype(F32)) * y).astype(out_ref.dtype)


def _mlstm_finish(hf, hb, o_gate, out_gain, rows):
    hd = MLSTM_HEADS * MLSTM_DIM
    tm = ROW_TILE
    return pl.pallas_call(
        functools.partial(_mlstm_finish_kernel, n_heads=MLSTM_HEADS),
        grid=(rows // tm,),
        in_specs=[pl.BlockSpec((tm, hd), lambda i: (i, 0)),
                  pl.BlockSpec((tm, hd), lambda i: (i, 0)),
                  pl.BlockSpec((tm, hd), lambda i: (i, 0)),
                  pl.BlockSpec((1, hd), lambda i: (0, 0))],
        out_specs=pl.BlockSpec((tm, hd), lambda i: (i, 0)),
        out_shape=jax.ShapeDtypeStruct((rows, hd), BF16),
        compiler_params=_cparams(("parallel",), 0),
        name="mlstm_finish",
    )(hf, hb, o_gate, out_gain.astype(F32).reshape(1, hd))


def _moe_up_kernel(be_ref, nv_ref, st_ref, tok_ref, wg_ref, wu_ref, o_ref, xbuf, xb, sem, *, nj):
    b, j = pl.program_id(0), pl.program_id(1)
    nb = pl.num_programs(0)
    n_valid = nv_ref[b]
    per_step = MOE_BLK // nj

    def copy(blk, r):
        src = tok_ref.at[pl.ds(st_ref[blk * MOE_BLK + r], 1), :]
        return pltpu.make_async_copy(src, xbuf.at[pl.ds(r, 1), :], sem)

    @pl.when((b == 0) & (j == 0) & (n_valid > 0))
    def _():
        def body(r, carry):
            copy(0, r).start()
            return carry
        lax.fori_loop(0, MOE_BLK, body, 0)

    @pl.when((j == 0) & (n_valid > 0))
    def _():
        def body(r, carry):
            copy(b, r).wait()
            return carry
        lax.fori_loop(0, MOE_BLK, body, 0)
        xb[...] = xbuf[...].astype(BF16)

    @pl.when((b + 1 < nb) & (nv_ref[jnp.minimum(b + 1, nb - 1)] > 0))
    def _():
        for r in range(per_step):
            copy(b + 1, j * per_step + r).start()

    @pl.when(n_valid > 0)
    def _():
        wg, wu = _mxu_operand(wg_ref[...]), _mxu_operand(wu_ref[...])
        for s in range(MOE_BLK // MOE_SLAB):
            sl = slice(s * MOE_SLAB, (s + 1) * MOE_SLAB)

            @pl.when(n_valid > s * MOE_SLAB)
            def _():
                a = xb[sl, :]
                g = jnp.dot(a, wg, preferred_element_type=F32)
                u = jnp.dot(a, wu, preferred_element_type=F32)
                o_ref[sl, :] = (_silu(g) * u).astype(o_ref.dtype)

            @pl.when(n_valid <= s * MOE_SLAB)
            def _():
                o_ref[sl, :] = jnp.zeros((MOE_SLAB, o_ref.shape[1]), o_ref.dtype)

    @pl.when(n_valid == 0)
    def _():
        o_ref[...] = jnp.zeros_like(o_ref)


def _moe_up(tok, w_gate, w_up, lead, block_e, n_valid, slot_tok, *, tn):
    n_slots = slot_tok.shape[0]
    k = tok.shape[1]
    n = w_gate.shape[-1]
    nb, nj = n_slots // MOE_BLK, n // tn
    assert MOE_BLK % nj == 0

    def w_map(b, j, be, nv, st):
        return (lead, be[b], 0, jnp.where(nv[b] > 0, j, nj - 1))

    return pl.pallas_call(
        functools.partial(_moe_up_kernel, nj=nj),
        grid_spec=pltpu.PrefetchScalarGridSpec(
            num_scalar_prefetch=3,
            grid=(nb, nj),
            in_specs=[pl.BlockSpec(memory_space=pl.ANY),
                      pl.BlockSpec((None, None, k, tn), w_map),
                      pl.BlockSpec((None, None, k, tn), w_map)],
            out_specs=pl.BlockSpec((MOE_BLK, tn), lambda b, j, be, nv, st: (b, j)),
            scratch_shapes=[pltpu.VMEM((MOE_BLK, k), F32),
                            pltpu.VMEM((MOE_BLK, k), BF16),
                            pltpu.SemaphoreType.DMA(())]),
        out_shape=jax.ShapeDtypeStruct((n_slots, n), BF16),
        compiler_params=_cparams(("arbitrary", "arbitrary"),
                                 MOE_BLK * k * 6 + 2 * k * tn * (2 * w_gate.dtype.itemsize + 2)
                                 + 4 * MOE_BLK * tn * 4),
        name="moe_up",
    )(block_e, n_valid, slot_tok, tok, w_gate, w_up)


def _moe_down_kernel(be_ref, nv_ref, nu_ref, a_ref, w_ref, o_ref):
    n_valid = nv_ref[pl.program_id(0)]

    @pl.when(n_valid > 0)
    def _():
        w = _mxu_operand(w_ref[...])
        for s in range(MOE_BLK // MOE_SLAB):
            sl = slice(s * MOE_SLAB, (s + 1) * MOE_SLAB)

            @pl.when(n_valid > s * MOE_SLAB)
            def _():
                o_ref[sl, :] = jnp.dot(a_ref[sl, :], w, preferred_element_type=F32)

            @pl.when(n_valid <= s * MOE_SLAB)
            def _():
                o_ref[sl, :] = jnp.zeros((MOE_SLAB, o_ref.shape[1]), o_ref.dtype)

    @pl.when(n_valid == 0)
    def _():
        o_ref[...] = jnp.zeros_like(o_ref)


def _moe_down(a, w, lead, block_e, n_valid, n_used, *, tn):
    n_slots, k = a.shape
    n = w.shape[-1]
    nb, nj = n_slots // MOE_BLK, n // tn

    def a_map(b, j, be, nv, nu):
        return (jnp.minimum(b, nu[0] - 1), 0)

    def w_map(b, j, be, nv, nu):
        return (lead, be[b], 0, jnp.where(nv[b] > 0, j, nj - 1))

    return pl.pallas_call(
        _moe_down_kernel,
        grid_spec=pltpu.PrefetchScalarGridSpec(
            num_scalar_prefetch=3,
            grid=(nb, nj),
            in_specs=[pl.BlockSpec((MOE_BLK, k), a_map), pl.BlockSpec((None, None, k, tn), w_map)],
            out_specs=pl.BlockSpec((MOE_BLK, tn), lambda b, j, be, nv, nu: (b, j))),
        out_shape=jax.ShapeDtypeStruct((n_slots, n), F32),
        compiler_params=_cparams(("arbitrary", "arbitrary"),
                                 2 * MOE_BLK * k * 2 + k * tn * (2 * w.dtype.itemsize + 2) + 4 * MOE_BLK * tn * 4),
        name="moe_down",
    )(block_e, n_valid, n_used, a, w)


def _combine_kernel(d0_ref, d1_ref, y_ref, x_ref, w0_ref, w1_ref, g_ref, o_ref, buf, sem, *, rows):
    base = pl.program_id(0) * rows

    def copy(r, which):
        idx = d0_ref[base + r] if which == 0 else d1_ref[base + r]
        return pltpu.make_async_copy(y_ref.at[pl.ds(idx, 1), :], buf.at[which, pl.ds(r, 1), :], sem)

    def start(r, carry):
        copy(r, 0).start()
        copy(r, 1).start()
        return carry

    def wait(r, carry):
        copy(r, 0).wait()
        copy(r, 1).wait()
        return carry

    lax.fori_loop(0, rows, start, 0)
    lax.fori_loop(0, rows, wait, 0)
    y = w0_ref[...] * buf[0] + w1_ref[...] * buf[1]
    o_ref[...] = x_ref[...] + g_ref[0:1, :] * y


def _moe_combine(y_slots, x, d0, d1, w0, w1, mods, gate_idx):
    n_tok, d = x.shape
    rows = GATHER_ROWS // 2
    return pl.pallas_call(
        functools.partial(_combine_kernel, rows=rows),
        grid_spec=pltpu.PrefetchScalarGridSpec(
            num_scalar_prefetch=2,
            grid=(n_tok // rows,),
            in_specs=[pl.BlockSpec(memory_space=pl.ANY),
                      pl.BlockSpec((rows, d), lambda i, a, b: (i, 0)),
                      pl.BlockSpec((rows, 1), lambda i, a, b: (i, 0)),
                      pl.BlockSpec((rows, 1), lambda i, a, b: (i, 0)),
                      pl.BlockSpec((8, d), lambda i, a, b: (0, gate_idx))],
            out_specs=pl.BlockSpec((rows, d), lambda i, a, b: (i, 0)),
            scratch_shapes=[pltpu.VMEM((2, rows, d), F32), pltpu.SemaphoreType.DMA(())]),
        out_shape=jax.ShapeDtypeStruct((n_tok, d), F32),
        compiler_params=_cparams(("arbitrary",), 8 * rows * d * 4),
        name="moe_combine",
    )(d0, d1, y_slots, x, w0, w1, mods)


def _moe_routing(logits, n_tok):
    top_logit, top_idx = lax.top_k(logits, TOP_K)
    top_w = jax.nn.softmax(top_logit, axis=-1)
    flat_e = top_idx.reshape(-1).astype(jnp.int32)
    n_assign = n_tok * TOP_K
    onehot = (flat_e[:, None] == jnp.arange(N_EXPERTS, dtype=jnp.int32)[None, :]).astype(jnp.int32)
    csum = jnp.cumsum(onehot, axis=0)
    rank = jnp.take_along_axis(csum, flat_e[:, None], axis=1)[:, 0] - 1
    counts = csum[-1]
    padded = (counts + MOE_BLK - 1) // MOE_BLK * MOE_BLK
    pad_ends = jnp.cumsum(padded)
    pad_starts = pad_ends - padded
    dest = (pad_starts[flat_e] + rank).astype(jnp.int32)
    n_blocks = -(-(n_assign + N_EXPERTS * (MOE_BLK - 1)) // MOE_BLK)
    n_slots = n_blocks * MOE_BLK
    slot_tok = jnp.zeros((n_slots,), jnp.int32).at[dest].set(jnp.arange(n_assign, dtype=jnp.int32) // TOP_K)
    blk_start = jnp.arange(n_blocks, dtype=jnp.int32) * MOE_BLK
    block_e = jnp.minimum(jnp.searchsorted(pad_ends, blk_start, side="right"), N_EXPERTS - 1).astype(jnp.int32)
    slab_ends = pad_starts + (counts + MOE_SLAB - 1) // MOE_SLAB * MOE_SLAB
    n_valid = jnp.clip(slab_ends[block_e] - blk_start, 0, MOE_BLK).astype(jnp.int32)
    n_used = (pad_ends[-1] // MOE_BLK).astype(jnp.int32).reshape(1)
    dest2 = dest.reshape(n_tok, TOP_K)
    return dest2[:, 0], dest2[:, 1], top_w[:, 0:1], top_w[:, 1:2], slot_tok, block_e, n_valid, n_used


def _rope_tables(n_lat, n_ctx, dim):
    rows = n_lat // GRID_W
    row = jnp.broadcast_to(jnp.arange(rows)[:, None], (rows, GRID_W)).reshape(-1).astype(F32)
    col = jnp.broadcast_to(jnp.arange(GRID_W)[None, :], (rows, GRID_W)).reshape(-1).astype(F32)
    quarter = dim // 4
    inv = ROPE_THETA ** (-jnp.arange(quarter, dtype=F32) / quarter)
    ang = jnp.concatenate([row[:, None] * inv, col[:, None] * inv], axis=-1)
    cos, sin = jnp.cos(ang), jnp.sin(ang)
    reps = LANES // dim
    c = jnp.tile(jnp.concatenate([cos, cos], axis=-1), (1, reps))
    s = jnp.tile(jnp.concatenate([-sin, sin], axis=-1), (1, reps))
    c = jnp.concatenate([c, jnp.ones((n_ctx, LANES), F32)], axis=0)
    s = jnp.concatenate([s, jnp.zeros((n_ctx, LANES), F32)], axis=0)
    return c, s


def kernel(x, c, ctx, c_ctx, w_mod, b_mod, attn_norm, ffn_norm, w_in, w_out, diff_q_norm, diff_k_norm, diff_lambda, diff_out_norm, gqa_q_norm, gqa_k_norm, mlstm_conv_w, mlstm_conv_b, mlstm_gate_b, mlstm_out_norm, ffn_w_gate, ffn_w_up, ffn_w_down, router_w, router_b, exp_w_gate, exp_w_up, exp_w_down):
    batch, n_lat, d = x.shape
    n_ctx = ctx.shape[1]
    depth = w_in.shape[0]
    assert batch == 1 and n_ctx == ROW_TILE and n_lat % MLSTM_CHUNK == 0 and n_lat % GRID_W == 0
    t_all = n_lat + n_ctx
    hd = MLSTM_HEADS * MLSTM_DIM

    cc = jnp.zeros((8, d), F32).at[0].set(c[0]).at[1].set(c_ctx)
    mods_all = _mods(cc, w_mod, b_mod)
    rope_diff = _rope_tables(n_lat, n_ctx, DIFF_QK_DIM)
    rope_gqa = _rope_tables(n_lat, n_ctx, GQA_DIM)
    pos = jnp.arange(MLSTM_CHUNK)
    vis_fwd = (pos[None, :] <= pos[:, None]).astype(F32)
    vis_bwd = (pos[None, :] >= pos[:, None]).astype(F32)

    xa = jnp.concatenate([x[0], ctx[0]], axis=0)
    for l in range(depth):
        need_ctx = l < depth - 1
        rows = t_all if need_ctx else n_lat
        lam_init = 0.8 - 0.6 * math.exp(-0.3 * l)
        mods = mods_all[l]

        h = _modnorm(xa, attn_norm[l], mods, 0, n_lat)
        proj = _matmul([h], [(w_in, l, 0, 0, d, 0)], rows=t_all, n=_MAIN_WIDTH, tm=_row_tile(t_all), tn=256,
                       out_dtype=BF16, name="in_proj")
        gates = _matmul([h], [(w_in, l, 0, 0, d, 0)], rows=t_all, n=LANES, col0=_MG, tm=_row_tile(t_all), tn=LANES,
                        valid_cols=w_in.shape[-1] - _MG, name="gate_proj")

        dq = _qkprep(proj, _DQ, DIFF_HEADS, diff_q_norm[l], *rope_diff, seg=DIFF_QK_DIM,
                     scale=DIFF_QK_DIM ** -0.5 * LOG2E, split=True)
        dk = _qkprep(proj, _DK, DIFF_HEADS, diff_k_norm[l], *rope_diff, seg=DIFF_QK_DIM, scale=1.0, split=False)
        mix_d = _attention(dq, dk, proj[:, _DV:_GQ], n_kv=DIFF_HEADS, n_q_heads=2, n_lat=n_lat, n_ctx=n_ctx,
                           q_rows=rows, diff=(diff_lambda[l], diff_out_norm[l], lam_init))

        gq = _qkprep(proj, _GQ, GQA_HEADS, gqa_q_norm[l], *rope_gqa, seg=GQA_DIM, scale=GQA_DIM ** -0.5 * LOG2E,
                     split=False)
        gk = _qkprep(proj, _GK, GQA_KV_HEADS, gqa_k_norm[l], *rope_gqa, seg=GQA_DIM, scale=1.0, split=False)
        mix_g = _attention(gq, gk, proj[:, _GV:_MQ], n_kv=GQA_KV_HEADS, n_q_heads=GQA_HEADS // GQA_KV_HEADS,
                           n_lat=n_lat, n_ctx=n_ctx, q_rows=rows)

        qk = _mlstm_conv(proj, mlstm_conv_w[l], mlstm_conv_b[l], n_lat)
        kt = qk[:, hd:].T
        g_t = gates[:, :4 * MLSTM_HEADS].T.reshape(4, MLSTM_HEADS, t_all)
        g_t = jnp.pad(g_t, ((0, 0), (0, 16 - MLSTM_HEADS), (0, 0)))
        gb = jnp.pad(mlstm_gate_b[l].astype(F32), ((0, 0), (0, 16 - MLSTM_HEADS)))[:, :, None]
        mv, mo = proj[:, _MV:_MO], proj[:, _MO:_MAIN_WIDTH]
        h_f = _mlstm_direction(qk, kt, mv, g_t[0], g_t[1], gb[0], gb[1], vis_fwd, n_lat=n_lat, reverse=False)
        h_b = _mlstm_direction(qk, kt, mv, g_t[2], g_t[3], gb[2], gb[3], vis_bwd, n_lat=n_lat, reverse=True)
        mix_m = _mlstm_finish(h_f, h_b, mo, mlstm_out_norm[l], rows)

        kb = 512
        w_blocks = [(w_out, l, ai, off, kb, (row0 + off) // kb)
                    for ai, (row0, width) in enumerate(((0, 1024), (1024, 1536), (2560, 1536)))
                    for off in range(0, width, kb)]
        x1 = _matmul([mix_d, mix_g, mix_m], w_blocks, rows=rows, n=d, tm=_row_tile(rows), tn=256, kind="resid",
                     resid=(xa, mods, 2), n_lat=n_lat, name="out_proj")

        if l % 2 == 0:
            tok = _modnorm(x1, ffn_norm[l], mods, 3, n_lat)
            d_ff = ffn_w_gate.shape[-1]
            hid = _matmul([tok], [(ffn_w_gate, l // 2, 0, 0, d, 0), (ffn_w_up, l // 2, 0, 0, d, 0)], rows=rows,
                          n=d_ff, tm=_row_tile(rows), tn=256, kind="swiglu", out_dtype=BF16, name="ffn_up")
            w_down = ffn_w_down[l // 2].astype(BF16)[None]
            xa = _matmul([hid], [(w_down, 0, 0, 0, d_ff, 0)], rows=rows, n=d, tm=_row_tile(rows, (768, 512, 256)),
                         tn=256, kind="resid", resid=(x1, mods, 5), n_lat=n_lat, name="ffn_down")
        else:
            assert not need_ctx
            rw = jnp.pad(router_w[l // 2].astype(F32), ((0, 0), (0, LANES - N_EXPERTS)))
            rb = jnp.pad(router_b[l // 2].astype(F32), (0, LANES - N_EXPERTS)).reshape(1, LANES)
            tok, logits = _modnorm(x1, ffn_norm[l], mods, 3, n_lat, router=(rw, rb), out_dtype=F32)
            d0, d1, w0, w1, slot_tok, block_e, n_valid, n_used = _moe_routing(logits[:, :N_EXPERTS], rows)
            hid = _moe_up(tok, exp_w_gate, exp_w_up, l // 2, block_e, n_valid, slot_tok, tn=256)
            ys = _moe_down(hid, exp_w_down, l // 2, block_e, n_valid, n_used, tn=256)
            xa = _moe_combine(ys, x1, d0, d1, w0, w1, mods, 5)
    return xa[:n_lat].reshape(batch, n_lat, d)
```

```python
import functools
import math

import jax
import jax.numpy as jnp
from jax import lax
from jax.experimental import pallas as pl
from jax.experimental.pallas import tpu as pltpu

F32 = jnp.float32
BF16 = jnp.bfloat16
HIGHEST = lax.Precision.HIGHEST

GRID_W = 64
ROPE_THETA = 10000.0
NORM_EPS = 1e-6
LOG2E = math.log2(math.e)

DIFF_HEADS = 8
DIFF_QK_DIM = 64
GQA_HEADS = 12
GQA_KV_HEADS = 4
GQA_DIM = 128
MLSTM_HEADS = 12
MLSTM_DIM = 128
N_EXPERTS = 8
TOP_K = 2

LANES = 128
ROW_TILE = 256
ONES_ROWS = 16
MLSTM_CHUNK = 256
MOE_BLK = 1024
MOE_SLAB = 256
GATHER_ROWS = 256
VMEM_CAP = 60 * 1024 * 1024

_DQ, _DK, _DV = 0, 1024, 2048
_GQ, _GK, _GV = 3072, 4608, 5120
_MQ, _MK, _MV, _MO, _MG = 5632, 7168, 8704, 10240, 11776
_MAIN_WIDTH = 11776


def _cparams(sem, vmem_bytes):
    limit = int(min(VMEM_CAP, max(32 * 1024 * 1024, vmem_bytes + (8 << 20))))
    return pltpu.CompilerParams(dimension_semantics=sem, vmem_limit_bytes=limit)


def _row_tile(m, cands=(1408, 1024, 768, 512, 256)):
    for t in cands:
        if m % t == 0:
            return t
    raise ValueError(f"no row tile for {m}")


def _log_sigmoid(x):
    return jnp.minimum(x, 0.0) - jnp.log1p(jnp.exp(-jnp.abs(x)))


def _silu(x):
    return x * jax.nn.sigmoid(x)


def _mod_kernel(c_ref, w_ref, b_ref, o_ref):
    s = _silu(c_ref[...])
    o_ref[0] = jnp.dot(s, w_ref[0], preferred_element_type=F32) + b_ref[0]


def _mods(cc, w_mod, b_mod):
    n_layers, d, n6 = w_mod.shape
    tn = 512
    return pl.pallas_call(
        _mod_kernel,
        grid=(n_layers, n6 // tn),
        in_specs=[pl.BlockSpec((8, d), lambda l, j: (0, 0)),
                  pl.BlockSpec((1, d, tn), lambda l, j: (l, 0, j)),
                  pl.BlockSpec((1, 1, tn), lambda l, j: (l, 0, j))],
        out_specs=pl.BlockSpec((1, 8, tn), lambda l, j: (l, 0, j)),
        out_shape=jax.ShapeDtypeStruct((n_layers, 8, n6), F32),
        compiler_params=_cparams(("parallel", "parallel"), 2 * d * tn * 4),
        name="adaln_mods",
    )(cc, w_mod, b_mod.reshape(n_layers, 1, n6))


def _modnorm_kernel(x_ref, g_ref, sh_ref, sc_ref, *rest, n_lat, tm, with_router):
    x = x_ref[...]
    ms = jnp.mean(x * x, axis=-1, keepdims=True)
    y = x * lax.rsqrt(ms + NORM_EPS) * g_ref[...]
    rows = pl.program_id(0) * tm + lax.broadcasted_iota(jnp.int32, (tm, 1), 0)
    is_ctx = rows >= n_lat
    sc = jnp.where(is_ctx, sc_ref[1:2, :], sc_ref[0:1, :])
    sh = jnp.where(is_ctx, sh_ref[1:2, :], sh_ref[0:1, :])
    h = y * (1.0 + sc) + sh
    if with_router:
        rw_ref, rb_ref, o_ref, lg_ref = rest
        lg_ref[...] = jnp.dot(h, rw_ref[...], preferred_element_type=F32, precision=HIGHEST) + rb_ref[...]
    else:
        (o_ref,) = rest
    o_ref[...] = h.astype(o_ref.dtype)


def _modnorm(x, gain, mods, shift_idx, n_lat, router=None, out_dtype=BF16):
    m, d = x.shape
    tm = ROW_TILE
    in_specs = [pl.BlockSpec((tm, d), lambda i: (i, 0)),
                pl.BlockSpec((1, d), lambda i: (0, 0)),
                pl.BlockSpec((8, d), lambda i: (0, shift_idx)),
                pl.BlockSpec((8, d), lambda i: (0, shift_idx + 1))]
    args = [x, gain.reshape(1, d), mods, mods]
    out_specs = pl.BlockSpec((tm, d), lambda i: (i, 0))
    out_shape = jax.ShapeDtypeStruct((m, d), out_dtype)
    if router is not None:
        rw, rb = router
        in_specs += [pl.BlockSpec((d, LANES), lambda i: (0, 0)),
                     pl.BlockSpec((1, LANES), lambda i: (0, 0))]
        args += [rw, rb]
        out_specs = [out_specs, pl.BlockSpec((tm, LANES), lambda i: (i, 0))]
        out_shape = [out_shape, jax.ShapeDtypeStruct((m, LANES), F32)]
    return pl.pallas_call(
        functools.partial(_modnorm_kernel, n_lat=n_lat, tm=tm, with_router=router is not None),
        grid=(m // tm,),
        in_specs=in_specs, out_specs=out_specs, out_shape=out_shape,
        compiler_params=_cparams(("parallel",), 2 * tm * d * 6 + 4 * tm * d * 4 + d * LANES * 8),
        name="modnorm",
    )(*args)


def _mxu_operand(w):
    return w if w.dtype == BF16 else w.astype(BF16)


def _mm_kernel(*refs, a_of_w, kind, n_lat, tm, nt=False, valid_cols=None):
    n_a = 1 + max(ai for ai, _ in a_of_w)
    n_w = len(a_of_w)
    a_refs, w_refs, rest = refs[:n_a], refs[n_a:n_a + n_w], refs[n_a + n_w:]
    col_axis, k_axis = (0, 1) if nt else (1, 0)

    def lhs(i):
        ai, off = a_of_w[i]
        return a_refs[ai][:, off:off + w_refs[i].shape[k_axis]]

    def rhs(i):
        w = w_refs[i][...]
        if valid_cols is not None:
            w = jnp.where(lax.broadcasted_iota(jnp.int32, w.shape, col_axis) < valid_cols, w, 0.0)
        return _mxu_operand(w)

    def mm(i):
        return lax.dot_general(lhs(i), rhs(i), (((1,), (k_axis,)), ((), ())), preferred_element_type=F32)

    if kind == "swiglu":
        g, u = mm(0), mm(1)
        (o_ref,) = rest
        o_ref[...] = (_silu(g) * u).astype(o_ref.dtype)
        return
    acc = mm(0)
    for i in range(1, n_w):
        acc = acc + mm(i)
    if kind == "plain":
        (o_ref,) = rest
        o_ref[...] = acc.astype(o_ref.dtype)
    else:
        x_ref, g_ref, o_ref = rest
        rows = pl.program_id(0) * tm + lax.broadcasted_iota(jnp.int32, (tm, 1), 0)
        gate = jnp.where(rows >= n_lat, g_ref[1:2, :], g_ref[0:1, :])
        o_ref[...] = x_ref[...] + gate * acc


def _matmul(a_list, w_list, *, rows, n, tm, tn, col0=0, kind="plain", out_dtype=F32, resid=None, n_lat=0,
            nt=False, valid_cols=None, name="matmul"):
    assert col0 % tn == 0
    cb = col0 // tn
    in_specs = [pl.BlockSpec((tm, a.shape[1]), lambda i, j: (i, 0)) for a in a_list]
    args = list(a_list)
    vmem = sum(2 * tm * a.shape[1] * 2 for a in a_list) + 6 * tm * tn * 4
    for w3d, lead, _, _, k_rows, k_blk in w_list:
        if nt:
            in_specs.append(pl.BlockSpec((None, tn, k_rows), lambda i, j, lead=lead, k_blk=k_blk: (lead, cb + j, k_blk)))
        else:
            in_specs.append(pl.BlockSpec((None, k_rows, tn), lambda i, j, lead=lead, k_blk=k_blk: (lead, k_blk, cb + j)))
        args.append(w3d)
        vmem += k_rows * tn * (2 * w3d.dtype.itemsize + 2)
    if kind == "resid":
        x, mods, gate_idx = resid
        d = x.shape[1]
        in_specs += [pl.BlockSpec((tm, tn), lambda i, j: (i, j)),
                     pl.BlockSpec((8, tn), lambda i, j: (0, gate_idx * (d // tn) + j))]
        args += [x, mods]
    a_of_w = tuple((a_idx, a_off) for _, _, a_idx, a_off, _, _ in w_list)
    return pl.pallas_call(
        functools.partial(_mm_kernel, a_of_w=a_of_w, kind=kind, n_lat=n_lat, tm=tm, nt=nt, valid_cols=valid_cols),
        grid=(rows // tm, n // tn),
        in_specs=in_specs,
        out_specs=pl.BlockSpec((tm, tn), lambda i, j: (i, j)),
        out_shape=jax.ShapeDtypeStruct((rows, n), out_dtype),
        compiler_params=_cparams(("parallel", "arbitrary"), vmem),
        name=name,
    )(*args)


def _qkprep_kernel(x_ref, g_ref, c_ref, s_ref, o_ref, *, n_tiles, seg, scale, split):
    lane = lax.broadcasted_iota(jnp.int32, c_ref.shape, 1)
    gain, cos, sin = g_ref[...], c_ref[...], s_ref[...]
    half = seg // 2
    ow = 2 * LANES if split else LANES
    for j in range(n_tiles):
        x = x_ref[:, j * LANES:(j + 1) * LANES].astype(F32)
        x2 = x * x
        if seg == LANES:
            ms = jnp.sum(x2, axis=-1, keepdims=True) * (1.0 / seg)
        else:
            lo = lane < seg
            s_lo = jnp.sum(jnp.where(lo, x2, 0.0), axis=-1, keepdims=True)
            s_hi = jnp.sum(jnp.where(lo, 0.0, x2), axis=-1, keepdims=True)
            ms = jnp.where(lo, s_lo, s_hi) * (1.0 / seg)
        y = x * lax.rsqrt(ms + NORM_EPS) * gain
        if seg == LANES:
            partner = pltpu.roll(y, half, 1)
        else:
            partner = jnp.where((lane % seg) < half, pltpu.roll(y, LANES - half, 1), pltpu.roll(y, half, 1))
        y = (y * cos + partner * sin) * scale
        if split:
            o_ref[:, j * ow:j * ow + LANES] = jnp.where(lane < seg, y, 0.0).astype(o_ref.dtype)
            o_ref[:, j * ow + LANES:(j + 1) * ow] = jnp.where(lane < seg, 0.0, y).astype(o_ref.dtype)
        else:
            o_ref[:, j * ow:(j + 1) * ow] = y.astype(o_ref.dtype)


def _qkprep(proj, col0, n_tiles, gain, rope_c, rope_s, *, seg, scale, split, ctx_first=False):
    t = proj.shape[0]
    tm = ROW_TILE
    n_row_tiles = t // tm
    shift = 1 if ctx_first else 0
    width = n_tiles * LANES
    assert col0 % width == 0
    cb = col0 // width
    g = jnp.tile(gain.astype(F32), LANES // seg).reshape(1, LANES)
    ow = (2 if split else 1) * width
    return pl.pallas_call(
        functools.partial(_qkprep_kernel, n_tiles=n_tiles, seg=seg, scale=scale, split=split),
        grid=(t // tm,),
        in_specs=[pl.BlockSpec((tm, width), lambda i: (i, cb)),
                  pl.BlockSpec((1, LANES), lambda i: (0, 0)),
                  pl.BlockSpec((tm, LANES), lambda i: (i, 0)),
                  pl.BlockSpec((tm, LANES), lambda i: (i, 0))],
        out_specs=pl.BlockSpec((tm, ow), lambda i: ((i + shift) % n_row_tiles, 0)),
        out_shape=jax.ShapeDtypeStruct((t, ow), BF16),
        compiler_params=_cparams(("parallel",), 0),
        name="qk_prep",
    )(proj, g, rope_c, rope_s)


def _attn_kernel(q_ref, k_ref, vt_ref, *rest, n_q_heads, n_ctx, tq, tk, ctx_tile, lam_init):
    qs = [q_ref[:, g * LANES:(g + 1) * LANES] for g in range(n_q_heads)]

    def scores(kc):
        return tuple(lax.dot_general(kc, q, (((1,), (1,)), ((), ())), preferred_element_type=F32) for q in qs)

    def update(ss, vtc, carry):
        new = []
        for s, (m, acc) in zip(ss, carry):
            m_new = jnp.maximum(m, jnp.max(s, axis=0, keepdims=True))
            p = jnp.exp2(s - m_new).astype(BF16)
            new.append((m_new, jnp.exp2(m - m_new) * acc + jnp.dot(vtc, p, preferred_element_type=F32)))
        return tuple(new)

    def attend(n_keys, chunk):
        carry = tuple((jnp.full((1, tq), -jnp.inf, F32), jnp.zeros((vt_ref.shape[0], tq), F32))
                      for _ in range(n_q_heads))
        ss = scores(k_ref[0:chunk, :])
        for c in range(n_keys // chunk):
            ss_next = scores(k_ref[(c + 1) * chunk:(c + 2) * chunk, :]) if (c + 1) * chunk < n_keys else None
            carry = update(ss, vt_ref[:, c * chunk:(c + 1) * chunk], carry)
            ss = ss_next
        outs_t = [acc[:LANES] / acc[LANES:LANES + 1] for _, acc in carry]
        if lam_init is None:
            (o_ref,) = rest
            for g in range(n_q_heads):
                o_ref[:, g * LANES:(g + 1) * LANES] = outs_t[g].T.astype(o_ref.dtype)
        else:
            lam_ref, g_ref, o_ref = rest
            lp = lam_ref[...]
            lam = (jnp.exp(jnp.sum(lp[0:1, :] * lp[1:2, :], axis=-1, keepdims=True))
                   - jnp.exp(jnp.sum(lp[2:3, :] * lp[3:4, :], axis=-1, keepdims=True)) + lam_init)
            o = (outs_t[0] - lam * outs_t[1]).T
            ms = jnp.mean(o * o, axis=-1, keepdims=True)
            o_ref[...] = (o * lax.rsqrt(ms + NORM_EPS) * g_ref[...] * (1.0 - lam_init)).astype(o_ref.dtype)

    if ctx_tile is None:
        attend(k_ref.shape[0], tk)
    else:
        is_ctx = pl.program_id(1) == ctx_tile

        @pl.when(is_ctx)
        def _():
            attend(n_ctx, n_ctx)

        @pl.when(jnp.logical_not(is_ctx))
        def _():
            attend(k_ref.shape[0], tk)


def _attention(q, k, v, *, n_kv, n_q_heads, n_lat, n_ctx, q_rows, diff=None):
    t = k.shape[0]
    tq = ROW_TILE
    tk = next(c for c in (768, 512, 256) if t % c == 0)
    qw = n_q_heads * LANES
    ow = LANES if diff is not None else qw
    vt = jnp.concatenate([v.T.reshape(n_kv, LANES, t), jnp.ones((n_kv, ONES_ROWS, t), v.dtype)], axis=1)
    vt = vt.reshape(n_kv * (LANES + ONES_ROWS), t)
    in_specs = [pl.BlockSpec((tq, qw), lambda h, i: (i, h)),
                pl.BlockSpec((t, LANES), lambda h, i: (0, h)),
                pl.BlockSpec((LANES + ONES_ROWS, t), lambda h, i: (h, 0))]
    args = [q, k, vt]
    lam_init = None
    if diff is not None:
        lam_p, gain, lam_init = diff
        in_specs += [pl.BlockSpec((4, DIFF_QK_DIM), lambda h, i: (0, 0)),
                     pl.BlockSpec((1, LANES), lambda h, i: (0, 0))]
        args += [lam_p.astype(F32), gain.astype(F32).reshape(1, LANES)]
    vmem = 2 * 2 * t * LANES * 2 + 4 * tq * qw * 4 + 8 * n_q_heads * tq * tk * 4
    return pl.pallas_call(
        functools.partial(_attn_kernel, n_q_heads=n_q_heads, n_ctx=n_ctx, tq=tq, tk=tk,
                          ctx_tile=n_lat // tq if q_rows > n_lat else None, lam_init=lam_init),
        grid=(n_kv, q_rows // tq),
        in_specs=in_specs,
        out_specs=pl.BlockSpec((tq, ow), lambda h, i: (i, h)),
        out_shape=jax.ShapeDtypeStruct((q_rows, n_kv * ow), BF16),
        compiler_params=_cparams(("parallel", "arbitrary"), vmem),
        name="attention",
    )(*args)


def _conv_kernel(x_ref, xp_ref, xn_ref, w_ref, b_ref, o_ref, *, n_lat, t_all, tm, k_col_tile):
    i, j = pl.program_id(0), pl.program_id(1)
    x = x_ref[...].astype(F32)
    row = lax.broadcasted_iota(jnp.int32, (tm, 1), 0)
    grow = i * tm + row
    prev = jnp.where(row == 0, xp_ref[15:16, :].astype(F32), pltpu.roll(x, 1, 0))
    nxt = jnp.where(row == tm - 1, xn_ref[0:1, :].astype(F32), pltpu.roll(x, tm - 1, 0))
    prev = jnp.where((grow == 0) | (grow == n_lat), 0.0, prev)
    nxt = jnp.where((grow == n_lat - 1) | (grow == t_all - 1), 0.0, nxt)
    y = b_ref[...] + prev * w_ref[0:1, :] + x * w_ref[1:2, :] + nxt * w_ref[2:3, :]
    y = _silu(y) * jnp.where(j >= k_col_tile, MLSTM_DIM ** -0.5, 1.0)
    o_ref[...] = y.astype(o_ref.dtype)


def _mlstm_conv(proj, conv_w, conv_b, n_lat):
    t = proj.shape[0]
    tm, tc, hb = ROW_TILE, 512, 16
    width = 2 * MLSTM_HEADS * MLSTM_DIM
    cb = _MQ // tc
    nrb = t // hb
    return pl.pallas_call(
        functools.partial(_conv_kernel, n_lat=n_lat, t_all=t, tm=tm, k_col_tile=(width // 2) // tc),
        grid=(t // tm, width // tc),
        in_specs=[pl.BlockSpec((tm, tc), lambda i, j: (i, cb + j)),
                  pl.BlockSpec((hb, tc), lambda i, j: (jnp.maximum(i * (tm // hb) - 1, 0), cb + j)),
                  pl.BlockSpec((hb, tc), lambda i, j: (jnp.minimum((i + 1) * (tm // hb), nrb - 1), cb + j)),
                  pl.BlockSpec((3, tc), lambda i, j: (0, j)),
                  pl.BlockSpec((1, tc), lambda i, j: (0, j))],
        out_specs=pl.BlockSpec((tm, tc), lambda i, j: (i, j)),
        out_shape=jax.ShapeDtypeStruct((t, width), BF16),
        compiler_params=_cparams(("parallel", "arbitrary"), 0),
        name="mlstm_conv",
    )(proj, proj, proj, conv_w.astype(F32), conv_b.astype(F32).reshape(1, width))


def _mlstm_kernel(q_ref, k_ref, kt_ref, v_ref, gi_ref, gf_ref, bi_ref, bf_ref, mask_ref, h_ref,
                  c_sc, n_sc, m_sc, *, n_heads):
    @pl.when(pl.program_id(0) == 0)
    def _():
        c_sc[...] = jnp.zeros_like(c_sc)
        n_sc[...] = jnp.zeros_like(n_sc)
        m_sc[...] = jnp.zeros_like(m_sc)

    vis = mask_ref[...]
    li_all = gi_ref[...] + bi_ref[...]
    lf_all = _log_sigmoid(gf_ref[...] + bf_ref[...])
    br_all = lax.dot_general(lf_all, vis, (((1,), (1,)), ((), ())), preferred_element_type=F32, precision=HIGHEST)
    bend_all = jnp.sum(lf_all, axis=1, keepdims=True)
    d = MLSTM_DIM
    for h in range(n_heads):
        lir, lfr, br = li_all[h:h + 1, :], lf_all[h:h + 1, :], br_all[h:h + 1, :]
        b_end = bend_all[h:h + 1, :]
        m_prev = m_sc[h:h + 1, 0:1]
        bc = jnp.sum(vis * lfr, axis=1, keepdims=True)
        lw = b_end - br + lir
        m_new = jnp.maximum(b_end + m_prev, jnp.max(lw, axis=1, keepdims=True))
        w_end = jnp.exp(lw - m_new)
        decay = jnp.exp(b_end + m_prev - m_new)
        log_d = jnp.where(vis > 0.0, bc - br + lir, -jnp.inf)
        m_t = jnp.maximum(bc + m_prev, jnp.max(log_d, axis=1, keepdims=True))
        dmat = jnp.exp(log_d - m_t)
        inter = jnp.exp(bc + m_prev - m_t)
        qh = q_ref[:, h * d:(h + 1) * d]
        kh = k_ref[:, h * d:(h + 1) * d]
        vh = v_ref[:, h * d:(h + 1) * d]
        kth = kt_ref[h * d:(h + 1) * d, :]
        c_prev = c_sc[h]
        n_prev = n_sc[h:h + 1, :]
        sc = jnp.dot(qh, kth, preferred_element_type=F32) * dmat
        num = (jnp.dot(sc.astype(BF16), vh, preferred_element_type=F32)
               + inter * jnp.dot(qh, c_prev.astype(BF16), preferred_element_type=F32))
        den = (jnp.sum(sc, axis=1, keepdims=True)
               + inter * jnp.sum(qh.astype(F32) * n_prev, axis=1, keepdims=True))
        h_ref[:, h * d:(h + 1) * d] = num / jnp.maximum(jnp.abs(den), jnp.exp(-m_t))
        ktw = (kth.astype(F32) * w_end).astype(BF16)
        c_sc[h] = decay * c_prev + jnp.dot(ktw, vh, preferred_element_type=F32)
        w8 = jnp.broadcast_to(w_end, (8, w_end.shape[1])).astype(BF16)
        n_sc[h:h + 1, :] = decay * n_prev + jnp.dot(w8, kh, preferred_element_type=F32)[0:1, :]
        m_sc[h:h + 1, :] = jnp.broadcast_to(m_new, (1, LANES))


def _mlstm_direction(qk, kt, v, gi, gf, bi, bf, vis, *, n_lat, reverse):
    t = qk.shape[0]
    lc = MLSTM_CHUNK
    n_chunks = t // lc
    ctx_blk = n_lat // lc
    hd = MLSTM_HEADS * MLSTM_DIM

    def blk(c):
        lat = (n_chunks - 1 - c) if reverse else (c - 1)
        return jnp.where(c == 0, ctx_blk, lat)

    return pl.pallas_call(
        functools.partial(_mlstm_kernel, n_heads=MLSTM_HEADS),
        grid=(n_chunks,),
        in_specs=[pl.BlockSpec((lc, hd), lambda c: (blk(c), 0)),
                  pl.BlockSpec((lc, hd), lambda c: (blk(c), 1)),
                  pl.BlockSpec((hd, lc), lambda c: (0, blk(c))),
                  pl.BlockSpec((lc, hd), lambda c: (blk(c), 0)),
                  pl.BlockSpec((16, lc), lambda c: (0, blk(c))),
                  pl.BlockSpec((16, lc), lambda c: (0, blk(c))),
                  pl.BlockSpec((16, 1), lambda c: (0, 0)),
                  pl.BlockSpec((16, 1), lambda c: (0, 0)),
                  pl.BlockSpec((lc, lc), lambda c: (0, 0))],
        out_specs=pl.BlockSpec((lc, hd), lambda c: (blk(c), 0)),
        out_shape=jax.ShapeDtypeStruct((t, hd), F32),
        scratch_shapes=[pltpu.VMEM((MLSTM_HEADS, MLSTM_DIM, MLSTM_DIM), F32),
                        pltpu.VMEM((16, LANES), F32),
                        pltpu.VMEM((16, LANES), F32)],
        compiler_params=_cparams(("arbitrary",), 16 << 20),
        name="mlstm_bwd" if reverse else "mlstm_fwd",
    )(qk, qk, kt, v, gi, gf, bi, bf, vis)


def _mlstm_finish_kernel(hf_ref, hb_ref, o_ref, g_ref, out_ref, *, n_heads):
    d = MLSTM_DIM
    for h in range(n_heads):
        sl = slice(h * d, (h + 1) * d)
        s = hf_ref[:, sl] + hb_ref[:, sl]
        ms = jnp.mean(s * s, axis=-1, keepdims=True)
        y = s * lax.rsqrt(ms + NORM_EPS) * g_ref[:, sl]
        out_ref[:, sl] = (jax.nn.sigmoid(o_ref[:, sl].astype(F32)) * y).astype(out_ref.dtype)


def _mlstm_finish(hf, hb, o_gate, out_gain, rows):
    hd = MLSTM_HEADS * MLSTM_DIM
    tm = ROW_TILE
    return pl.pallas_call(
        functools.partial(_mlstm_finish_kernel, n_heads=MLSTM_HEADS),
        grid=(rows // tm,),
        in_specs=[pl.BlockSpec((tm, hd), lambda i: (i, 0)),
                  pl.BlockSpec((tm, hd), lambda i: (i, 0)),
                  pl.BlockSpec((tm, hd), lambda i: (i, 0)),
                  pl.BlockSpec((1, hd), lambda i: (0, 0))],
        out_specs=pl.BlockSpec((tm, hd), lambda i: (i, 0)),
        out_shape=jax.ShapeDtypeStruct((rows, hd), BF16),
        compiler_params=_cparams(("parallel",), 0),
        name="mlstm_finish",
    )(hf, hb, o_gate, out_gain.astype(F32).reshape(1, hd))


def _moe_up_kernel(be_ref, nv_ref, st_ref, tok_ref, wg_ref, wu_ref, o_ref, xbuf, xb, sem, *, nj):
    b, j = pl.program_id(0), pl.program_id(1)
    nb = pl.num_programs(0)
    n_valid = nv_ref[b]
    per_step = MOE_BLK // nj

    def copy(blk, r):
        src = tok_ref.at[pl.ds(st_ref[blk * MOE_BLK + r], 1), :]
        return pltpu.make_async_copy(src, xbuf.at[pl.ds(r, 1), :], sem)

    @pl.when((b == 0) & (j == 0) & (n_valid > 0))
    def _():
        def body(r, carry):
            copy(0, r).start()
            return carry
        lax.fori_loop(0, MOE_BLK, body, 0)

    @pl.when((j == 0) & (n_valid > 0))
    def _():
        def body(r, carry):
            copy(b, r).wait()
            return carry
        lax.fori_loop(0, MOE_BLK, body, 0)
        xb[...] = xbuf[...].astype(BF16)

    @pl.when((b + 1 < nb) & (nv_ref[jnp.minimum(b + 1, nb - 1)] > 0))
    def _():
        for r in range(per_step):
            copy(b + 1, j * per_step + r).start()

    def swiglu(sl):
        a = xb[sl, :]
        g = jnp.dot(a, _mxu_operand(wg_ref[...]), preferred_element_type=F32)
        u = jnp.dot(a, _mxu_operand(wu_ref[...]), preferred_element_type=F32)
        o_ref[sl, :] = (_silu(g) * u).astype(o_ref.dtype)

    @pl.when(n_valid == MOE_BLK)
    def _():
        swiglu(slice(None))

    @pl.when((n_valid > 0) & (n_valid < MOE_BLK))
    def _():
        for s in range(MOE_BLK // MOE_SLAB):
            sl = slice(s * MOE_SLAB, (s + 1) * MOE_SLAB)

            @pl.when(n_valid > s * MOE_SLAB)
            def _():
                swiglu(sl)

            @pl.when(n_valid <= s * MOE_SLAB)
            def _():
                o_ref[sl, :] = jnp.zeros((MOE_SLAB, o_ref.shape[1]), o_ref.dtype)

    @pl.when(n_valid == 0)
    def _():
        o_ref[...] = jnp.zeros_like(o_ref)


def _moe_up(tok, w_gate, w_up, lead, block_e, n_valid, slot_tok, *, tn):
    n_slots = slot_tok.shape[0]
    k = tok.shape[1]
    n = w_gate.shape[-1]
    nb, nj = n_slots // MOE_BLK, n // tn
    assert MOE_BLK % nj == 0

    def w_map(b, j, be, nv, st):
        return (lead, be[b], 0, jnp.where(nv[b] > 0, j, nj - 1))

    return pl.pallas_call(
        functools.partial(_moe_up_kernel, nj=nj),
        grid_spec=pltpu.PrefetchScalarGridSpec(
            num_scalar_prefetch=3,
            grid=(nb, nj),
            in_specs=[pl.BlockSpec(memory_space=pl.ANY),
                      pl.BlockSpec((None, None, k, tn), w_map),
                      pl.BlockSpec((None, None, k, tn), w_map)],
            out_specs=pl.BlockSpec((MOE_BLK, tn), lambda b, j, be, nv, st: (b, j)),
            scratch_shapes=[pltpu.VMEM((MOE_BLK, k), F32),
                            pltpu.VMEM((MOE_BLK, k), BF16),
                            pltpu.SemaphoreType.DMA(())]),
        out_shape=jax.ShapeDtypeStruct((n_slots, n), BF16),
        compiler_params=_cparams(("arbitrary", "arbitrary"),
                                 MOE_BLK * k * 6 + 2 * k * tn * (2 * w_gate.dtype.itemsize + 2)
                                 + 4 * MOE_BLK * tn * 4),
        name="moe_up",
    )(block_e, n_valid, slot_tok, tok, w_gate, w_up)


def _moe_down_kernel(be_ref, nv_ref, nu_ref, a_ref, w_ref, o_ref):
    n_valid = nv_ref[pl.program_id(0)]

    def down(sl):
        o_ref[sl, :] = jnp.dot(a_ref[sl, :], _mxu_operand(w_ref[...]), preferred_element_type=F32)

    @pl.when(n_valid == MOE_BLK)
    def _():
        down(slice(None))

    @pl.when((n_valid > 0) & (n_valid < MOE_BLK))
    def _():
        for s in range(MOE_BLK // MOE_SLAB):
            sl = slice(s * MOE_SLAB, (s + 1) * MOE_SLAB)

            @pl.when(n_valid > s * MOE_SLAB)
            def _():
                down(sl)

            @pl.when(n_valid <= s * MOE_SLAB)
            def _():
                o_ref[sl, :] = jnp.zeros((MOE_SLAB, o_ref.shape[1]), o_ref.dtype)

    @pl.when(n_valid == 0)
    def _():
        o_ref[...] = jnp.zeros_like(o_ref)


def _moe_down(a, w, lead, block_e, n_valid, n_used, *, tn):
    n_slots, k = a.shape
    n = w.shape[-1]
    nb, nj = n_slots // MOE_BLK, n // tn

    def a_map(b, j, be, nv, nu):
        return (jnp.minimum(b, nu[0] - 1), 0)

    def w_map(b, j, be, nv, nu):
        return (lead, be[b], 0, jnp.where(nv[b] > 0, j, nj - 1))

    return pl.pallas_call(
        _moe_down_kernel,
        grid_spec=pltpu.PrefetchScalarGridSpec(
            num_scalar_prefetch=3,
            grid=(nb, nj),
            in_specs=[pl.BlockSpec((MOE_BLK, k), a_map), pl.BlockSpec((None, None, k, tn), w_map)],
            out_specs=pl.BlockSpec((MOE_BLK, tn), lambda b, j, be, nv, nu: (b, j))),
        out_shape=jax.ShapeDtypeStruct((n_slots, n), F32),
        compiler_params=_cparams(("arbitrary", "arbitrary"),
                                 2 * MOE_BLK * k * 2 + k * tn * (2 * w.dtype.itemsize + 2) + 4 * MOE_BLK * tn * 4),
        name="moe_down",
    )(block_e, n_valid, n_used, a, w)


def _combine_kernel(d0_ref, d1_ref, y_ref, x_ref, w0_ref, w1_ref, g_ref, o_ref, buf, sem, *, rows):
    base = pl.program_id(0) * rows

    def copy(r, which):
        idx = d0_ref[base + r] if which == 0 else d1_ref[base + r]
        return pltpu.make_async_copy(y_ref.at[pl.ds(idx, 1), :], buf.at[which, pl.ds(r, 1), :], sem)

    def start(r, carry):
        copy(r, 0).start()
        copy(r, 1).start()
        return carry

    def wait(r, carry):
        copy(r, 0).wait()
        copy(r, 1).wait()
        return carry

    lax.fori_loop(0, rows, start, 0)
    lax.fori_loop(0, rows, wait, 0)
    y = w0_ref[...] * buf[0] + w1_ref[...] * buf[1]
    o_ref[...] = x_ref[...] + g_ref[0:1, :] * y


def _moe_combine(y_slots, x, d0, d1, w0, w1, mods, gate_idx):
    n_tok, d = x.shape
    rows = GATHER_ROWS // 2
    return pl.pallas_call(
        functools.partial(_combine_kernel, rows=rows),
        grid_spec=pltpu.PrefetchScalarGridSpec(
            num_scalar_prefetch=2,
            grid=(n_tok // rows,),
            in_specs=[pl.BlockSpec(memory_space=pl.ANY),
                      pl.BlockSpec((rows, d), lambda i, a, b: (i, 0)),
                      pl.BlockSpec((rows, 1), lambda i, a, b: (i, 0)),
                      pl.BlockSpec((rows, 1), lambda i, a, b: (i, 0)),
                      pl.BlockSpec((8, d), lambda i, a, b: (0, gate_idx))],
            out_specs=pl.BlockSpec((rows, d), lambda i, a, b: (i, 0)),
            scratch_shapes=[pltpu.VMEM((2, rows, d), F32), pltpu.SemaphoreType.DMA(())]),
        out_shape=jax.ShapeDtypeStruct((n_tok, d), F32),
        compiler_params=_cparams(("arbitrary",), 8 * rows * d * 4),
        name="moe_combine",
    )(d0, d1, y_slots, x, w0, w1, mods)


def _moe_routing(logits, n_tok):
    top_logit, top_idx = lax.top_k(logits, TOP_K)
    top_w = jax.nn.softmax(top_logit, axis=-1)
    flat_e = top_idx.reshape(-1).astype(jnp.int32)
    n_assign = n_tok * TOP_K
    onehot = (flat_e[:, None] == jnp.arange(N_EXPERTS, dtype=jnp.int32)[None, :]).astype(jnp.int32)
    csum = jnp.cumsum(onehot, axis=0)
    rank = jnp.take_along_axis(csum, flat_e[:, None], axis=1)[:, 0] - 1
    counts = csum[-1]
    padded = (counts + MOE_BLK - 1) // MOE_BLK * MOE_BLK
    pad_ends = jnp.cumsum(padded)
    pad_starts = pad_ends - padded
    dest = (pad_starts[flat_e] + rank).astype(jnp.int32)
    n_blocks = -(-(n_assign + N_EXPERTS * (MOE_BLK - 1)) // MOE_BLK)
    n_slots = n_blocks * MOE_BLK
    slot_tok = jnp.zeros((n_slots,), jnp.int32).at[dest].set(jnp.arange(n_assign, dtype=jnp.int32) // TOP_K)
    blk_start = jnp.arange(n_blocks, dtype=jnp.int32) * MOE_BLK
    block_e = jnp.minimum(jnp.searchsorted(pad_ends, blk_start, side="right"), N_EXPERTS - 1).astype(jnp.int32)
    slab_ends = pad_starts + (counts + MOE_SLAB - 1) // MOE_SLAB * MOE_SLAB
    n_valid = jnp.clip(slab_ends[block_e] - blk_start, 0, MOE_BLK).astype(jnp.int32)
    n_used = (pad_ends[-1] // MOE_BLK).astype(jnp.int32).reshape(1)
    dest2 = dest.reshape(n_tok, TOP_K)
    return dest2[:, 0], dest2[:, 1], top_w[:, 0:1], top_w[:, 1:2], slot_tok, block_e, n_valid, n_used


def _rope_tables(n_lat, n_ctx, dim):
    rows = n_lat // GRID_W
    row = jnp.broadcast_to(jnp.arange(rows)[:, None], (rows, GRID_W)).reshape(-1).astype(F32)
    col = jnp.broadcast_to(jnp.arange(GRID_W)[None, :], (rows, GRID_W)).reshape(-1).astype(F32)
    quarter = dim // 4
    inv = ROPE_THETA ** (-jnp.arange(quarter, dtype=F32) / quarter)
    ang = jnp.concatenate([row[:, None] * inv, col[:, None] * inv], axis=-1)
    cos, sin = jnp.cos(ang), jnp.sin(ang)
    reps = LANES // dim
    c = jnp.tile(jnp.concatenate([cos, cos], axis=-1), (1, reps))
    s = jnp.tile(jnp.concatenate([-sin, sin], axis=-1), (1, reps))
    c = jnp.concatenate([c, jnp.ones((n_ctx, LANES), F32)], axis=0)
    s = jnp.concatenate([s, jnp.zeros((n_ctx, LANES), F32)], axis=0)
    return c, s


def kernel(x, c, ctx, c_ctx, w_mod, b_mod, attn_norm, ffn_norm, w_in, w_out, diff_q_norm, diff_k_norm, diff_lambda, diff_out_norm, gqa_q_norm, gqa_k_norm, mlstm_conv_w, mlstm_conv_b, mlstm_gate_b, mlstm_out_norm, ffn_w_gate, ffn_w_up, ffn_w_down, router_w, router_b, exp_w_gate, exp_w_up, exp_w_down):
    batch, n_lat, d = x.shape
    n_ctx = ctx.shape[1]
    depth = w_in.shape[0]
    assert batch == 1 and n_ctx == ROW_TILE and n_lat % MLSTM_CHUNK == 0 and n_lat % GRID_W == 0
    t_all = n_lat + n_ctx
    hd = MLSTM_HEADS * MLSTM_DIM

    cc = jnp.zeros((8, d), F32).at[0].set(c[0]).at[1].set(c_ctx)
    mods_all = _mods(cc, w_mod, b_mod)
    rope_diff = _rope_tables(n_lat, n_ctx, DIFF_QK_DIM)
    rope_gqa = _rope_tables(n_lat, n_ctx, GQA_DIM)
    pos = jnp.arange(MLSTM_CHUNK)
    vis_fwd = (pos[None, :] <= pos[:, None]).astype(F32)
    vis_bwd = (pos[None, :] >= pos[:, None]).astype(F32)

    w_in_t = jnp.swapaxes(w_in, 1, 2)
    xa = jnp.concatenate([x[0], ctx[0]], axis=0)
    for l in range(depth):
        need_ctx = l < depth - 1
        rows = t_all if need_ctx else n_lat
        lam_init = 0.8 - 0.6 * math.exp(-0.3 * l)
        mods = mods_all[l]

        h = _modnorm(xa, attn_norm[l], mods, 0, n_lat)
        proj = _matmul([h], [(w_in_t, l, 0, 0, d, 0)], rows=t_all, n=_MAIN_WIDTH, tm=_row_tile(t_all), tn=256,
                       nt=True, out_dtype=BF16, name="in_proj")
        gates = _matmul([h], [(w_in_t, l, 0, 0, d, 0)], rows=t_all, n=LANES, col0=_MG, tm=_row_tile(t_all), tn=LANES,
                        nt=True, valid_cols=w_in.shape[-1] - _MG, name="gate_proj")

        dq = _qkprep(proj, _DQ, DIFF_HEADS, diff_q_norm[l], *rope_diff, seg=DIFF_QK_DIM,
                     scale=DIFF_QK_DIM ** -0.5 * LOG2E, split=True)
        dk = _qkprep(proj, _DK, DIFF_HEADS, diff_k_norm[l], *rope_diff, seg=DIFF_QK_DIM, scale=1.0, split=False,
                     ctx_first=True)
        mix_d = _attention(dq, dk, jnp.roll(proj[:, _DV:_GQ], n_ctx, axis=0), n_kv=DIFF_HEADS, n_q_heads=2,
                           n_lat=n_lat, n_ctx=n_ctx, q_rows=rows, diff=(diff_lambda[l], diff_out_norm[l], lam_init))

        gq = _qkprep(proj, _GQ, GQA_HEADS, gqa_q_norm[l], *rope_gqa, seg=GQA_DIM, scale=GQA_DIM ** -0.5 * LOG2E,
                     split=False)
        gk = _qkprep(proj, _GK, GQA_KV_HEADS, gqa_k_norm[l], *rope_gqa, seg=GQA_DIM, scale=1.0, split=False,
                     ctx_first=True)
        mix_g = _attention(gq, gk, jnp.roll(proj[:, _GV:_MQ], n_ctx, axis=0), n_kv=GQA_KV_HEADS,
                           n_q_heads=GQA_HEADS // GQA_KV_HEADS, n_lat=n_lat, n_ctx=n_ctx, q_rows=rows)

        qk = _mlstm_conv(proj, mlstm_conv_w[l], mlstm_conv_b[l], n_lat)
        kt = qk[:, hd:].T
        g_t = gates[:, :4 * MLSTM_HEADS].T.reshape(4, MLSTM_HEADS, t_all)
        g_t = jnp.pad(g_t, ((0, 0), (0, 16 - MLSTM_HEADS), (0, 0)))
        gb = jnp.pad(mlstm_gate_b[l].astype(F32), ((0, 0), (0, 16 - MLSTM_HEADS)))[:, :, None]
        mv, mo = proj[:, _MV:_MO], proj[:, _MO:_MAIN_WIDTH]
        h_f = _mlstm_direction(qk, kt, mv, g_t[0], g_t[1], gb[0], gb[1], vis_fwd, n_lat=n_lat, reverse=False)
        h_b = _mlstm_direction(qk, kt, mv, g_t[2], g_t[3], gb[2], gb[3], vis_bwd, n_lat=n_lat, reverse=True)
        mix_m = _mlstm_finish(h_f, h_b, mo, mlstm_out_norm[l], rows)

        kb = 512
        w_blocks = [(w_out, l, ai, off, kb, (row0 + off) // kb)
                    for ai, (row0, width) in enumerate(((0, 1024), (1024, 1536), (2560, 1536)))
                    for off in range(0, width, kb)]
        x1 = _matmul([mix_d, mix_g, mix_m], w_blocks, rows=rows, n=d, tm=_row_tile(rows), tn=256, kind="resid",
                     resid=(xa, mods, 2), n_lat=n_lat, name="out_proj")

        if l % 2 == 0:
            tok = _modnorm(x1, ffn_norm[l], mods, 3, n_lat)
            d_ff = ffn_w_gate.shape[-1]
            hid = _matmul([tok], [(ffn_w_gate, l // 2, 0, 0, d, 0), (ffn_w_up, l // 2, 0, 0, d, 0)], rows=rows,
                          n=d_ff, tm=_row_tile(rows), tn=256, kind="swiglu", out_dtype=BF16, name="ffn_up")
            w_down = ffn_w_down[l // 2].astype(BF16)[None]
            xa = _matmul([hid], [(w_down, 0, 0, 0, d_ff, 0)], rows=rows, n=d, tm=_row_tile(rows, (768, 512, 256)),
                         tn=256, kind="resid", resid=(x1, mods, 5), n_lat=n_lat, name="ffn_down")
        else:
            assert not need_ctx
            rw = jnp.pad(router_w[l // 2].astype(F32), ((0, 0), (0, LANES - N_EXPERTS)))
            rb = jnp.pad(router_b[l // 2].astype(F32), (0, LANES - N_EXPERTS)).reshape(1, LANES)
            tok, logits = _modnorm(x1, ffn_norm[l], mods, 3, n_lat, router=(rw, rb), out_dtype=F32)
            d0, d1, w0, w1, slot_tok, block_e, n_valid, n_used = _moe_routing(logits[:, :N_EXPERTS], rows)
            hid = _moe_up(tok, exp_w_gate, exp_w_up, l // 2, block_e, n_valid, slot_tok, tn=256)
            ys = _moe_down(hid, exp_w_down, l // 2, block_e, n_valid, n_used, tn=256)
            xa = _moe_combine(ys, x1, d0, d1, w0, w1, mods, 5)
    return xa[:n_lat].reshape(batch, n_lat, d)
```

```python
import functools
import math

import jax
import jax.numpy as jnp
from jax import lax
from jax.experimental import pallas as pl
from jax.experimental.pallas import tpu as pltpu

F32 = jnp.float32
BF16 = jnp.bfloat16
HIGHEST = lax.Precision.HIGHEST

GRID_W = 64
ROPE_THETA = 10000.0
NORM_EPS = 1e-6
LOG2E = math.log2(math.e)

DIFF_HEADS = 8
DIFF_QK_DIM = 64
GQA_HEADS = 12
GQA_KV_HEADS = 4
GQA_DIM = 128
MLSTM_HEADS = 12
MLSTM_DIM = 128
N_EXPERTS = 8
TOP_K = 2

LANES = 128
ROW_TILE = 256
ONES_ROWS = 16
DIFF_Q_TILE = 512
GQA_Q_TILE = 256
CONV_ROW_TILE = 768
MLSTM_CHUNK = 256
MOE_BLK = 1024
MOE_SLAB = 256
GATHER_ROWS = 256
VMEM_CAP = 60 * 1024 * 1024

_DQ, _DK, _DV = 0, 1024, 2048
_GQ, _GK, _GV = 3072, 4608, 5120
_MQ, _MK, _MV, _MO, _MG = 5632, 7168, 8704, 10240, 11776
_MAIN_WIDTH = 11776


def _cparams(sem, vmem_bytes):
    limit = int(min(VMEM_CAP, max(32 * 1024 * 1024, vmem_bytes + (8 << 20))))
    return pltpu.CompilerParams(dimension_semantics=sem, vmem_limit_bytes=limit)


def _row_tile(m, cands=(1408, 1024, 768, 512, 256)):
    for t in cands:
        if m % t == 0:
            return t
    raise ValueError(f"no row tile for {m}")


def _log_sigmoid(x):
    return jnp.minimum(x, 0.0) - jnp.log1p(jnp.exp(-jnp.abs(x)))


def _silu(x):
    return x * jax.nn.sigmoid(x)


def _mod_kernel(c_ref, w_ref, b_ref, o_ref):
    s = _silu(c_ref[...])
    o_ref[0] = jnp.dot(s, w_ref[0], preferred_element_type=F32) + b_ref[0]


def _mods(cc, w_mod, b_mod):
    n_layers, d, n6 = w_mod.shape
    tn = 512
    return pl.pallas_call(
        _mod_kernel,
        grid=(n_layers, n6 // tn),
        in_specs=[pl.BlockSpec((8, d), lambda l, j: (0, 0)),
                  pl.BlockSpec((1, d, tn), lambda l, j: (l, 0, j)),
                  pl.BlockSpec((1, 1, tn), lambda l, j: (l, 0, j))],
        out_specs=pl.BlockSpec((1, 8, tn), lambda l, j: (l, 0, j)),
        out_shape=jax.ShapeDtypeStruct((n_layers, 8, n6), F32),
        compiler_params=_cparams(("parallel", "parallel"), 2 * d * tn * 4),
        name="adaln_mods",
    )(cc, w_mod, b_mod.reshape(n_layers, 1, n6))


def _modnorm_kernel(x_ref, g_ref, sh_ref, sc_ref, *rest, n_lat, tm, with_router):
    x = x_ref[...]
    ms = jnp.mean(x * x, axis=-1, keepdims=True)
    y = x * lax.rsqrt(ms + NORM_EPS) * g_ref[...]
    rows = pl.program_id(0) * tm + lax.broadcasted_iota(jnp.int32, (tm, 1), 0)
    is_ctx = rows >= n_lat
    sc = jnp.where(is_ctx, sc_ref[1:2, :], sc_ref[0:1, :])
    sh = jnp.where(is_ctx, sh_ref[1:2, :], sh_ref[0:1, :])
    h = y * (1.0 + sc) + sh
    if with_router:
        rw_ref, rb_ref, o_ref, lg_ref = rest
        lg_ref[...] = jnp.dot(h, rw_ref[...], preferred_element_type=F32, precision=HIGHEST) + rb_ref[...]
    else:
        (o_ref,) = rest
    o_ref[...] = h.astype(o_ref.dtype)


def _modnorm(x, gain, mods, shift_idx, n_lat, router=None, out_dtype=BF16):
    m, d = x.shape
    tm = ROW_TILE
    in_specs = [pl.BlockSpec((tm, d), lambda i: (i, 0)),
                pl.BlockSpec((1, d), lambda i: (0, 0)),
                pl.BlockSpec((8, d), lambda i: (0, shift_idx)),
                pl.BlockSpec((8, d), lambda i: (0, shift_idx + 1))]
    args = [x, gain.reshape(1, d), mods, mods]
    out_specs = pl.BlockSpec((tm, d), lambda i: (i, 0))
    out_shape = jax.ShapeDtypeStruct((m, d), out_dtype)
    if router is not None:
        rw, rb = router
        in_specs += [pl.BlockSpec((d, LANES), lambda i: (0, 0)),
                     pl.BlockSpec((1, LANES), lambda i: (0, 0))]
        args += [rw, rb]
        out_specs = [out_specs, pl.BlockSpec((tm, LANES), lambda i: (i, 0))]
        out_shape = [out_shape, jax.ShapeDtypeStruct((m, LANES), F32)]
    return pl.pallas_call(
        functools.partial(_modnorm_kernel, n_lat=n_lat, tm=tm, with_router=router is not None),
        grid=(m // tm,),
        in_specs=in_specs, out_specs=out_specs, out_shape=out_shape,
        compiler_params=_cparams(("parallel",), 2 * tm * d * 6 + 4 * tm * d * 4 + d * LANES * 8),
        name="modnorm",
    )(*args)


def _mxu_operand(w):
    return w if w.dtype == BF16 else w.astype(BF16)


def _mm_kernel(*refs, a_of_w, kind, n_lat, tm, nt=False, valid_cols=None):
    n_a = 1 + max(ai for ai, _ in a_of_w)
    n_w = len(a_of_w)
    a_refs, w_refs, rest = refs[:n_a], refs[n_a:n_a + n_w], refs[n_a + n_w:]
    col_axis, k_axis = (0, 1) if nt else (1, 0)

    def lhs(i):
        ai, off = a_of_w[i]
        return a_refs[ai][:, off:off + w_refs[i].shape[k_axis]]

    def rhs(i):
        w = w_refs[i][...]
        if valid_cols is not None:
            w = jnp.where(lax.broadcasted_iota(jnp.int32, w.shape, col_axis) < valid_cols, w, 0.0)
        return _mxu_operand(w)

    def mm(i):
        return lax.dot_general(lhs(i), rhs(i), (((1,), (k_axis,)), ((), ())), preferred_element_type=F32)

    if kind == "swiglu":
        g, u = mm(0), mm(1)
        (o_ref,) = rest
        o_ref[...] = (_silu(g) * u).astype(o_ref.dtype)
        return
    acc = mm(0)
    for i in range(1, n_w):
        acc = acc + mm(i)
    if kind == "plain":
        (o_ref,) = rest
        o_ref[...] = acc.astype(o_ref.dtype)
    else:
        x_ref, g_ref, o_ref = rest
        rows = pl.program_id(0) * tm + lax.broadcasted_iota(jnp.int32, (tm, 1), 0)
        gate = jnp.where(rows >= n_lat, g_ref[1:2, :], g_ref[0:1, :])
        o_ref[...] = x_ref[...] + gate * acc


def _matmul(a_list, w_list, *, rows, n, tm, tn, col0=0, kind="plain", out_dtype=F32, resid=None, n_lat=0,
            nt=False, valid_cols=None, name="matmul"):
    assert col0 % tn == 0
    cb = col0 // tn
    in_specs = [pl.BlockSpec((tm, a.shape[1]), lambda i, j: (i, 0)) for a in a_list]
    args = list(a_list)
    vmem = sum(2 * tm * a.shape[1] * 2 for a in a_list) + 6 * tm * tn * 4
    for w3d, lead, _, _, k_rows, k_blk in w_list:
        if nt:
            in_specs.append(pl.BlockSpec((None, tn, k_rows), lambda i, j, lead=lead, k_blk=k_blk: (lead, cb + j, k_blk)))
        else:
            in_specs.append(pl.BlockSpec((None, k_rows, tn), lambda i, j, lead=lead, k_blk=k_blk: (lead, k_blk, cb + j)))
        args.append(w3d)
        vmem += k_rows * tn * (2 * w3d.dtype.itemsize + 2)
    if kind == "resid":
        x, mods, gate_idx = resid
        d = x.shape[1]
        in_specs += [pl.BlockSpec((tm, tn), lambda i, j: (i, j)),
                     pl.BlockSpec((8, tn), lambda i, j: (0, gate_idx * (d // tn) + j))]
        args += [x, mods]
    a_of_w = tuple((a_idx, a_off) for _, _, a_idx, a_off, _, _ in w_list)
    return pl.pallas_call(
        functools.partial(_mm_kernel, a_of_w=a_of_w, kind=kind, n_lat=n_lat, tm=tm, nt=nt, valid_cols=valid_cols),
        grid=(rows // tm, n // tn),
        in_specs=in_specs,
        out_specs=pl.BlockSpec((tm, tn), lambda i, j: (i, j)),
        out_shape=jax.ShapeDtypeStruct((rows, n), out_dtype),
        compiler_params=_cparams(("parallel", "arbitrary"), vmem),
        name=name,
    )(*args)


def _qkprep_kernel(x_ref, g_ref, c_ref, s_ref, o_ref, *, n_tiles, seg, scale, split):
    lane = lax.broadcasted_iota(jnp.int32, c_ref.shape, 1)
    gain, cos, sin = g_ref[...], c_ref[...], s_ref[...]
    half = seg // 2
    ow = 2 * LANES if split else LANES
    for j in range(n_tiles):
        x = x_ref[:, j * LANES:(j + 1) * LANES].astype(F32)
        x2 = x * x
        if seg == LANES:
            ms = jnp.sum(x2, axis=-1, keepdims=True) * (1.0 / seg)
        else:
            lo = lane < seg
            s_lo = jnp.sum(jnp.where(lo, x2, 0.0), axis=-1, keepdims=True)
            s_hi = jnp.sum(jnp.where(lo, 0.0, x2), axis=-1, keepdims=True)
            ms = jnp.where(lo, s_lo, s_hi) * (1.0 / seg)
        y = x * lax.rsqrt(ms + NORM_EPS) * gain
        if seg == LANES:
            partner = pltpu.roll(y, half, 1)
        else:
            partner = jnp.where((lane % seg) < half, pltpu.roll(y, LANES - half, 1), pltpu.roll(y, half, 1))
        y = (y * cos + partner * sin) * scale
        if split:
            o_ref[:, j * ow:j * ow + LANES] = jnp.where(lane < seg, y, 0.0).astype(o_ref.dtype)
            o_ref[:, j * ow + LANES:(j + 1) * ow] = jnp.where(lane < seg, 0.0, y).astype(o_ref.dtype)
        else:
            o_ref[:, j * ow:(j + 1) * ow] = y.astype(o_ref.dtype)


def _qkprep(proj, col0, n_tiles, gain, rope_c, rope_s, *, seg, scale, split, ctx_first=False):
    t = proj.shape[0]
    tm = ROW_TILE
    n_row_tiles = t // tm
    shift = 1 if ctx_first else 0
    width = n_tiles * LANES
    assert col0 % width == 0
    cb = col0 // width
    g = jnp.tile(gain.astype(F32), LANES // seg).reshape(1, LANES)
    ow = (2 if split else 1) * width
    return pl.pallas_call(
        functools.partial(_qkprep_kernel, n_tiles=n_tiles, seg=seg, scale=scale, split=split),
        grid=(t // tm,),
        in_specs=[pl.BlockSpec((tm, width), lambda i: (i, cb)),
                  pl.BlockSpec((1, LANES), lambda i: (0, 0)),
                  pl.BlockSpec((tm, LANES), lambda i: (i, 0)),
                  pl.BlockSpec((tm, LANES), lambda i: (i, 0))],
        out_specs=pl.BlockSpec((tm, ow), lambda i: ((i + shift) % n_row_tiles, 0)),
        out_shape=jax.ShapeDtypeStruct((t, ow), BF16),
        compiler_params=_cparams(("parallel",), 0),
        name="qk_prep",
    )(proj, g, rope_c, rope_s)


def _attn_kernel(q_ref, k_ref, vt_ref, *rest, n_q_heads, tq, tk, lam_init):
    n_keys = k_ref.shape[0]
    qs = [q_ref[:, g * LANES:(g + 1) * LANES] for g in range(n_q_heads)]

    def scores(kc):
        return tuple(lax.dot_general(kc, q, (((1,), (1,)), ((), ())), preferred_element_type=F32) for q in qs)

    def update(ss, vtc, carry):
        new = []
        for s, (m, acc) in zip(ss, carry):
            m_new = jnp.maximum(m, jnp.max(s, axis=0, keepdims=True))
            p = jnp.exp2(s - m_new).astype(BF16)
            new.append((m_new, jnp.exp2(m - m_new) * acc + jnp.dot(vtc, p, preferred_element_type=F32)))
        return tuple(new)

    carry = tuple((jnp.full((1, tq), -jnp.inf, F32), jnp.zeros((vt_ref.shape[0], tq), F32))
                  for _ in range(n_q_heads))
    ss = scores(k_ref[0:tk, :])
    for c in range(n_keys // tk):
        ss_next = scores(k_ref[(c + 1) * tk:(c + 2) * tk, :]) if (c + 1) * tk < n_keys else None
        carry = update(ss, vt_ref[:, c * tk:(c + 1) * tk], carry)
        ss = ss_next
    outs_t = [acc[:LANES] / acc[LANES:LANES + 1] for _, acc in carry]
    if lam_init is None:
        (o_ref,) = rest
        for g in range(n_q_heads):
            o_ref[:, g * LANES:(g + 1) * LANES] = outs_t[g].T.astype(o_ref.dtype)
    else:
        lam_ref, g_ref, o_ref = rest
        lp = lam_ref[...]
        lam = (jnp.exp(jnp.sum(lp[0:1, :] * lp[1:2, :], axis=-1, keepdims=True))
               - jnp.exp(jnp.sum(lp[2:3, :] * lp[3:4, :], axis=-1, keepdims=True)) + lam_init)
        o = (outs_t[0] - lam * outs_t[1]).T
        ms = jnp.mean(o * o, axis=-1, keepdims=True)
        o_ref[...] = (o * lax.rsqrt(ms + NORM_EPS) * g_ref[...] * (1.0 - lam_init)).astype(o_ref.dtype)


def _attention(q, k, v, *, n_kv, n_q_heads, n_lat, n_ctx, with_ctx_queries, tq, diff=None):
    t = k.shape[0]
    qw = n_q_heads * LANES
    ow = LANES if diff is not None else qw
    hv = LANES + ONES_ROWS
    vt = jnp.concatenate([v.T.reshape(n_kv, LANES, t), jnp.ones((n_kv, ONES_ROWS, t), v.dtype)], axis=1)
    vt = vt.reshape(n_kv * hv, t)
    extra_specs, extra_args, lam_init = [], [], None
    if diff is not None:
        lam_p, gain, lam_init = diff
        extra_specs = [pl.BlockSpec((4, DIFF_QK_DIM), lambda h, i: (0, 0)), pl.BlockSpec((1, LANES), lambda h, i: (0, 0))]
        extra_args = [lam_p.astype(F32), gain.astype(F32).reshape(1, LANES)]

    def call(q_rows, q_row0, n_keys, tq, tk):
        qb0 = q_row0 // tq
        vmem = 2 * 2 * n_keys * LANES * 2 + 4 * tq * qw * 4 + 8 * n_q_heads * tq * tk * 4
        return pl.pallas_call(
            functools.partial(_attn_kernel, n_q_heads=n_q_heads, tq=tq, tk=tk, lam_init=lam_init),
            grid=(n_kv, q_rows // tq),
            in_specs=[pl.BlockSpec((tq, qw), lambda h, i: (qb0 + i, h)),
                      pl.BlockSpec((n_keys, LANES), lambda h, i: (0, h)),
                      pl.BlockSpec((hv, n_keys), lambda h, i: (h, 0))] + extra_specs,
            out_specs=pl.BlockSpec((tq, ow), lambda h, i: (i, h)),
            out_shape=jax.ShapeDtypeStruct((q_rows, n_kv * ow), BF16),
            compiler_params=_cparams(("parallel", "arbitrary"), vmem),
            name="attention",
        )(q, k, vt, *extra_args)

    tq = math.gcd(tq, n_lat)
    out = call(n_lat, 0, t, tq, next(c for c in (768, 512, 256) if t % c == 0))
    if with_ctx_queries:
        out = jnp.concatenate([out, call(n_ctx, n_lat, n_ctx, n_ctx, n_ctx)], axis=0)
    return out


def _conv_kernel(x_ref, xp_ref, xn_ref, w_ref, b_ref, o_ref, *, n_lat, t_all, tm, k_col_tile):
    i, j = pl.program_id(0), pl.program_id(1)
    x = x_ref[...].astype(F32)
    row = lax.broadcasted_iota(jnp.int32, (tm, 1), 0)
    grow = i * tm + row
    prev = jnp.where(row == 0, xp_ref[15:16, :].astype(F32), pltpu.roll(x, 1, 0))
    nxt = jnp.where(row == tm - 1, xn_ref[0:1, :].astype(F32), pltpu.roll(x, tm - 1, 0))
    prev = jnp.where((grow == 0) | (grow == n_lat), 0.0, prev)
    nxt = jnp.where((grow == n_lat - 1) | (grow == t_all - 1), 0.0, nxt)
    y = b_ref[...] + prev * w_ref[0:1, :] + x * w_ref[1:2, :] + nxt * w_ref[2:3, :]
    y = _silu(y) * jnp.where(j >= k_col_tile, MLSTM_DIM ** -0.5, 1.0)
    o_ref[...] = y.astype(o_ref.dtype)


def _mlstm_conv(proj, conv_w, conv_b, n_lat):
    t = proj.shape[0]
    tm, tc, hb = (CONV_ROW_TILE if t % CONV_ROW_TILE == 0 else ROW_TILE), 512, 16
    width = 2 * MLSTM_HEADS * MLSTM_DIM
    cb = _MQ // tc
    nrb = t // hb
    return pl.pallas_call(
        functools.partial(_conv_kernel, n_lat=n_lat, t_all=t, tm=tm, k_col_tile=(width // 2) // tc),
        grid=(t // tm, width // tc),
        in_specs=[pl.BlockSpec((tm, tc), lambda i, j: (i, cb + j)),
                  pl.BlockSpec((hb, tc), lambda i, j: (jnp.maximum(i * (tm // hb) - 1, 0), cb + j)),
                  pl.BlockSpec((hb, tc), lambda i, j: (jnp.minimum((i + 1) * (tm // hb), nrb - 1), cb + j)),
                  pl.BlockSpec((3, tc), lambda i, j: (0, j)),
                  pl.BlockSpec((1, tc), lambda i, j: (0, j))],
        out_specs=pl.BlockSpec((tm, tc), lambda i, j: (i, j)),
        out_shape=jax.ShapeDtypeStruct((t, width), BF16),
        compiler_params=_cparams(("parallel", "arbitrary"), 0),
        name="mlstm_conv",
    )(proj, proj, proj, conv_w.astype(F32), conv_b.astype(F32).reshape(1, width))


def _mlstm_kernel(q_ref, k_ref, vt_ref, gi_ref, gf_ref, bi_ref, bf_ref, mask_ref, ht_ref, ct_sc, n_sc, m_sc, *, n_heads):
    @pl.when(pl.program_id(0) == 0)
    def _():
        ct_sc[...] = jnp.zeros_like(ct_sc)
        n_sc[...] = jnp.zeros_like(n_sc)
        m_sc[...] = jnp.zeros_like(m_sc)

    vis_t = mask_ref[...]
    li_all = gi_ref[...] + bi_ref[...]
    lf_all = _log_sigmoid(gf_ref[...] + bf_ref[...])
    br_all = jnp.dot(lf_all, vis_t, preferred_element_type=F32, precision=HIGHEST)
    bend_all = jnp.sum(lf_all, axis=1, keepdims=True)
    c_cols = (br_all - li_all).T
    d = MLSTM_DIM
    nt = (((1,), (1,)), ((), ()))
    for h in range(n_heads):
        lir, br = li_all[h:h + 1, :], br_all[h:h + 1, :]
        b_end = bend_all[h:h + 1, :]
        m_prev = m_sc[h:h + 1, 0:1]
        lw = b_end - br + lir
        m_new = jnp.maximum(b_end + m_prev, jnp.max(lw, axis=1, keepdims=True))
        w_end = jnp.exp(lw - m_new)
        decay = jnp.exp(b_end + m_prev - m_new)
        log_d = jnp.where(vis_t > 0.0, br - c_cols[:, h:h + 1], -jnp.inf)
        m_t = jnp.maximum(br + m_prev, jnp.max(log_d, axis=0, keepdims=True))
        dmat = jnp.exp(log_d - m_t)
        inter = jnp.exp(br + m_prev - m_t)
        qh = q_ref[:, h * d:(h + 1) * d]
        kh = k_ref[:, h * d:(h + 1) * d]
        vth = vt_ref[h * d:(h + 1) * d, :]
        ct_prev = ct_sc[h]
        n_prev = n_sc[h:h + 1, :]
        sc = lax.dot_general(kh, qh, nt, preferred_element_type=F32) * dmat
        num = (jnp.dot(vth, sc.astype(BF16), preferred_element_type=F32)
               + inter * lax.dot_general(ct_prev.astype(BF16), qh, nt, preferred_element_type=F32))
        n8 = jnp.broadcast_to(n_prev, (8, d)).astype(BF16)
        den = (jnp.sum(sc, axis=0, keepdims=True)
               + inter * lax.dot_general(n8, qh, nt, preferred_element_type=F32)[0:1, :])
        ht_ref[h * d:(h + 1) * d, :] = num / jnp.maximum(jnp.abs(den), jnp.exp(-m_t))
        vtw = (vth.astype(F32) * w_end).astype(BF16)
        ct_sc[h] = decay * ct_prev + jnp.dot(vtw, kh, preferred_element_type=F32)
        w8 = jnp.broadcast_to(w_end, (8, w_end.shape[1])).astype(BF16)
        n_sc[h:h + 1, :] = decay * n_prev + jnp.dot(w8, kh, preferred_element_type=F32)[0:1, :]
        m_sc[h:h + 1, :] = jnp.broadcast_to(m_new, (1, LANES))


def _mlstm_direction(qk, vt, gi, gf, bi, bf, vis_t, *, n_lat, reverse):
    t = qk.shape[0]
    lc = MLSTM_CHUNK
    n_chunks = t // lc
    ctx_blk = n_lat // lc
    hd = MLSTM_HEADS * MLSTM_DIM

    def blk(c):
        lat = (n_chunks - 1 - c) if reverse else (c - 1)
        return jnp.where(c == 0, ctx_blk, lat)

    return pl.pallas_call(
        functools.partial(_mlstm_kernel, n_heads=MLSTM_HEADS),
        grid=(n_chunks,),
        in_specs=[pl.BlockSpec((lc, hd), lambda c: (blk(c), 0)),
                  pl.BlockSpec((lc, hd), lambda c: (blk(c), 1)),
                  pl.BlockSpec((hd, lc), lambda c: (0, blk(c))),
                  pl.BlockSpec((16, lc), lambda c: (0, blk(c))),
                  pl.BlockSpec((16, lc), lambda c: (0, blk(c))),
                  pl.BlockSpec((16, 1), lambda c: (0, 0)),
                  pl.BlockSpec((16, 1), lambda c: (0, 0)),
                  pl.BlockSpec((lc, lc), lambda c: (0, 0))],
        out_specs=pl.BlockSpec((hd, lc), lambda c: (0, blk(c))),
        out_shape=jax.ShapeDtypeStruct((hd, t), F32),
        scratch_shapes=[pltpu.VMEM((MLSTM_HEADS, MLSTM_DIM, MLSTM_DIM), F32),
                        pltpu.VMEM((16, LANES), F32),
                        pltpu.VMEM((16, LANES), F32)],
        compiler_params=_cparams(("arbitrary",), 16 << 20),
        name="mlstm_bwd" if reverse else "mlstm_fwd",
    )(qk, qk, vt, gi, gf, bi, bf, vis_t)


def _mlstm_finish_kernel(hf_ref, hb_ref, o0_ref, o1_ref, o2_ref, g_ref, out_ref, *, n_heads):
    d = MLSTM_DIM
    o_refs = (o0_ref, o1_ref, o2_ref)
    per_blk = o0_ref.shape[1] // d
    for h in range(n_heads):
        sl = slice(h * d, (h + 1) * d)
        s = hf_ref[sl, :] + hb_ref[sl, :]
        ms = jnp.mean(s * s, axis=0, keepdims=True)
        y = (s * lax.rsqrt(ms + NORM_EPS)).T * g_ref[:, sl]
        o = o_refs[h // per_blk][:, (h % per_blk) * d:(h % per_blk + 1) * d]
        out_ref[:, sl] = (jax.nn.sigmoid(o.astype(F32)) * y).astype(out_ref.dtype)


def _mlstm_finish(hf_t, hb_t, proj, out_gain, rows):
    hd = MLSTM_HEADS * MLSTM_DIM
    tm = ROW_TILE
    ow = hd // 3
    assert _MO % ow == 0
    return pl.pallas_call(
        functools.partial(_mlstm_finish_kernel, n_heads=MLSTM_HEADS),
        grid=(rows // tm,),
        in_specs=[pl.BlockSpec((hd, tm), lambda i: (0, i)),
                  pl.BlockSpec((hd, tm), lambda i: (0, i)),
                  pl.BlockSpec((tm, ow), lambda i: (i, _MO // ow)),
                  pl.BlockSpec((tm, ow), lambda i: (i, _MO // ow + 1)),
                  pl.BlockSpec((tm, ow), lambda i: (i, _MO // ow + 2)),
                  pl.BlockSpec((1, hd), lambda i: (0, 0))],
        out_specs=pl.BlockSpec((tm, hd), lambda i: (i, 0)),
        out_shape=jax.ShapeDtypeStruct((rows, hd), BF16),
        compiler_params=_cparams(("parallel",), 0),
        name="mlstm_finish",
    )(hf_t, hb_t, proj, proj, proj, out_gain.astype(F32).reshape(1, hd))


def _moe_up_kernel(be_ref, nv_ref, st_ref, tok_ref, wg_ref, wu_ref, o_ref, xbuf, xb, sem, *, nj):
    b, j = pl.program_id(0), pl.program_id(1)
    nb = pl.num_programs(0)
    n_valid = nv_ref[b]
    per_step = MOE_BLK // nj

    def copy(blk, r):
        src = tok_ref.at[pl.ds(st_ref[blk * MOE_BLK + r], 1), :]
        return pltpu.make_async_copy(src, xbuf.at[pl.ds(r, 1), :], sem)

    @pl.when((b == 0) & (j == 0) & (n_valid > 0))
    def _():
        def body(r, carry):
            copy(0, r).start()
            return carry
        lax.fori_loop(0, MOE_BLK, body, 0)

    @pl.when((j == 0) & (n_valid > 0))
    def _():
        pltpu.make_async_copy(tok_ref.at[pl.ds(0, MOE_BLK), :], xbuf, sem).wait()
        xb[...] = xbuf[...].astype(BF16)

    @pl.when((b + 1 < nb) & (nv_ref[jnp.minimum(b + 1, nb - 1)] > 0))
    def _():
        for r in range(per_step):
            copy(b + 1, j * per_step + r).start()

    def swiglu(sl):
        a = xb[sl, :]
        g = jnp.dot(a, _mxu_operand(wg_ref[...]), preferred_element_type=F32)
        u = jnp.dot(a, _mxu_operand(wu_ref[...]), preferred_element_type=F32)
        o_ref[sl, :] = (_silu(g) * u).astype(o_ref.dtype)

    @pl.when(n_valid == MOE_BLK)
    def _():
        swiglu(slice(None))

    @pl.when((n_valid > 0) & (n_valid < MOE_BLK))
    def _():
        for s in range(MOE_BLK // MOE_SLAB):
            sl = slice(s * MOE_SLAB, (s + 1) * MOE_SLAB)

            @pl.when(n_valid > s * MOE_SLAB)
            def _():
                swiglu(sl)

            @pl.when(n_valid <= s * MOE_SLAB)
            def _():
                o_ref[sl, :] = jnp.zeros((MOE_SLAB, o_ref.shape[1]), o_ref.dtype)

    @pl.when(n_valid == 0)
    def _():
        o_ref[...] = jnp.zeros_like(o_ref)


def _moe_up(tok, w_gate, w_up, lead, block_e, n_valid, slot_tok, *, tn):
    n_slots = slot_tok.shape[0]
    k = tok.shape[1]
    n = w_gate.shape[-1]
    nb, nj = n_slots // MOE_BLK, n // tn
    assert MOE_BLK % nj == 0

    def w_map(b, j, be, nv, st):
        return (lead, be[b], 0, jnp.where(nv[b] > 0, j, nj - 1))

    return pl.pallas_call(
        functools.partial(_moe_up_kernel, nj=nj),
        grid_spec=pltpu.PrefetchScalarGridSpec(
            num_scalar_prefetch=3,
            grid=(nb, nj),
            in_specs=[pl.BlockSpec(memory_space=pl.ANY),
                      pl.BlockSpec((None, None, k, tn), w_map),
                      pl.BlockSpec((None, None, k, tn), w_map)],
            out_specs=pl.BlockSpec((MOE_BLK, tn), lambda b, j, be, nv, st: (b, j)),
            scratch_shapes=[pltpu.VMEM((MOE_BLK, k), F32),
                            pltpu.VMEM((MOE_BLK, k), BF16),
                            pltpu.SemaphoreType.DMA(())]),
        out_shape=jax.ShapeDtypeStruct((n_slots, n), BF16),
        compiler_params=_cparams(("arbitrary", "arbitrary"),
                                 MOE_BLK * k * 6 + 2 * k * tn * (2 * w_gate.dtype.itemsize + 2)
                                 + 4 * MOE_BLK * tn * 4),
        name="moe_up",
    )(block_e, n_valid, slot_tok, tok, w_gate, w_up)


def _moe_down_kernel(be_ref, nv_ref, nu_ref, a_ref, w_ref, o_ref):
    n_valid = nv_ref[pl.program_id(0)]

    def down(sl):
        o_ref[sl, :] = jnp.dot(a_ref[sl, :], _mxu_operand(w_ref[...]), preferred_element_type=F32)

    @pl.when(n_valid == MOE_BLK)
    def _():
        down(slice(None))

    @pl.when((n_valid > 0) & (n_valid < MOE_BLK))
    def _():
        for s in range(MOE_BLK // MOE_SLAB):
            sl = slice(s * MOE_SLAB, (s + 1) * MOE_SLAB)

            @pl.when(n_valid > s * MOE_SLAB)
            def _():
                down(sl)

            @pl.when(n_valid <= s * MOE_SLAB)
            def _():
                o_ref[sl, :] = jnp.zeros((MOE_SLAB, o_ref.shape[1]), o_ref.dtype)

    @pl.when(n_valid == 0)
    def _():
        o_ref[...] = jnp.zeros_like(o_ref)


def _moe_down(a, w, lead, block_e, n_valid, n_used, *, tn):
    n_slots, k = a.shape
    n = w.shape[-1]
    nb, nj = n_slots // MOE_BLK, n // tn

    def a_map(b, j, be, nv, nu):
        return (jnp.minimum(b, nu[0] - 1), 0)

    def w_map(b, j, be, nv, nu):
        return (lead, be[b], 0, jnp.where(nv[b] > 0, j, nj - 1))

    return pl.pallas_call(
        _moe_down_kernel,
        grid_spec=pltpu.PrefetchScalarGridSpec(
            num_scalar_prefetch=3,
            grid=(nb, nj),
            in_specs=[pl.BlockSpec((MOE_BLK, k), a_map), pl.BlockSpec((None, None, k, tn), w_map)],
            out_specs=pl.BlockSpec((MOE_BLK, tn), lambda b, j, be, nv, nu: (b, j))),
        out_shape=jax.ShapeDtypeStruct((n_slots, n), F32),
        compiler_params=_cparams(("arbitrary", "arbitrary"),
                                 2 * MOE_BLK * k * 2 + k * tn * (2 * w.dtype.itemsize + 2) + 4 * MOE_BLK * tn * 4),
        name="moe_down",
    )(block_e, n_valid, n_used, a, w)


def _combine_kernel(d0_ref, d1_ref, y_ref, x_ref, w0_ref, w1_ref, g_ref, o_ref, buf, sem, *, rows):
    base = pl.program_id(0) * rows

    def copy(r, which):
        idx = d0_ref[base + r] if which == 0 else d1_ref[base + r]
        return pltpu.make_async_copy(y_ref.at[pl.ds(idx, 1), :], buf.at[which, pl.ds(r, 1), :], sem)

    def start(r, carry):
        copy(r, 0).start()
        copy(r, 1).start()
        return carry

    lax.fori_loop(0, rows, start, 0, unroll=8)
    for which in range(2):
        pltpu.make_async_copy(y_ref.at[pl.ds(0, rows), :], buf.at[which], sem).wait()
    y = w0_ref[...] * buf[0] + w1_ref[...] * buf[1]
    o_ref[...] = x_ref[...] + g_ref[0:1, :] * y


def _moe_combine(y_slots, x, d0, d1, w0, w1, mods, gate_idx):
    n_tok, d = x.shape
    rows = GATHER_ROWS // 2
    return pl.pallas_call(
        functools.partial(_combine_kernel, rows=rows),
        grid_spec=pltpu.PrefetchScalarGridSpec(
            num_scalar_prefetch=2,
            grid=(n_tok // rows,),
            in_specs=[pl.BlockSpec(memory_space=pl.ANY),
                      pl.BlockSpec((rows, d), lambda i, a, b: (i, 0)),
                      pl.BlockSpec((rows, 1), lambda i, a, b: (i, 0)),
                      pl.BlockSpec((rows, 1), lambda i, a, b: (i, 0)),
                      pl.BlockSpec((8, d), lambda i, a, b: (0, gate_idx))],
            out_specs=pl.BlockSpec((rows, d), lambda i, a, b: (i, 0)),
            scratch_shapes=[pltpu.VMEM((2, rows, d), F32), pltpu.SemaphoreType.DMA(())]),
        out_shape=jax.ShapeDtypeStruct((n_tok, d), F32),
        compiler_params=_cparams(("arbitrary",), 8 * rows * d * 4),
        name="moe_combine",
    )(d0, d1, y_slots, x, w0, w1, mods)


def _moe_routing(logits, n_tok):
    top_logit, top_idx = lax.top_k(logits, TOP_K)
    top_w = jax.nn.softmax(top_logit, axis=-1)
    flat_e = top_idx.reshape(-1).astype(jnp.int32)
    n_assign = n_tok * TOP_K
    onehot = (flat_e[:, None] == jnp.arange(N_EXPERTS, dtype=jnp.int32)[None, :]).astype(jnp.int32)
    csum = jnp.cumsum(onehot, axis=0)
    rank = jnp.take_along_axis(csum, flat_e[:, None], axis=1)[:, 0] - 1
    counts = csum[-1]
    padded = (counts + MOE_BLK - 1) // MOE_BLK * MOE_BLK
    pad_ends = jnp.cumsum(padded)
    pad_starts = pad_ends - padded
    dest = (pad_starts[flat_e] + rank).astype(jnp.int32)
    n_blocks = -(-(n_assign + N_EXPERTS * (MOE_BLK - 1)) // MOE_BLK)
    n_slots = n_blocks * MOE_BLK
    slot_tok = jnp.zeros((n_slots,), jnp.int32).at[dest].set(jnp.arange(n_assign, dtype=jnp.int32) // TOP_K)
    blk_start = jnp.arange(n_blocks, dtype=jnp.int32) * MOE_BLK
    block_e = jnp.minimum(jnp.searchsorted(pad_ends, blk_start, side="right"), N_EXPERTS - 1).astype(jnp.int32)
    slab_ends = pad_starts + (counts + MOE_SLAB - 1) // MOE_SLAB * MOE_SLAB
    n_valid = jnp.clip(slab_ends[block_e] - blk_start, 0, MOE_BLK).astype(jnp.int32)
    n_used = (pad_ends[-1] // MOE_BLK).astype(jnp.int32).reshape(1)
    dest2 = dest.reshape(n_tok, TOP_K)
    return dest2[:, 0], dest2[:, 1], top_w[:, 0:1], top_w[:, 1:2], slot_tok, block_e, n_valid, n_used


def _rope_tables(n_lat, n_ctx, dim):
    rows = n_lat // GRID_W
    row = jnp.broadcast_to(jnp.arange(rows)[:, None], (rows, GRID_W)).reshape(-1).astype(F32)
    col = jnp.broadcast_to(jnp.arange(GRID_W)[None, :], (rows, GRID_W)).reshape(-1).astype(F32)
    quarter = dim // 4
    inv = ROPE_THETA ** (-jnp.arange(quarter, dtype=F32) / quarter)
    ang = jnp.concatenate([row[:, None] * inv, col[:, None] * inv], axis=-1)
    cos, sin = jnp.cos(ang), jnp.sin(ang)
    reps = LANES // dim
    c = jnp.tile(jnp.concatenate([cos, cos], axis=-1), (1, reps))
    s = jnp.tile(jnp.concatenate([-sin, sin], axis=-1), (1, reps))
    c = jnp.concatenate([c, jnp.ones((n_ctx, LANES), F32)], axis=0)
    s = jnp.concatenate([s, jnp.zeros((n_ctx, LANES), F32)], axis=0)
    return c, s


def kernel(x, c, ctx, c_ctx, w_mod, b_mod, attn_norm, ffn_norm, w_in, w_out, diff_q_norm, diff_k_norm, diff_lambda, diff_out_norm, gqa_q_norm, gqa_k_norm, mlstm_conv_w, mlstm_conv_b, mlstm_gate_b, mlstm_out_norm, ffn_w_gate, ffn_w_up, ffn_w_down, router_w, router_b, exp_w_gate, exp_w_up, exp_w_down):
    batch, n_lat, d = x.shape
    n_ctx = ctx.shape[1]
    depth = w_in.shape[0]
    assert batch == 1 and n_ctx == ROW_TILE and n_lat % MLSTM_CHUNK == 0 and n_lat % GRID_W == 0
    t_all = n_lat + n_ctx
    hd = MLSTM_HEADS * MLSTM_DIM

    cc = jnp.zeros((8, d), F32).at[0].set(c[0]).at[1].set(c_ctx)
    mods_all = _mods(cc, w_mod, b_mod)
    rope_diff = _rope_tables(n_lat, n_ctx, DIFF_QK_DIM)
    rope_gqa = _rope_tables(n_lat, n_ctx, GQA_DIM)
    pos = jnp.arange(MLSTM_CHUNK)
    vis_fwd = (pos[None, :] <= pos[:, None]).astype(F32)
    vis_bwd = (pos[None, :] >= pos[:, None]).astype(F32)

    w_in_t = jnp.swapaxes(w_in, 1, 2)
    xa = jnp.concatenate([x[0], ctx[0]], axis=0)
    for l in range(depth):
        need_ctx = l < depth - 1
        rows = t_all if need_ctx else n_lat
        lam_init = 0.8 - 0.6 * math.exp(-0.3 * l)
        mods = mods_all[l]

        h = _modnorm(xa, attn_norm[l], mods, 0, n_lat)
        proj = _matmul([h], [(w_in_t, l, 0, 0, d, 0)], rows=t_all, n=_MAIN_WIDTH, tm=_row_tile(t_all), tn=256,
                       nt=True, out_dtype=BF16, name="in_proj")
        gates = _matmul([h], [(w_in_t, l, 0, 0, d, 0)], rows=t_all, n=LANES, col0=_MG, tm=_row_tile(t_all), tn=LANES,
                        nt=True, valid_cols=w_in.shape[-1] - _MG, name="gate_proj")

        dq = _qkprep(proj, _DQ, DIFF_HEADS, diff_q_norm[l], *rope_diff, seg=DIFF_QK_DIM,
                     scale=DIFF_QK_DIM ** -0.5 * LOG2E, split=True)
        dk = _qkprep(proj, _DK, DIFF_HEADS, diff_k_norm[l], *rope_diff, seg=DIFF_QK_DIM, scale=1.0, split=False,
                     ctx_first=True)
        mix_d = _attention(dq, dk, jnp.roll(proj[:, _DV:_GQ], n_ctx, axis=0), n_kv=DIFF_HEADS, n_q_heads=2,
                           n_lat=n_lat, n_ctx=n_ctx, with_ctx_queries=need_ctx, tq=DIFF_Q_TILE,
                           diff=(diff_lambda[l], diff_out_norm[l], lam_init))

        gq = _qkprep(proj, _GQ, GQA_HEADS, gqa_q_norm[l], *rope_gqa, seg=GQA_DIM, scale=GQA_DIM ** -0.5 * LOG2E,
                     split=False)
        gk = _qkprep(proj, _GK, GQA_KV_HEADS, gqa_k_norm[l], *rope_gqa, seg=GQA_DIM, scale=1.0, split=False,
                     ctx_first=True)
        mix_g = _attention(gq, gk, jnp.roll(proj[:, _GV:_MQ], n_ctx, axis=0), n_kv=GQA_KV_HEADS,
                           n_q_heads=GQA_HEADS // GQA_KV_HEADS, n_lat=n_lat, n_ctx=n_ctx, with_ctx_queries=need_ctx,
                           tq=GQA_Q_TILE)

        qk = _mlstm_conv(proj, mlstm_conv_w[l], mlstm_conv_b[l], n_lat)
        vt_m = proj[:, _MV:_MO].T
        g_t = gates[:, :4 * MLSTM_HEADS].T.reshape(4, MLSTM_HEADS, t_all)
        g_t = jnp.pad(g_t, ((0, 0), (0, 16 - MLSTM_HEADS), (0, 0)))
        gb = jnp.pad(mlstm_gate_b[l].astype(F32), ((0, 0), (0, 16 - MLSTM_HEADS)))[:, :, None]
        h_f = _mlstm_direction(qk, vt_m, g_t[0], g_t[1], gb[0], gb[1], vis_fwd.T, n_lat=n_lat, reverse=False)
        h_b = _mlstm_direction(qk, vt_m, g_t[2], g_t[3], gb[2], gb[3], vis_bwd.T, n_lat=n_lat, reverse=True)
        mix_m = _mlstm_finish(h_f, h_b, proj, mlstm_out_norm[l], rows)

        kb = 512
        w_blocks = [(w_out, l, ai, off, kb, (row0 + off) // kb)
                    for ai, (row0, width) in enumerate(((0, 1024), (1024, 1536), (2560, 1536)))
                    for off in range(0, width, kb)]
        x1 = _matmul([mix_d, mix_g, mix_m], w_blocks, rows=rows, n=d, tm=_row_tile(rows), tn=256, kind="resid",
                     resid=(xa, mods, 2), n_lat=n_lat, name="out_proj")

        if l % 2 == 0:
            tok = _modnorm(x1, ffn_norm[l], mods, 3, n_lat)
            d_ff = ffn_w_gate.shape[-1]
            hid = _matmul([tok], [(ffn_w_gate, l // 2, 0, 0, d, 0), (ffn_w_up, l // 2, 0, 0, d, 0)], rows=rows,
                          n=d_ff, tm=_row_tile(rows), tn=256, kind="swiglu", out_dtype=BF16, name="ffn_up")
            w_down = ffn_w_down[l // 2].astype(BF16)[None]
            xa = _matmul([hid], [(w_down, 0, 0, 0, d_ff, 0)], rows=rows, n=d, tm=_row_tile(rows, (768, 512, 256)),
                         tn=256, kind="resid", resid=(x1, mods, 5), n_lat=n_lat, name="ffn_down")
        else:
            assert not need_ctx
            rw = jnp.pad(router_w[l // 2].astype(F32), ((0, 0), (0, LANES - N_EXPERTS)))
            rb = jnp.pad(router_b[l // 2].astype(F32), (0, LANES - N_EXPERTS)).reshape(1, LANES)
            tok, logits = _modnorm(x1, ffn_norm[l], mods, 3, n_lat, router=(rw, rb), out_dtype=F32)
            d0, d1, w0, w1, slot_tok, block_e, n_valid, n_used = _moe_routing(logits[:, :N_EXPERTS], rows)
            hid = _moe_up(tok, exp_w_gate, exp_w_up, l // 2, block_e, n_valid, slot_tok, tn=256)
            ys = _moe_down(hid, exp_w_down, l // 2, block_e, n_valid, n_used, tn=256)
            xa = _moe_combine(ys, x1, d0, d1, w0, w1, mods, 5)
    return xa[:n_lat].reshape(batch, n_lat, d)
```

```python
import functools
import math

import jax
import jax.numpy as jnp
import numpy as np
from jax import lax
from jax.experimental import pallas as pl
from jax.experimental.pallas import tpu as pltpu

F32 = jnp.float32
BF16 = jnp.bfloat16
HIGHEST = lax.Precision.HIGHEST

GRID_W = 64
ROPE_THETA = 10000.0
NORM_EPS = 1e-6
LOG2E = math.log2(math.e)

DIFF_HEADS = 8
DIFF_QK_DIM = 64
GQA_HEADS = 12
GQA_KV_HEADS = 4
GQA_DIM = 128
MLSTM_HEADS = 12
MLSTM_DIM = 128
N_EXPERTS = 8
TOP_K = 2

LANES = 128
ROW_TILE = 256
ONES_ROWS = 16
DIFF_Q_TILE = 512
GQA_Q_TILE = 256
CONV_ROW_TILE = 768
MLSTM_CHUNK = 256
MOE_BLK = 1024
MOE_SLAB = 256
GATHER_ROWS = 256
VMEM_CAP = 60 * 1024 * 1024

_DQ, _DK, _DV = 0, 1024, 2048
_GQ, _GK, _GV = 3072, 4608, 5120
_MQ, _MK, _MV, _MO, _MG = 5632, 7168, 8704, 10240, 11776
_MAIN_WIDTH = 11776


def _cparams(sem, vmem_bytes):
    limit = int(min(VMEM_CAP, max(32 * 1024 * 1024, vmem_bytes + (8 << 20))))
    return pltpu.CompilerParams(dimension_semantics=sem, vmem_limit_bytes=limit)


def _row_tile(m, cands=(1408, 1024, 768, 512, 256)):
    for t in cands:
        if m % t == 0:
            return t
    raise ValueError(f"no row tile for {m}")


def _log_sigmoid(x):
    return jnp.minimum(x, 0.0) - jnp.log1p(jnp.exp(-jnp.abs(x)))


def _silu(x):
    return x * jax.nn.sigmoid(x)


def _mod_kernel(c_ref, w_ref, b_ref, o_ref):
    s = _silu(c_ref[...])
    o_ref[0] = jnp.dot(s, w_ref[0], preferred_element_type=F32) + b_ref[0]


def _mods(cc, w_mod, b_mod):
    n_layers, d, n6 = w_mod.shape
    tn = 512
    return pl.pallas_call(
        _mod_kernel,
        grid=(n_layers, n6 // tn),
        in_specs=[pl.BlockSpec((8, d), lambda l, j: (0, 0)),
                  pl.BlockSpec((1, d, tn), lambda l, j: (l, 0, j)),
                  pl.BlockSpec((1, 1, tn), lambda l, j: (l, 0, j))],
        out_specs=pl.BlockSpec((1, 8, tn), lambda l, j: (l, 0, j)),
        out_shape=jax.ShapeDtypeStruct((n_layers, 8, n6), F32),
        compiler_params=_cparams(("parallel", "parallel"), 2 * d * tn * 4),
        name="adaln_mods",
    )(cc, w_mod, b_mod.reshape(n_layers, 1, n6))


def _modnorm_kernel(x_ref, g_ref, sh_ref, sc_ref, *rest, n_lat, tm, with_router):
    x = x_ref[...]
    ms = jnp.mean(x * x, axis=-1, keepdims=True)
    y = x * lax.rsqrt(ms + NORM_EPS) * g_ref[...]
    rows = pl.program_id(0) * tm + lax.broadcasted_iota(jnp.int32, (tm, 1), 0)
    is_ctx = rows >= n_lat
    sc = jnp.where(is_ctx, sc_ref[1:2, :], sc_ref[0:1, :])
    sh = jnp.where(is_ctx, sh_ref[1:2, :], sh_ref[0:1, :])
    h = y * (1.0 + sc) + sh
    if with_router:
        rw_ref, rb_ref, o_ref, lg_ref = rest
        lg_ref[...] = jnp.dot(h, rw_ref[...], preferred_element_type=F32, precision=HIGHEST) + rb_ref[...]
    else:
        (o_ref,) = rest
    o_ref[...] = h.astype(o_ref.dtype)


def _modnorm(x, gain, mods, shift_idx, n_lat, router=None, out_dtype=BF16):
    m, d = x.shape
    tm = ROW_TILE
    in_specs = [pl.BlockSpec((tm, d), lambda i: (i, 0)),
                pl.BlockSpec((1, d), lambda i: (0, 0)),
                pl.BlockSpec((8, d), lambda i: (0, shift_idx)),
                pl.BlockSpec((8, d), lambda i: (0, shift_idx + 1))]
    args = [x, gain.reshape(1, d), mods, mods]
    out_specs = pl.BlockSpec((tm, d), lambda i: (i, 0))
    out_shape = jax.ShapeDtypeStruct((m, d), out_dtype)
    if router is not None:
        rw, rb = router
        in_specs += [pl.BlockSpec((d, LANES), lambda i: (0, 0)),
                     pl.BlockSpec((1, LANES), lambda i: (0, 0))]
        args += [rw, rb]
        out_specs = [out_specs, pl.BlockSpec((tm, LANES), lambda i: (i, 0))]
        out_shape = [out_shape, jax.ShapeDtypeStruct((m, LANES), F32)]
    return pl.pallas_call(
        functools.partial(_modnorm_kernel, n_lat=n_lat, tm=tm, with_router=router is not None),
        grid=(m // tm,),
        in_specs=in_specs, out_specs=out_specs, out_shape=out_shape,
        compiler_params=_cparams(("parallel",), 2 * tm * d * 6 + 4 * tm * d * 4 + d * LANES * 8),
        name="modnorm",
    )(*args)


def _mxu_operand(w):
    return w if w.dtype == BF16 else w.astype(BF16)


def _mm_kernel(*refs, a_of_w, kind, n_lat, tm, nt=False, valid_cols=None):
    n_a = 1 + max(ai for ai, _ in a_of_w)
    n_w = len(a_of_w)
    a_refs, w_refs, rest = refs[:n_a], refs[n_a:n_a + n_w], refs[n_a + n_w:]
    col_axis, k_axis = (0, 1) if nt else (1, 0)

    def lhs(i):
        ai, off = a_of_w[i]
        return a_refs[ai][:, off:off + w_refs[i].shape[k_axis]]

    def rhs(i):
        w = w_refs[i][...]
        if valid_cols is not None:
            w = jnp.where(lax.broadcasted_iota(jnp.int32, w.shape, col_axis) < valid_cols, w, 0.0)
        return _mxu_operand(w)

    def mm(i):
        return lax.dot_general(lhs(i), rhs(i), (((1,), (k_axis,)), ((), ())), preferred_element_type=F32)

    if kind == "swiglu":
        g, u = mm(0), mm(1)
        (o_ref,) = rest
        o_ref[...] = (_silu(g) * u).astype(o_ref.dtype)
        return
    acc = mm(0)
    for i in range(1, n_w):
        acc = acc + mm(i)
    if kind == "plain":
        (o_ref,) = rest
        o_ref[...] = acc.astype(o_ref.dtype)
    else:
        x_ref, g_ref, o_ref = rest
        rows = pl.program_id(0) * tm + lax.broadcasted_iota(jnp.int32, (tm, 1), 0)
        gate = jnp.where(rows >= n_lat, g_ref[1:2, :], g_ref[0:1, :])
        o_ref[...] = x_ref[...] + gate * acc


def _matmul(a_list, w_list, *, rows, n, tm, tn, col0=0, kind="plain", out_dtype=F32, resid=None, n_lat=0,
            nt=False, valid_cols=None, name="matmul"):
    assert col0 % tn == 0
    cb = col0 // tn
    in_specs = [pl.BlockSpec((tm, a.shape[1]), lambda i, j: (i, 0)) for a in a_list]
    args = list(a_list)
    vmem = sum(2 * tm * a.shape[1] * 2 for a in a_list) + 6 * tm * tn * 4
    for w3d, lead, _, _, k_rows, k_blk in w_list:
        if nt:
            in_specs.append(pl.BlockSpec((None, tn, k_rows), lambda i, j, lead=lead, k_blk=k_blk: (lead, cb + j, k_blk)))
        else:
            in_specs.append(pl.BlockSpec((None, k_rows, tn), lambda i, j, lead=lead, k_blk=k_blk: (lead, k_blk, cb + j)))
        args.append(w3d)
        vmem += k_rows * tn * (2 * w3d.dtype.itemsize + 2)
    if kind == "resid":
        x, mods, gate_idx = resid
        d = x.shape[1]
        in_specs += [pl.BlockSpec((tm, tn), lambda i, j: (i, j)),
                     pl.BlockSpec((8, tn), lambda i, j: (0, gate_idx * (d // tn) + j))]
        args += [x, mods]
    a_of_w = tuple((a_idx, a_off) for _, _, a_idx, a_off, _, _ in w_list)
    return pl.pallas_call(
        functools.partial(_mm_kernel, a_of_w=a_of_w, kind=kind, n_lat=n_lat, tm=tm, nt=nt, valid_cols=valid_cols),
        grid=(rows // tm, n // tn),
        in_specs=in_specs,
        out_specs=pl.BlockSpec((tm, tn), lambda i, j: (i, j)),
        out_shape=jax.ShapeDtypeStruct((rows, n), out_dtype),
        compiler_params=_cparams(("parallel", "arbitrary"), vmem),
        name=name,
    )(*args)


def _qkprep_kernel(x_ref, g_ref, c_ref, s_ref, o_ref, *, n_tiles, seg, scale, split):
    lane = lax.broadcasted_iota(jnp.int32, c_ref.shape, 1)
    gain, cos, sin = g_ref[...], c_ref[...], s_ref[...]
    half = seg // 2
    ow = 2 * LANES if split else LANES
    for j in range(n_tiles):
        x = x_ref[:, j * LANES:(j + 1) * LANES].astype(F32)
        x2 = x * x
        if seg == LANES:
            ms = jnp.sum(x2, axis=-1, keepdims=True) * (1.0 / seg)
        else:
            lo = lane < seg
            s_lo = jnp.sum(jnp.where(lo, x2, 0.0), axis=-1, keepdims=True)
            s_hi = jnp.sum(jnp.where(lo, 0.0, x2), axis=-1, keepdims=True)
            ms = jnp.where(lo, s_lo, s_hi) * (1.0 / seg)
        y = x * lax.rsqrt(ms + NORM_EPS) * gain
        if seg == LANES:
            partner = pltpu.roll(y, half, 1)
        else:
            partner = jnp.where((lane % seg) < half, pltpu.roll(y, LANES - half, 1), pltpu.roll(y, half, 1))
        y = (y * cos + partner * sin) * scale
        if split:
            o_ref[:, j * ow:j * ow + LANES] = jnp.where(lane < seg, y, 0.0).astype(o_ref.dtype)
            o_ref[:, j * ow + LANES:(j + 1) * ow] = jnp.where(lane < seg, 0.0, y).astype(o_ref.dtype)
        else:
            o_ref[:, j * ow:(j + 1) * ow] = y.astype(o_ref.dtype)


def _qkprep(proj, col0, n_tiles, gain, rope_c, rope_s, *, seg, scale, split):
    t = proj.shape[0]
    tm = ROW_TILE
    width = n_tiles * LANES
    assert col0 % width == 0
    cb = col0 // width
    g = jnp.tile(gain.astype(F32), LANES // seg).reshape(1, LANES)
    ow = (2 if split else 1) * width
    return pl.pallas_call(
        functools.partial(_qkprep_kernel, n_tiles=n_tiles, seg=seg, scale=scale, split=split),
        grid=(t // tm,),
        in_specs=[pl.BlockSpec((tm, width), lambda i: (i, cb)),
                  pl.BlockSpec((1, LANES), lambda i: (0, 0)),
                  pl.BlockSpec((tm, LANES), lambda i: (i, 0)),
                  pl.BlockSpec((tm, LANES), lambda i: (i, 0))],
        out_specs=pl.BlockSpec((tm, ow), lambda i: (i, 0)),
        out_shape=jax.ShapeDtypeStruct((t, ow), BF16),
        compiler_params=_cparams(("parallel",), 0),
        name="qk_prep",
    )(proj, g, rope_c, rope_s)


def _attn_kernel(q_ref, k_ref, vt_ref, *rest, n_q_heads, tq, tk, lam_init):
    n_keys = k_ref.shape[0]
    qs = [q_ref[:, g * LANES:(g + 1) * LANES] for g in range(n_q_heads)]

    def scores(kc):
        return tuple(lax.dot_general(kc, q, (((1,), (1,)), ((), ())), preferred_element_type=F32) for q in qs)

    def update(ss, vtc, carry):
        new = []
        for s, (m, acc) in zip(ss, carry):
            m_new = jnp.maximum(m, jnp.max(s, axis=0, keepdims=True))
            p = jnp.exp2(s - m_new).astype(BF16)
            new.append((m_new, jnp.exp2(m - m_new) * acc + jnp.dot(vtc, p, preferred_element_type=F32)))
        return tuple(new)

    carry = tuple((jnp.full((1, tq), -jnp.inf, F32), jnp.zeros((vt_ref.shape[0], tq), F32))
                  for _ in range(n_q_heads))
    ss = scores(k_ref[0:tk, :])
    for c in range(n_keys // tk):
        ss_next = scores(k_ref[(c + 1) * tk:(c + 2) * tk, :]) if (c + 1) * tk < n_keys else None
        carry = update(ss, vt_ref[:, c * tk:(c + 1) * tk], carry)
        ss = ss_next
    outs_t = [acc[:LANES] / acc[LANES:LANES + 1] for _, acc in carry]
    if lam_init is None:
        (o_ref,) = rest
        for g in range(n_q_heads):
            o_ref[:, g * LANES:(g + 1) * LANES] = outs_t[g].T.astype(o_ref.dtype)
    else:
        lam_ref, g_ref, o_ref = rest
        lp = lam_ref[...]
        lam = (jnp.exp(jnp.sum(lp[0:1, :] * lp[1:2, :], axis=-1, keepdims=True))
               - jnp.exp(jnp.sum(lp[2:3, :] * lp[3:4, :], axis=-1, keepdims=True)) + lam_init)
        o = (outs_t[0] - lam * outs_t[1]).T
        ms = jnp.mean(o * o, axis=-1, keepdims=True)
        o_ref[...] = (o * lax.rsqrt(ms + NORM_EPS) * g_ref[...] * (1.0 - lam_init)).astype(o_ref.dtype)


def _attention(q, k, v, *, n_kv, n_q_heads, n_lat, n_ctx, with_ctx_queries, tq, diff=None):
    t = k.shape[0]
    qw = n_q_heads * LANES
    ow = LANES if diff is not None else qw
    hv = LANES + ONES_ROWS
    vt = jnp.concatenate([v.T.reshape(n_kv, LANES, t), jnp.ones((n_kv, ONES_ROWS, t), v.dtype)], axis=1)
    vt = vt.reshape(n_kv * hv, t)
    extra_specs, extra_args, lam_init = [], [], None
    if diff is not None:
        lam_p, gain, lam_init = diff
        extra_specs = [pl.BlockSpec((4, DIFF_QK_DIM), lambda h, i: (0, 0)), pl.BlockSpec((1, LANES), lambda h, i: (0, 0))]
        extra_args = [lam_p.astype(F32), gain.astype(F32).reshape(1, LANES)]

    def call(q_rows, q_row0, n_keys, key0, tq, tk):
        qb0, kb0 = q_row0 // tq, key0 // n_keys
        vmem = 2 * 2 * n_keys * LANES * 2 + 4 * tq * qw * 4 + 8 * n_q_heads * tq * tk * 4
        return pl.pallas_call(
            functools.partial(_attn_kernel, n_q_heads=n_q_heads, tq=tq, tk=tk, lam_init=lam_init),
            grid=(n_kv, q_rows // tq),
            in_specs=[pl.BlockSpec((tq, qw), lambda h, i: (qb0 + i, h)),
                      pl.BlockSpec((n_keys, LANES), lambda h, i: (kb0, h)),
                      pl.BlockSpec((hv, n_keys), lambda h, i: (h, kb0))] + extra_specs,
            out_specs=pl.BlockSpec((tq, ow), lambda h, i: (i, h)),
            out_shape=jax.ShapeDtypeStruct((q_rows, n_kv * ow), BF16),
            compiler_params=_cparams(("parallel", "arbitrary"), vmem),
            name="attention",
        )(q, k, vt, *extra_args)

    tq = math.gcd(tq, n_lat)
    out = call(n_lat, 0, t, 0, tq, next(c for c in (768, 512, 256) if t % c == 0))
    if with_ctx_queries:
        assert n_lat % n_ctx == 0
        out = jnp.concatenate([out, call(n_ctx, n_lat, n_ctx, n_lat, n_ctx, n_ctx)], axis=0)
    return out


def _conv_kernel(x_ref, xp_ref, xn_ref, w_ref, b_ref, o_ref, *, n_lat, t_all, tm, k_col_tile):
    i, j = pl.program_id(0), pl.program_id(1)
    x = x_ref[...].astype(F32)
    row = lax.broadcasted_iota(jnp.int32, (tm, 1), 0)
    grow = i * tm + row
    prev = jnp.where(row == 0, xp_ref[15:16, :].astype(F32), pltpu.roll(x, 1, 0))
    nxt = jnp.where(row == tm - 1, xn_ref[0:1, :].astype(F32), pltpu.roll(x, tm - 1, 0))
    prev = jnp.where((grow == 0) | (grow == n_lat), 0.0, prev)
    nxt = jnp.where((grow == n_lat - 1) | (grow == t_all - 1), 0.0, nxt)
    y = b_ref[...] + prev * w_ref[0:1, :] + x * w_ref[1:2, :] + nxt * w_ref[2:3, :]
    y = _silu(y) * jnp.where(j >= k_col_tile, MLSTM_DIM ** -0.5, 1.0)
    o_ref[...] = y.astype(o_ref.dtype)


def _mlstm_conv(proj, conv_w, conv_b, n_lat):
    t = proj.shape[0]
    tm, tc, hb = (CONV_ROW_TILE if t % CONV_ROW_TILE == 0 else ROW_TILE), 512, 16
    width = 2 * MLSTM_HEADS * MLSTM_DIM
    cb = _MQ // tc
    nrb = t // hb
    return pl.pallas_call(
        functools.partial(_conv_kernel, n_lat=n_lat, t_all=t, tm=tm, k_col_tile=(width // 2) // tc),
        grid=(t // tm, width // tc),
        in_specs=[pl.BlockSpec((tm, tc), lambda i, j: (i, cb + j)),
                  pl.BlockSpec((hb, tc), lambda i, j: (jnp.maximum(i * (tm // hb) - 1, 0), cb + j)),
                  pl.BlockSpec((hb, tc), lambda i, j: (jnp.minimum((i + 1) * (tm // hb), nrb - 1), cb + j)),
                  pl.BlockSpec((3, tc), lambda i, j: (0, j)),
                  pl.BlockSpec((1, tc), lambda i, j: (0, j))],
        out_specs=pl.BlockSpec((tm, tc), lambda i, j: (i, j)),
        out_shape=jax.ShapeDtypeStruct((t, width), BF16),
        compiler_params=_cparams(("parallel", "arbitrary"), 0),
        name="mlstm_conv",
    )(proj, proj, proj, conv_w.astype(F32), conv_b.astype(F32).reshape(1, width))


def _mlstm_kernel(q_ref, k_ref, vt_ref, gi_ref, gf_ref, bi_ref, bf_ref, mask_ref, ht_ref, ct_sc, n_sc, m_sc, *, n_heads):
    @pl.when(pl.program_id(0) == 0)
    def _():
        ct_sc[...] = jnp.zeros_like(ct_sc)
        n_sc[...] = jnp.zeros_like(n_sc)
        m_sc[...] = jnp.zeros_like(m_sc)

    vis_t = mask_ref[...]
    li_all = gi_ref[...] + bi_ref[...]
    lf_all = _log_sigmoid(gf_ref[...] + bf_ref[...])
    br_all = jnp.dot(lf_all, vis_t, preferred_element_type=F32, precision=HIGHEST)
    bend_all = jnp.sum(lf_all, axis=1, keepdims=True)
    c_cols = (br_all - li_all).T
    d = MLSTM_DIM
    nt = (((1,), (1,)), ((), ()))
    for h in range(n_heads):
        lir, br = li_all[h:h + 1, :], br_all[h:h + 1, :]
        b_end = bend_all[h:h + 1, :]
        m_prev = m_sc[h:h + 1, 0:1]
        lw = b_end - br + lir
        m_new = jnp.maximum(b_end + m_prev, jnp.max(lw, axis=1, keepdims=True))
        w_end = jnp.exp(lw - m_new)
        decay = jnp.exp(b_end + m_prev - m_new)
        log_d = jnp.where(vis_t > 0.0, br - c_cols[:, h:h + 1], -jnp.inf)
        m_t = jnp.maximum(br + m_prev, jnp.max(log_d, axis=0, keepdims=True))
        dmat = jnp.exp(log_d - m_t)
        inter = jnp.exp(br + m_prev - m_t)
        qh = q_ref[:, h * d:(h + 1) * d]
        kh = k_ref[:, h * d:(h + 1) * d]
        vth = vt_ref[h * d:(h + 1) * d, :]
        ct_prev = ct_sc[h]
        n_prev = n_sc[h:h + 1, :]
        sc = lax.dot_general(kh, qh, nt, preferred_element_type=F32) * dmat
        num = (jnp.dot(vth, sc.astype(BF16), preferred_element_type=F32)
               + inter * lax.dot_general(ct_prev.astype(BF16), qh, nt, preferred_element_type=F32))
        n8 = jnp.broadcast_to(n_prev, (8, d)).astype(BF16)
        den = (jnp.sum(sc, axis=0, keepdims=True)
               + inter * lax.dot_general(n8, qh, nt, preferred_element_type=F32)[0:1, :])
        ht_ref[h * d:(h + 1) * d, :] = num / jnp.maximum(jnp.abs(den), jnp.exp(-m_t))
        vtw = (vth.astype(F32) * w_end).astype(BF16)
        ct_sc[h] = decay * ct_prev + jnp.dot(vtw, kh, preferred_element_type=F32)
        w8 = jnp.broadcast_to(w_end, (8, w_end.shape[1])).astype(BF16)
        n_sc[h:h + 1, :] = decay * n_prev + jnp.dot(w8, kh, preferred_element_type=F32)[0:1, :]
        m_sc[h:h + 1, :] = jnp.broadcast_to(m_new, (1, LANES))


def _mlstm_direction(qk, vt, gi, gf, bi, bf, vis_t, *, n_lat, reverse):
    t = qk.shape[0]
    lc = MLSTM_CHUNK
    n_chunks = t // lc
    ctx_blk = n_lat // lc
    hd = MLSTM_HEADS * MLSTM_DIM

    def blk(c):
        lat = (n_chunks - 1 - c) if reverse else (c - 1)
        return jnp.where(c == 0, ctx_blk, lat)

    return pl.pallas_call(
        functools.partial(_mlstm_kernel, n_heads=MLSTM_HEADS),
        grid=(n_chunks,),
        in_specs=[pl.BlockSpec((lc, hd), lambda c: (blk(c), 0)),
                  pl.BlockSpec((lc, hd), lambda c: (blk(c), 1)),
                  pl.BlockSpec((hd, lc), lambda c: (0, blk(c))),
                  pl.BlockSpec((16, lc), lambda c: (0, blk(c))),
                  pl.BlockSpec((16, lc), lambda c: (0, blk(c))),
                  pl.BlockSpec((16, 1), lambda c: (0, 0)),
                  pl.BlockSpec((16, 1), lambda c: (0, 0)),
                  pl.BlockSpec((lc, lc), lambda c: (0, 0))],
        out_specs=pl.BlockSpec((hd, lc), lambda c: (0, blk(c))),
        out_shape=jax.ShapeDtypeStruct((hd, t), F32),
        scratch_shapes=[pltpu.VMEM((MLSTM_HEADS, MLSTM_DIM, MLSTM_DIM), F32),
                        pltpu.VMEM((16, LANES), F32),
                        pltpu.VMEM((16, LANES), F32)],
        compiler_params=_cparams(("arbitrary",), 16 << 20),
        name="mlstm_bwd" if reverse else "mlstm_fwd",
    )(qk, qk, vt, gi, gf, bi, bf, vis_t)


def _mlstm_finish_kernel(hf_ref, hb_ref, o0_ref, o1_ref, o2_ref, g_ref, out_ref, *, n_heads):
    d = MLSTM_DIM
    o_refs = (o0_ref, o1_ref, o2_ref)
    per_blk = o0_ref.shape[1] // d
    for h in range(n_heads):
        sl = slice(h * d, (h + 1) * d)
        s = hf_ref[sl, :] + hb_ref[sl, :]
        ms = jnp.mean(s * s, axis=0, keepdims=True)
        y = (s * lax.rsqrt(ms + NORM_EPS)).T * g_ref[:, sl]
        o = o_refs[h // per_blk][:, (h % per_blk) * d:(h % per_blk + 1) * d]
        out_ref[:, sl] = (jax.nn.sigmoid(o.astype(F32)) * y).astype(out_ref.dtype)


def _mlstm_finish(hf_t, hb_t, proj, out_gain, rows):
    hd = MLSTM_HEADS * MLSTM_DIM
    tm = ROW_TILE
    ow = hd // 3
    assert _MO % ow == 0
    return pl.pallas_call(
        functools.partial(_mlstm_finish_kernel, n_heads=MLSTM_HEADS),
        grid=(rows // tm,),
        in_specs=[pl.BlockSpec((hd, tm), lambda i: (0, i)),
                  pl.BlockSpec((hd, tm), lambda i: (0, i)),
                  pl.BlockSpec((tm, ow), lambda i: (i, _MO // ow)),
                  pl.BlockSpec((tm, ow), lambda i: (i, _MO // ow + 1)),
                  pl.BlockSpec((tm, ow), lambda i: (i, _MO // ow + 2)),
                  pl.BlockSpec((1, hd), lambda i: (0, 0))],
        out_specs=pl.BlockSpec((tm, hd), lambda i: (i, 0)),
        out_shape=jax.ShapeDtypeStruct((rows, hd), BF16),
        compiler_params=_cparams(("parallel",), 0),
        name="mlstm_finish",
    )(hf_t, hb_t, proj, proj, proj, out_gain.astype(F32).reshape(1, hd))


def _moe_up_kernel(be_ref, nv_ref, st_ref, tok_ref, wg_ref, wu_ref, o_ref, xbuf, xb, sem, *, nj):
    b, j = pl.program_id(0), pl.program_id(1)
    nb = pl.num_programs(0)
    n_valid = nv_ref[b]
    per_step = MOE_BLK // nj

    def copy(blk, r):
        src = tok_ref.at[pl.ds(st_ref[blk * MOE_BLK + r], 1), :]
        return pltpu.make_async_copy(src, xbuf.at[pl.ds(r, 1), :], sem)

    @pl.when((b == 0) & (j == 0) & (n_valid > 0))
    def _():
        def body(r, carry):
            copy(0, r).start()
            return carry
        lax.fori_loop(0, MOE_BLK, body, 0)

    @pl.when((j == 0) & (n_valid > 0))
    def _():
        pltpu.make_async_copy(tok_ref.at[pl.ds(0, MOE_BLK), :], xbuf, sem).wait()
        xb[...] = xbuf[...].astype(BF16)

    @pl.when((b + 1 < nb) & (nv_ref[jnp.minimum(b + 1, nb - 1)] > 0))
    def _():
        for r in range(per_step):
            copy(b + 1, j * per_step + r).start()

    def swiglu(sl):
        a = xb[sl, :]
        g = jnp.dot(a, _mxu_operand(wg_ref[...]), preferred_element_type=F32)
        u = jnp.dot(a, _mxu_operand(wu_ref[...]), preferred_element_type=F32)
        o_ref[sl, :] = (_silu(g) * u).astype(o_ref.dtype)

    @pl.when(n_valid == MOE_BLK)
    def _():
        swiglu(slice(None))

    @pl.when((n_valid > 0) & (n_valid < MOE_BLK))
    def _():
        for s in range(MOE_BLK // MOE_SLAB):
            sl = slice(s * MOE_SLAB, (s + 1) * MOE_SLAB)

            @pl.when(n_valid > s * MOE_SLAB)
            def _():
                swiglu(sl)

            @pl.when(n_valid <= s * MOE_SLAB)
            def _():
                o_ref[sl, :] = jnp.zeros((MOE_SLAB, o_ref.shape[1]), o_ref.dtype)

    @pl.when(n_valid == 0)
    def _():
        o_ref[...] = jnp.zeros_like(o_ref)


def _moe_up(tok, w_gate, w_up, lead, block_e, n_valid, slot_tok, *, tn):
    n_slots = slot_tok.shape[0]
    k = tok.shape[1]
    n = w_gate.shape[-1]
    nb, nj = n_slots // MOE_BLK, n // tn
    assert MOE_BLK % nj == 0

    def w_map(b, j, be, nv, st):
        return (lead, be[b], 0, jnp.where(nv[b] > 0, j, nj - 1))

    return pl.pallas_call(
        functools.partial(_moe_up_kernel, nj=nj),
        grid_spec=pltpu.PrefetchScalarGridSpec(
            num_scalar_prefetch=3,
            grid=(nb, nj),
            in_specs=[pl.BlockSpec(memory_space=pl.ANY),
                      pl.BlockSpec((None, None, k, tn), w_map),
                      pl.BlockSpec((None, None, k, tn), w_map)],
            out_specs=pl.BlockSpec((MOE_BLK, tn), lambda b, j, be, nv, st: (b, j)),
            scratch_shapes=[pltpu.VMEM((MOE_BLK, k), F32),
                            pltpu.VMEM((MOE_BLK, k), BF16),
                            pltpu.SemaphoreType.DMA(())]),
        out_shape=jax.ShapeDtypeStruct((n_slots, n), BF16),
        compiler_params=_cparams(("arbitrary", "arbitrary"),
                                 MOE_BLK * k * 6 + 2 * k * tn * (2 * w_gate.dtype.itemsize + 2)
                                 + 4 * MOE_BLK * tn * 4),
        name="moe_up",
    )(block_e, n_valid, slot_tok, tok, w_gate, w_up)


def _moe_down_kernel(be_ref, nv_ref, nu_ref, a_ref, w_ref, o_ref):
    n_valid = nv_ref[pl.program_id(0)]

    def down(sl):
        o_ref[sl, :] = jnp.dot(a_ref[sl, :], _mxu_operand(w_ref[...]), preferred_element_type=F32)

    @pl.when(n_valid == MOE_BLK)
    def _():
        down(slice(None))

    @pl.when((n_valid > 0) & (n_valid < MOE_BLK))
    def _():
        for s in range(MOE_BLK // MOE_SLAB):
            sl = slice(s * MOE_SLAB, (s + 1) * MOE_SLAB)

            @pl.when(n_valid > s * MOE_SLAB)
            def _():
                down(sl)

            @pl.when(n_valid <= s * MOE_SLAB)
            def _():
                o_ref[sl, :] = jnp.zeros((MOE_SLAB, o_ref.shape[1]), o_ref.dtype)

    @pl.when(n_valid == 0)
    def _():
        o_ref[...] = jnp.zeros_like(o_ref)


def _moe_down(a, w, lead, block_e, n_valid, n_used, *, tn):
    n_slots, k = a.shape
    n = w.shape[-1]
    nb, nj = n_slots // MOE_BLK, n // tn

    def a_map(b, j, be, nv, nu):
        return (jnp.minimum(b, nu[0] - 1), 0)

    def w_map(b, j, be, nv, nu):
        return (lead, be[b], 0, jnp.where(nv[b] > 0, j, nj - 1))

    return pl.pallas_call(
        _moe_down_kernel,
        grid_spec=pltpu.PrefetchScalarGridSpec(
            num_scalar_prefetch=3,
            grid=(nb, nj),
            in_specs=[pl.BlockSpec((MOE_BLK, k), a_map), pl.BlockSpec((None, None, k, tn), w_map)],
            out_specs=pl.BlockSpec((MOE_BLK, tn), lambda b, j, be, nv, nu: (b, j))),
        out_shape=jax.ShapeDtypeStruct((n_slots, n), F32),
        compiler_params=_cparams(("arbitrary", "arbitrary"),
                                 2 * MOE_BLK * k * 2 + k * tn * (2 * w.dtype.itemsize + 2) + 4 * MOE_BLK * tn * 4),
        name="moe_down",
    )(block_e, n_valid, n_used, a, w)


def _combine_kernel(d0_ref, d1_ref, y_ref, x_ref, w0_ref, w1_ref, g_ref, o_ref, buf, sem, *, rows):
    i = pl.program_id(0)
    n = pl.num_programs(0)
    slot = i % 2

    def gather(step, slot_):
        base = step * rows

        def body(r, carry):
            for which, d_ref in enumerate((d0_ref, d1_ref)):
                pltpu.make_async_copy(y_ref.at[pl.ds(d_ref[base + r], 1), :],
                                      buf.at[slot_, which, pl.ds(r, 1), :], sem.at[slot_]).start()
            return carry

        lax.fori_loop(0, rows, body, 0, unroll=8)

    @pl.when(i == 0)
    def _():
        gather(0, 0)

    @pl.when(i + 1 < n)
    def _():
        gather(i + 1, 1 - slot)

    for which in range(2):
        pltpu.make_async_copy(y_ref.at[pl.ds(0, rows), :], buf.at[slot, which], sem.at[slot]).wait()
    y = w0_ref[...] * buf[slot, 0] + w1_ref[...] * buf[slot, 1]
    o_ref[...] = x_ref[...] + g_ref[0:1, :] * y


def _moe_combine(y_slots, x, d0, d1, w0, w1, mods, gate_idx):
    n_tok, d = x.shape
    rows = GATHER_ROWS // 2
    return pl.pallas_call(
        functools.partial(_combine_kernel, rows=rows),
        grid_spec=pltpu.PrefetchScalarGridSpec(
            num_scalar_prefetch=2,
            grid=(n_tok // rows,),
            in_specs=[pl.BlockSpec(memory_space=pl.ANY),
                      pl.BlockSpec((rows, d), lambda i, a, b: (i, 0)),
                      pl.BlockSpec((rows, 1), lambda i, a, b: (i, 0)),
                      pl.BlockSpec((rows, 1), lambda i, a, b: (i, 0)),
                      pl.BlockSpec((8, d), lambda i, a, b: (0, gate_idx))],
            out_specs=pl.BlockSpec((rows, d), lambda i, a, b: (i, 0)),
            scratch_shapes=[pltpu.VMEM((2, 2, rows, d), F32), pltpu.SemaphoreType.DMA((2,))]),
        out_shape=jax.ShapeDtypeStruct((n_tok, d), F32),
        compiler_params=_cparams(("arbitrary",), 12 * rows * d * 4),
        name="moe_combine",
    )(d0, d1, y_slots, x, w0, w1, mods)


def _moe_routing(logits, n_tok):
    top_logit, top_idx = lax.top_k(logits, TOP_K)
    top_w = jax.nn.softmax(top_logit, axis=-1)
    flat_e = top_idx.reshape(-1).astype(jnp.int32)
    n_assign = n_tok * TOP_K
    onehot = (flat_e[:, None] == jnp.arange(N_EXPERTS, dtype=jnp.int32)[None, :]).astype(jnp.int32)
    csum = jnp.cumsum(onehot, axis=0)
    rank = jnp.take_along_axis(csum, flat_e[:, None], axis=1)[:, 0] - 1
    counts = csum[-1]
    padded = (counts + MOE_BLK - 1) // MOE_BLK * MOE_BLK
    pad_ends = jnp.cumsum(padded)
    pad_starts = pad_ends - padded
    dest = (pad_starts[flat_e] + rank).astype(jnp.int32)
    n_blocks = -(-(n_assign + N_EXPERTS * (MOE_BLK - 1)) // MOE_BLK)
    n_slots = n_blocks * MOE_BLK
    slot_tok = jnp.zeros((n_slots,), jnp.int32).at[dest].set(jnp.arange(n_assign, dtype=jnp.int32) // TOP_K)
    blk_start = jnp.arange(n_blocks, dtype=jnp.int32) * MOE_BLK
    block_e = jnp.minimum(jnp.searchsorted(pad_ends, blk_start, side="right"), N_EXPERTS - 1).astype(jnp.int32)
    slab_ends = pad_starts + (counts + MOE_SLAB - 1) // MOE_SLAB * MOE_SLAB
    n_valid = jnp.clip(slab_ends[block_e] - blk_start, 0, MOE_BLK).astype(jnp.int32)
    n_used = (pad_ends[-1] // MOE_BLK).astype(jnp.int32).reshape(1)
    dest2 = dest.reshape(n_tok, TOP_K)
    return dest2[:, 0], dest2[:, 1], top_w[:, 0:1], top_w[:, 1:2], slot_tok, block_e, n_valid, n_used


def _rope_tables(n_lat, n_ctx, dim):
    rows = n_lat // GRID_W
    row = np.repeat(np.arange(rows, dtype=np.float32), GRID_W)
    col = np.tile(np.arange(GRID_W, dtype=np.float32), rows)
    quarter = dim // 4
    inv = (np.float32(ROPE_THETA) ** (-np.arange(quarter, dtype=np.float32) / np.float32(quarter))).astype(np.float32)
    ang = np.concatenate([row[:, None] * inv, col[:, None] * inv], axis=-1).astype(np.float32)
    cos, sin = np.cos(ang), np.sin(ang)
    reps = LANES // dim
    c = np.tile(np.concatenate([cos, cos], axis=-1), (1, reps))
    s = np.tile(np.concatenate([-sin, sin], axis=-1), (1, reps))
    c = np.concatenate([c, np.ones((n_ctx, LANES), np.float32)], axis=0)
    s = np.concatenate([s, np.zeros((n_ctx, LANES), np.float32)], axis=0)
    return jnp.asarray(c, F32), jnp.asarray(s, F32)


def kernel(x, c, ctx, c_ctx, w_mod, b_mod, attn_norm, ffn_norm, w_in, w_out, diff_q_norm, diff_k_norm, diff_lambda, diff_out_norm, gqa_q_norm, gqa_k_norm, mlstm_conv_w, mlstm_conv_b, mlstm_gate_b, mlstm_out_norm, ffn_w_gate, ffn_w_up, ffn_w_down, router_w, router_b, exp_w_gate, exp_w_up, exp_w_down):
    batch, n_lat, d = x.shape
    n_ctx = ctx.shape[1]
    depth = w_in.shape[0]
    assert batch == 1 and n_ctx == ROW_TILE and n_lat % MLSTM_CHUNK == 0 and n_lat % GRID_W == 0
    t_all = n_lat + n_ctx
    hd = MLSTM_HEADS * MLSTM_DIM

    cc = jnp.zeros((8, d), F32).at[0].set(c[0]).at[1].set(c_ctx)
    mods_all = _mods(cc, w_mod, b_mod)
    rope_diff = _rope_tables(n_lat, n_ctx, DIFF_QK_DIM)
    rope_gqa = _rope_tables(n_lat, n_ctx, GQA_DIM)
    pos = jnp.arange(MLSTM_CHUNK)
    vis_fwd = (pos[None, :] <= pos[:, None]).astype(F32)
    vis_bwd = (pos[None, :] >= pos[:, None]).astype(F32)

    w_in_t = jnp.swapaxes(w_in, 1, 2)
    xa = jnp.concatenate([x[0], ctx[0]], axis=0)
    for l in range(depth):
        need_ctx = l < depth - 1
        rows = t_all if need_ctx else n_lat
        lam_init = 0.8 - 0.6 * math.exp(-0.3 * l)
        mods = mods_all[l]

        h = _modnorm(xa, attn_norm[l], mods, 0, n_lat)
        proj = _matmul([h], [(w_in_t, l, 0, 0, d, 0)], rows=t_all, n=_MAIN_WIDTH, tm=_row_tile(t_all), tn=512,
                       nt=True, out_dtype=BF16, name="in_proj")
        gates = _matmul([h], [(w_in_t, l, 0, 0, d, 0)], rows=t_all, n=LANES, col0=_MG, tm=_row_tile(t_all), tn=LANES,
                        nt=True, valid_cols=w_in.shape[-1] - _MG, name="gate_proj")

        dq = _qkprep(proj, _DQ, DIFF_HEADS, diff_q_norm[l], *rope_diff, seg=DIFF_QK_DIM,
                     scale=DIFF_QK_DIM ** -0.5 * LOG2E, split=True)
        dk = _qkprep(proj, _DK, DIFF_HEADS, diff_k_norm[l], *rope_diff, seg=DIFF_QK_DIM, scale=1.0, split=False)
        mix_d = _attention(dq, dk, proj[:, _DV:_GQ], n_kv=DIFF_HEADS, n_q_heads=2,
                           n_lat=n_lat, n_ctx=n_ctx, with_ctx_queries=need_ctx, tq=DIFF_Q_TILE,
                           diff=(diff_lambda[l], diff_out_norm[l], lam_init))

        gq = _qkprep(proj, _GQ, GQA_HEADS, gqa_q_norm[l], *rope_gqa, seg=GQA_DIM, scale=GQA_DIM ** -0.5 * LOG2E,
                     split=False)
        gk = _qkprep(proj, _GK, GQA_KV_HEADS, gqa_k_norm[l], *rope_gqa, seg=GQA_DIM, scale=1.0, split=False)
        mix_g = _attention(gq, gk, proj[:, _GV:_MQ], n_kv=GQA_KV_HEADS,
                           n_q_heads=GQA_HEADS // GQA_KV_HEADS, n_lat=n_lat, n_ctx=n_ctx, with_ctx_queries=need_ctx,
                           tq=GQA_Q_TILE)

        qk = _mlstm_conv(proj, mlstm_conv_w[l], mlstm_conv_b[l], n_lat)
        vt_m = proj[:, _MV:_MO].T
        g_t = gates[:, :4 * MLSTM_HEADS].T.reshape(4, MLSTM_HEADS, t_all)
        g_t = jnp.pad(g_t, ((0, 0), (0, 16 - MLSTM_HEADS), (0, 0)))
        gb = jnp.pad(mlstm_gate_b[l].astype(F32), ((0, 0), (0, 16 - MLSTM_HEADS)))[:, :, None]
        h_f = _mlstm_direction(qk, vt_m, g_t[0], g_t[1], gb[0], gb[1], vis_fwd.T, n_lat=n_lat, reverse=False)
        h_b = _mlstm_direction(qk, vt_m, g_t[2], g_t[3], gb[2], gb[3], vis_bwd.T, n_lat=n_lat, reverse=True)
        mix_m = _mlstm_finish(h_f, h_b, proj, mlstm_out_norm[l], rows)

        kb = 512
        w_blocks = [(w_out, l, ai, off, kb, (row0 + off) // kb)
                    for ai, (row0, width) in enumerate(((0, 1024), (1024, 1536), (2560, 1536)))
                    for off in range(0, width, kb)]
        x1 = _matmul([mix_d, mix_g, mix_m], w_blocks, rows=rows, n=d, tm=_row_tile(rows), tn=256, kind="resid",
                     resid=(xa, mods, 2), n_lat=n_lat, name="out_proj")

        if l % 2 == 0:
            tok = _modnorm(x1, ffn_norm[l], mods, 3, n_lat)
            d_ff = ffn_w_gate.shape[-1]
            hid = _matmul([tok], [(ffn_w_gate, l // 2, 0, 0, d, 0), (ffn_w_up, l // 2, 0, 0, d, 0)], rows=rows,
                          n=d_ff, tm=_row_tile(rows), tn=256, kind="swiglu", out_dtype=BF16, name="ffn_up")
            w_down = ffn_w_down[l // 2].astype(BF16)[None]
            xa = _matmul([hid], [(w_down, 0, 0, 0, d_ff, 0)], rows=rows, n=d, tm=_row_tile(rows, (768, 512, 256)),
                         tn=256, kind="resid", resid=(x1, mods, 5), n_lat=n_lat, name="ffn_down")
        else:
            assert not need_ctx
            rw = jnp.pad(router_w[l // 2].astype(F32), ((0, 0), (0, LANES - N_EXPERTS)))
            rb = jnp.pad(router_b[l // 2].astype(F32), (0, LANES - N_EXPERTS)).reshape(1, LANES)
            tok, logits = _modnorm(x1, ffn_norm[l], mods, 3, n_lat, router=(rw, rb), out_dtype=F32)
            d0, d1, w0, w1, slot_tok, block_e, n_valid, n_used = _moe_routing(logits[:, :N_EXPERTS], rows)
            hid = _moe_up(tok, exp_w_gate, exp_w_up, l // 2, block_e, n_valid, slot_tok, tn=256)
            ys = _moe_down(hid, exp_w_down, l // 2, block_e, n_valid, n_used, tn=512)
            xa = _moe_combine(ys, x1, d0, d1, w0, w1, mods, 5)
    return xa[:n_lat].reshape(batch, n_lat, d)
```

```python
import functools
import math

import jax
import jax.numpy as jnp
import numpy as np
from jax import lax
from jax.experimental import pallas as pl
from jax.experimental.pallas import tpu as pltpu

F32 = jnp.float32
BF16 = jnp.bfloat16
HIGHEST = lax.Precision.HIGHEST

GRID_W = 64
ROPE_THETA = 10000.0
NORM_EPS = 1e-6
LOG2E = math.log2(math.e)

DIFF_HEADS = 8
DIFF_QK_DIM = 64
GQA_HEADS = 12
GQA_KV_HEADS = 4
GQA_DIM = 128
MLSTM_HEADS = 12
MLSTM_DIM = 128
N_EXPERTS = 8
TOP_K = 2

LANES = 128
ROW_TILE = 256
NORM_ROWS = 16
ONES_ROWS = 16
DIFF_Q_TILE = 512
GQA_Q_TILE = 256
CONV_ROW_TILE = 768
MLSTM_CHUNK = 256
MOE_BLK = 1024
MOE_SLAB = 256
GATHER_ROWS = 256
VMEM_CAP = 60 * 1024 * 1024

_DQ, _DK, _DV = 0, 1024, 2048
_GQ, _GK, _GV = 3072, 4608, 5120
_MQ, _MK, _MV, _MO, _MG = 5632, 7168, 8704, 10240, 11776
_MAIN_WIDTH = 11776


def _cparams(sem, vmem_bytes):
    limit = int(min(VMEM_CAP, max(32 * 1024 * 1024, vmem_bytes + (8 << 20))))
    return pltpu.CompilerParams(dimension_semantics=sem, vmem_limit_bytes=limit)


def _row_tile(m, cands=(1408, 1024, 768, 512, 256)):
    for t in cands:
        if m % t == 0:
            return t
    raise ValueError(f"no row tile for {m}")


def _log_sigmoid(x):
    return jnp.minimum(x, 0.0) - jnp.log1p(jnp.exp(-jnp.abs(x)))


def _silu(x):
    return x * jax.nn.sigmoid(x)


def _mod_kernel(c_ref, w_ref, b_ref, o_ref):
    s = _silu(c_ref[...])
    o_ref[0] = jnp.dot(s, w_ref[0], preferred_element_type=F32) + b_ref[0]


def _mods(cc, w_mod, b_mod):
    n_layers, d, n6 = w_mod.shape
    tn = 512
    return pl.pallas_call(
        _mod_kernel,
        grid=(n_layers, n6 // tn),
        in_specs=[pl.BlockSpec((8, d), lambda l, j: (0, 0)),
                  pl.BlockSpec((1, d, tn), lambda l, j: (l, 0, j)),
                  pl.BlockSpec((1, 1, tn), lambda l, j: (l, 0, j))],
        out_specs=pl.BlockSpec((1, 8, tn), lambda l, j: (l, 0, j)),
        out_shape=jax.ShapeDtypeStruct((n_layers, 8, n6), F32),
        compiler_params=_cparams(("parallel", "parallel"), 2 * d * tn * 4),
        name="adaln_mods",
    )(cc, w_mod, b_mod.reshape(n_layers, 1, n6))


def _modnorm_kernel(*refs, n_lat, tm, with_router, split_input):
    n_x = 2 if split_input else 1
    x_refs, (g_ref, sh_ref, sc_ref), rest = refs[:n_x], refs[n_x:n_x + 3], refs[n_x + 3:]
    is_ctx_tile = pl.program_id(0) * tm >= n_lat
    mod_row = jnp.where(is_ctx_tile, 1, 0)
    gain = g_ref[...]
    scale1 = 1.0 + sc_ref[pl.ds(mod_row, 1), :]
    shift = sh_ref[pl.ds(mod_row, 1), :]
    o_ref = rest[2] if with_router else rest[0]

    def normalise(x_ref):
        def body(r, carry):
            rows = pl.ds(pl.multiple_of(r * NORM_ROWS, NORM_ROWS), NORM_ROWS)
            x = x_ref[rows, :]
            ms = jnp.mean(x * x, axis=-1, keepdims=True)
            o_ref[rows, :] = (x * lax.rsqrt(ms + NORM_EPS) * gain * scale1 + shift).astype(o_ref.dtype)
            if split_input:
                rest[1][rows, :] = x
            return carry

        lax.fori_loop(0, tm // NORM_ROWS, body, 0, unroll=4)

    if split_input:
        pl.when(is_ctx_tile)(lambda: normalise(x_refs[1]))
        pl.when(jnp.logical_not(is_ctx_tile))(lambda: normalise(x_refs[0]))
    else:
        normalise(x_refs[0])
    if with_router:
        rw_ref, rb_ref, _, lg_ref = rest
        lg_ref[...] = jnp.dot(o_ref[...], rw_ref[...], preferred_element_type=F32, precision=HIGHEST) + rb_ref[...]


def _modnorm(x, gain, mods, shift_idx, n_lat, router=None, out_dtype=BF16, ctx_rows=None):
    d = x.shape[1]
    tm = ROW_TILE
    m = x.shape[0] + (0 if ctx_rows is None else ctx_rows.shape[0])
    if ctx_rows is None:
        in_specs, args = [pl.BlockSpec((tm, d), lambda i: (i, 0))], [x]
    else:
        assert router is None and x.shape[0] == n_lat and ctx_rows.shape[0] == tm
        last = n_lat // tm - 1
        in_specs = [pl.BlockSpec((tm, d), lambda i: (jnp.minimum(i, last), 0)), pl.BlockSpec((tm, d), lambda i: (0, 0))]
        args = [x, ctx_rows]
    in_specs += [pl.BlockSpec((1, d), lambda i: (0, 0)),
                 pl.BlockSpec((8, d), lambda i: (0, shift_idx)),
                 pl.BlockSpec((8, d), lambda i: (0, shift_idx + 1))]
    args += [gain.reshape(1, d), mods, mods]
    out_specs = pl.BlockSpec((tm, d), lambda i: (i, 0))
    out_shape = jax.ShapeDtypeStruct((m, d), out_dtype)
    if router is not None:
        rw, rb = router
        in_specs += [pl.BlockSpec((d, LANES), lambda i: (0, 0)),
                     pl.BlockSpec((1, LANES), lambda i: (0, 0))]
        args += [rw, rb]
        out_specs = [out_specs, pl.BlockSpec((tm, LANES), lambda i: (i, 0))]
        out_shape = [out_shape, jax.ShapeDtypeStruct((m, LANES), F32)]
    elif ctx_rows is not None:
        out_specs = [out_specs, pl.BlockSpec((tm, d), lambda i: (i, 0))]
        out_shape = [out_shape, jax.ShapeDtypeStruct((m, d), x.dtype)]
    return pl.pallas_call(
        functools.partial(_modnorm_kernel, n_lat=n_lat, tm=tm, with_router=router is not None,
                          split_input=ctx_rows is not None),
        grid=(m // tm,),
        in_specs=in_specs, out_specs=out_specs, out_shape=out_shape,
        compiler_params=_cparams(("parallel",), 2 * tm * d * 6 + 6 * tm * d * 4 + d * LANES * 8),
        name="modnorm",
    )(*args)


def _mxu_operand(w):
    return w if w.dtype == BF16 else w.astype(BF16)


def _mm_kernel(*refs, a_of_w, kind, n_lat, tm, nt=False, valid_cols=None):
    n_a = 1 + max(ai for ai, _ in a_of_w)
    n_w = len(a_of_w)
    a_refs, w_refs, rest = refs[:n_a], refs[n_a:n_a + n_w], refs[n_a + n_w:]
    col_axis, k_axis = (0, 1) if nt else (1, 0)

    def lhs(i):
        ai, off = a_of_w[i]
        return a_refs[ai][:, off:off + w_refs[i].shape[k_axis]]

    def rhs(i):
        w = w_refs[i][...]
        if valid_cols is not None:
            w = jnp.where(lax.broadcasted_iota(jnp.int32, w.shape, col_axis) < valid_cols, w, 0.0)
        return _mxu_operand(w)

    def mm(i):
        return lax.dot_general(lhs(i), rhs(i), (((1,), (k_axis,)), ((), ())), preferred_element_type=F32)

    if kind == "swiglu":
        g, u = mm(0), mm(1)
        (o_ref,) = rest
        o_ref[...] = (_silu(g) * u).astype(o_ref.dtype)
        return
    acc = mm(0)
    for i in range(1, n_w):
        acc = acc + mm(i)
    if kind == "plain":
        (o_ref,) = rest
        o_ref[...] = acc.astype(o_ref.dtype)
    else:
        x_ref, g_ref, o_ref = rest
        rows = pl.program_id(0) * tm + lax.broadcasted_iota(jnp.int32, (tm, 1), 0)
        gate = jnp.where(rows >= n_lat, g_ref[1:2, :], g_ref[0:1, :])
        o_ref[...] = x_ref[...] + gate * acc


def _matmul(a_list, w_list, *, rows, n, tm, tn, col0=0, kind="plain", out_dtype=F32, resid=None, n_lat=0,
            nt=False, valid_cols=None, name="matmul"):
    assert col0 % tn == 0
    cb = col0 // tn
    in_specs = [pl.BlockSpec((tm, a.shape[1]), lambda i, j: (i, 0)) for a in a_list]
    args = list(a_list)
    vmem = sum(2 * tm * a.shape[1] * 2 for a in a_list) + 6 * tm * tn * 4
    for w3d, lead, _, _, k_rows, k_blk in w_list:
        if nt:
            in_specs.append(pl.BlockSpec((None, tn, k_rows), lambda i, j, lead=lead, k_blk=k_blk: (lead, cb + j, k_blk)))
        else:
            in_specs.append(pl.BlockSpec((None, k_rows, tn), lambda i, j, lead=lead, k_blk=k_blk: (lead, k_blk, cb + j)))
        args.append(w3d)
        vmem += k_rows * tn * (2 * w3d.dtype.itemsize + 2)
    if kind == "resid":
        x, mods, gate_idx = resid
        d = x.shape[1]
        in_specs += [pl.BlockSpec((tm, tn), lambda i, j: (i, j)),
                     pl.BlockSpec((8, tn), lambda i, j: (0, gate_idx * (d // tn) + j))]
        args += [x, mods]
    a_of_w = tuple((a_idx, a_off) for _, _, a_idx, a_off, _, _ in w_list)
    return pl.pallas_call(
        functools.partial(_mm_kernel, a_of_w=a_of_w, kind=kind, n_lat=n_lat, tm=tm, nt=nt, valid_cols=valid_cols),
        grid=(rows // tm, n // tn),
        in_specs=in_specs,
        out_specs=pl.BlockSpec((tm, tn), lambda i, j: (i, j)),
        out_shape=jax.ShapeDtypeStruct((rows, n), out_dtype),
        compiler_params=_cparams(("parallel", "arbitrary"), vmem),
        name=name,
    )(*args)


def _qkprep_kernel(x_ref, g_ref, c_ref, s_ref, o_ref, *, n_tiles, seg, scale, split):
    lane = lax.broadcasted_iota(jnp.int32, c_ref.shape, 1)
    gain, cos, sin = g_ref[...], c_ref[...], s_ref[...]
    half = seg // 2
    ow = 2 * LANES if split else LANES
    for j in range(n_tiles):
        x = x_ref[:, j * LANES:(j + 1) * LANES].astype(F32)
        x2 = x * x
        if seg == LANES:
            ms = jnp.sum(x2, axis=-1, keepdims=True) * (1.0 / seg)
        else:
            lo = lane < seg
            s_lo = jnp.sum(jnp.where(lo, x2, 0.0), axis=-1, keepdims=True)
            s_hi = jnp.sum(jnp.where(lo, 0.0, x2), axis=-1, keepdims=True)
            ms = jnp.where(lo, s_lo, s_hi) * (1.0 / seg)
        y = x * lax.rsqrt(ms + NORM_EPS) * gain
        if seg == LANES:
            partner = pltpu.roll(y, half, 1)
        else:
            partner = jnp.where((lane % seg) < half, pltpu.roll(y, LANES - half, 1), pltpu.roll(y, half, 1))
        y = (y * cos + partner * sin) * scale
        if split:
            o_ref[:, j * ow:j * ow + LANES] = jnp.where(lane < seg, y, 0.0).astype(o_ref.dtype)
            o_ref[:, j * ow + LANES:(j + 1) * ow] = jnp.where(lane < seg, 0.0, y).astype(o_ref.dtype)
        else:
            o_ref[:, j * ow:(j + 1) * ow] = y.astype(o_ref.dtype)


def _qkprep(proj, col0, n_tiles, gain, rope_c, rope_s, *, seg, scale, split):
    t = proj.shape[0]
    tm = ROW_TILE
    width = n_tiles * LANES
    assert col0 % width == 0
    cb = col0 // width
    g = jnp.tile(gain.astype(F32), LANES // seg).reshape(1, LANES)
    ow = (2 if split else 1) * width
    return pl.pallas_call(
        functools.partial(_qkprep_kernel, n_tiles=n_tiles, seg=seg, scale=scale, split=split),
        grid=(t // tm,),
        in_specs=[pl.BlockSpec((tm, width), lambda i: (i, cb)),
                  pl.BlockSpec((1, LANES), lambda i: (0, 0)),
                  pl.BlockSpec((tm, LANES), lambda i: (i, 0)),
                  pl.BlockSpec((tm, LANES), lambda i: (i, 0))],
        out_specs=pl.BlockSpec((tm, ow), lambda i: (i, 0)),
        out_shape=jax.ShapeDtypeStruct((t, ow), BF16),
        compiler_params=_cparams(("parallel",), 0),
        name="qk_prep",
    )(proj, g, rope_c, rope_s)


def _attn_kernel(q_ref, k_ref, vt_ref, *rest, n_q_heads, tq, tk, lam_init):
    n_keys = k_ref.shape[0]
    qs = [q_ref[:, g * LANES:(g + 1) * LANES] for g in range(n_q_heads)]

    def scores(kc):
        return tuple(lax.dot_general(kc, q, (((1,), (1,)), ((), ())), preferred_element_type=F32) for q in qs)

    def update(ss, vtc, carry):
        new = []
        for s, (m, acc) in zip(ss, carry):
            m_new = jnp.maximum(m, jnp.max(s, axis=0, keepdims=True))
            p = jnp.exp2(s - m_new).astype(BF16)
            new.append((m_new, jnp.exp2(m - m_new) * acc + jnp.dot(vtc, p, preferred_element_type=F32)))
        return tuple(new)

    carry = tuple((jnp.full((1, tq), -jnp.inf, F32), jnp.zeros((vt_ref.shape[0], tq), F32))
                  for _ in range(n_q_heads))
    ss = scores(k_ref[0:tk, :])
    for c in range(n_keys // tk):
        ss_next = scores(k_ref[(c + 1) * tk:(c + 2) * tk, :]) if (c + 1) * tk < n_keys else None
        carry = update(ss, vt_ref[:, c * tk:(c + 1) * tk], carry)
        ss = ss_next
    outs_t = [acc[:LANES] / acc[LANES:LANES + 1] for _, acc in carry]
    if lam_init is None:
        (o_ref,) = rest
        for g in range(n_q_heads):
            o_ref[:, g * LANES:(g + 1) * LANES] = outs_t[g].T.astype(o_ref.dtype)
    else:
        lam_ref, g_ref, o_ref = rest
        lp = lam_ref[...]
        lam = (jnp.exp(jnp.sum(lp[0:1, :] * lp[1:2, :], axis=-1, keepdims=True))
               - jnp.exp(jnp.sum(lp[2:3, :] * lp[3:4, :], axis=-1, keepdims=True)) + lam_init)
        o = (outs_t[0] - lam * outs_t[1]).T
        ms = jnp.mean(o * o, axis=-1, keepdims=True)
        o_ref[...] = (o * lax.rsqrt(ms + NORM_EPS) * g_ref[...] * (1.0 - lam_init)).astype(o_ref.dtype)


def _attention(q, k, v, *, n_kv, n_q_heads, n_lat, n_ctx, with_ctx_queries, tq, diff=None):
    t = k.shape[0]
    qw = n_q_heads * LANES
    ow = LANES if diff is not None else qw
    hv = LANES + ONES_ROWS
    vt = jnp.concatenate([v.T.reshape(n_kv, LANES, t), jnp.ones((n_kv, ONES_ROWS, t), v.dtype)], axis=1)
    vt = vt.reshape(n_kv * hv, t)
    extra_specs, extra_args, lam_init = [], [], None
    if diff is not None:
        lam_p, gain, lam_init = diff
        extra_specs = [pl.BlockSpec((4, DIFF_QK_DIM), lambda h, i: (0, 0)), pl.BlockSpec((1, LANES), lambda h, i: (0, 0))]
        extra_args = [lam_p.astype(F32), gain.astype(F32).reshape(1, LANES)]

    def call(q_rows, q_row0, n_keys, key0, tq, tk):
        qb0, kb0 = q_row0 // tq, key0 // n_keys
        vmem = 2 * 2 * n_keys * LANES * 2 + 4 * tq * qw * 4 + 8 * n_q_heads * tq * tk * 4
        return pl.pallas_call(
            functools.partial(_attn_kernel, n_q_heads=n_q_heads, tq=tq, tk=tk, lam_init=lam_init),
            grid=(n_kv, q_rows // tq),
            in_specs=[pl.BlockSpec((tq, qw), lambda h, i: (qb0 + i, h)),
                      pl.BlockSpec((n_keys, LANES), lambda h, i: (kb0, h)),
                      pl.BlockSpec((hv, n_keys), lambda h, i: (h, kb0))] + extra_specs,
            out_specs=pl.BlockSpec((tq, ow), lambda h, i: (i, h)),
            out_shape=jax.ShapeDtypeStruct((q_rows, n_kv * ow), BF16),
            compiler_params=_cparams(("parallel", "arbitrary"), vmem),
            name="attention",
        )(q, k, vt, *extra_args)

    tq = math.gcd(tq, n_lat)
    out = call(n_lat, 0, t, 0, tq, next(c for c in (768, 512, 256) if t % c == 0))
    if with_ctx_queries:
        assert n_lat % n_ctx == 0
        out = jnp.concatenate([out, call(n_ctx, n_lat, n_ctx, n_lat, n_ctx, n_ctx)], axis=0)
    return out


def _conv_kernel(x_ref, xp_ref, xn_ref, w_ref, b_ref, o_ref, *, n_lat, t_all, tm, k_col_tile):
    i, j = pl.program_id(0), pl.program_id(1)
    x = x_ref[...].astype(F32)
    row = lax.broadcasted_iota(jnp.int32, (tm, 1), 0)
    grow = i * tm + row
    prev = jnp.where(row == 0, xp_ref[15:16, :].astype(F32), pltpu.roll(x, 1, 0))
    nxt = jnp.where(row == tm - 1, xn_ref[0:1, :].astype(F32), pltpu.roll(x, tm - 1, 0))
    prev = jnp.where((grow == 0) | (grow == n_lat), 0.0, prev)
    nxt = jnp.where((grow == n_lat - 1) | (grow == t_all - 1), 0.0, nxt)
    y = b_ref[...] + prev * w_ref[0:1, :] + x * w_ref[1:2, :] + nxt * w_ref[2:3, :]
    y = _silu(y) * jnp.where(j >= k_col_tile, MLSTM_DIM ** -0.5, 1.0)
    o_ref[...] = y.astype(o_ref.dtype)


def _mlstm_conv(proj, conv_w, conv_b, n_lat):
    t = proj.shape[0]
    tm, tc, hb = (CONV_ROW_TILE if t % CONV_ROW_TILE == 0 else ROW_TILE), 512, 16
    width = 2 * MLSTM_HEADS * MLSTM_DIM
    cb = _MQ // tc
    nrb = t // hb
    return pl.pallas_call(
        functools.partial(_conv_kernel, n_lat=n_lat, t_all=t, tm=tm, k_col_tile=(width // 2) // tc),
        grid=(t // tm, width // tc),
        in_specs=[pl.BlockSpec((tm, tc), lambda i, j: (i, cb + j)),
                  pl.BlockSpec((hb, tc), lambda i, j: (jnp.maximum(i * (tm // hb) - 1, 0), cb + j)),
                  pl.BlockSpec((hb, tc), lambda i, j: (jnp.minimum((i + 1) * (tm // hb), nrb - 1), cb + j)),
                  pl.BlockSpec((3, tc), lambda i, j: (0, j)),
                  pl.BlockSpec((1, tc), lambda i, j: (0, j))],
        out_specs=pl.BlockSpec((tm, tc), lambda i, j: (i, j)),
        out_shape=jax.ShapeDtypeStruct((t, width), BF16),
        compiler_params=_cparams(("parallel", "arbitrary"), 0),
        name="mlstm_conv",
    )(proj, proj, proj, conv_w.astype(F32), conv_b.astype(F32).reshape(1, width))


def _mlstm_kernel(q_ref, k_ref, vt_ref, gi_ref, gf_ref, bi_ref, bf_ref, mask_ref, ht_ref, ct_sc, n_sc, m_sc, *, n_heads):
    @pl.when(pl.program_id(0) == 0)
    def _():
        ct_sc[...] = jnp.zeros_like(ct_sc)
        n_sc[...] = jnp.zeros_like(n_sc)
        m_sc[...] = jnp.zeros_like(m_sc)

    vis_t = mask_ref[...]
    li_all = gi_ref[...] + bi_ref[...]
    lf_all = _log_sigmoid(gf_ref[...] + bf_ref[...])
    br_all = jnp.dot(lf_all, vis_t, preferred_element_type=F32, precision=HIGHEST)
    bend_all = jnp.sum(lf_all, axis=1, keepdims=True)
    c_cols = (br_all - li_all).T
    d = MLSTM_DIM
    nt = (((1,), (1,)), ((), ()))
    for h in range(n_heads):
        lir, br = li_all[h:h + 1, :], br_all[h:h + 1, :]
        b_end = bend_all[h:h + 1, :]
        m_prev = m_sc[h:h + 1, 0:1]
        lw = b_end - br + lir
        m_new = jnp.maximum(b_end + m_prev, jnp.max(lw, axis=1, keepdims=True))
        w_end = jnp.exp(lw - m_new)
        decay = jnp.exp(b_end + m_prev - m_new)
        log_d = jnp.where(vis_t > 0.0, br - c_cols[:, h:h + 1], -jnp.inf)
        m_t = jnp.maximum(br + m_prev, jnp.max(log_d, axis=0, keepdims=True))
        dmat = jnp.exp(log_d - m_t)
        inter = jnp.exp(br + m_prev - m_t)
        qh = q_ref[:, h * d:(h + 1) * d]
        kh = k_ref[:, h * d:(h + 1) * d]
        vth = vt_ref[h * d:(h + 1) * d, :]
        ct_prev = ct_sc[h]
        n_prev = n_sc[h:h + 1, :]
        sc = lax.dot_general(kh, qh, nt, preferred_element_type=F32) * dmat
        num = (jnp.dot(vth, sc.astype(BF16), preferred_element_type=F32)
               + inter * lax.dot_general(ct_prev.astype(BF16), qh, nt, preferred_element_type=F32))
        n8 = jnp.broadcast_to(n_prev, (8, d)).astype(BF16)
        den = (jnp.sum(sc, axis=0, keepdims=True)
               + inter * lax.dot_general(n8, qh, nt, preferred_element_type=F32)[0:1, :])
        ht_ref[h * d:(h + 1) * d, :] = num / jnp.maximum(jnp.abs(den), jnp.exp(-m_t))
        vtw = (vth.astype(F32) * w_end).astype(BF16)
        ct_sc[h] = decay * ct_prev + jnp.dot(vtw, kh, preferred_element_type=F32)
        w8 = jnp.broadcast_to(w_end, (8, w_end.shape[1])).astype(BF16)
        n_sc[h:h + 1, :] = decay * n_prev + jnp.dot(w8, kh, preferred_element_type=F32)[0:1, :]
        m_sc[h:h + 1, :] = jnp.broadcast_to(m_new, (1, LANES))


def _mlstm_direction(qk, vt, gi, gf, bi, bf, vis_t, *, n_lat, reverse):
    t = qk.shape[0]
    lc = MLSTM_CHUNK
    n_chunks = t // lc
    ctx_blk = n_lat // lc
    hd = MLSTM_HEADS * MLSTM_DIM

    def blk(c):
        lat = (n_chunks - 1 - c) if reverse else (c - 1)
        return jnp.where(c == 0, ctx_blk, lat)

    return pl.pallas_call(
        functools.partial(_mlstm_kernel, n_heads=MLSTM_HEADS),
        grid=(n_chunks,),
        in_specs=[pl.BlockSpec((lc, hd), lambda c: (blk(c), 0)),
                  pl.BlockSpec((lc, hd), lambda c: (blk(c), 1)),
                  pl.BlockSpec((hd, lc), lambda c: (0, blk(c))),
                  pl.BlockSpec((16, lc), lambda c: (0, blk(c))),
                  pl.BlockSpec((16, lc), lambda c: (0, blk(c))),
                  pl.BlockSpec((16, 1), lambda c: (0, 0)),
                  pl.BlockSpec((16, 1), lambda c: (0, 0)),
                  pl.BlockSpec((lc, lc), lambda c: (0, 0))],
        out_specs=pl.BlockSpec((hd, lc), lambda c: (0, blk(c))),
        out_shape=jax.ShapeDtypeStruct((hd, t), F32),
        scratch_shapes=[pltpu.VMEM((MLSTM_HEADS, MLSTM_DIM, MLSTM_DIM), F32),
                        pltpu.VMEM((16, LANES), F32),
                        pltpu.VMEM((16, LANES), F32)],
        compiler_params=_cparams(("arbitrary",), 16 << 20),
        name="mlstm_bwd" if reverse else "mlstm_fwd",
    )(qk, qk, vt, gi, gf, bi, bf, vis_t)


def _mlstm_finish_kernel(hf_ref, hb_ref, o0_ref, o1_ref, o2_ref, g_ref, out_ref, *, n_heads):
    d = MLSTM_DIM
    o_refs = (o0_ref, o1_ref, o2_ref)
    per_blk = o0_ref.shape[1] // d
    for h in range(n_heads):
        sl = slice(h * d, (h + 1) * d)
        s = hf_ref[sl, :] + hb_ref[sl, :]
        ms = jnp.mean(s * s, axis=0, keepdims=True)
        y = (s * lax.rsqrt(ms + NORM_EPS)).T * g_ref[:, sl]
        o = o_refs[h // per_blk][:, (h % per_blk) * d:(h % per_blk + 1) * d]
        out_ref[:, sl] = (jax.nn.sigmoid(o.astype(F32)) * y).astype(out_ref.dtype)


def _mlstm_finish(hf_t, hb_t, proj, out_gain, rows):
    hd = MLSTM_HEADS * MLSTM_DIM
    tm = ROW_TILE
    ow = hd // 3
    assert _MO % ow == 0
    return pl.pallas_call(
        functools.partial(_mlstm_finish_kernel, n_heads=MLSTM_HEADS),
        grid=(rows // tm,),
        in_specs=[pl.BlockSpec((hd, tm), lambda i: (0, i)),
                  pl.BlockSpec((hd, tm), lambda i: (0, i)),
                  pl.BlockSpec((tm, ow), lambda i: (i, _MO // ow)),
                  pl.BlockSpec((tm, ow), lambda i: (i, _MO // ow + 1)),
                  pl.BlockSpec((tm, ow), lambda i: (i, _MO // ow + 2)),
                  pl.BlockSpec((1, hd), lambda i: (0, 0))],
        out_specs=pl.BlockSpec((tm, hd), lambda i: (i, 0)),
        out_shape=jax.ShapeDtypeStruct((rows, hd), BF16),
        compiler_params=_cparams(("parallel",), 0),
        name="mlstm_finish",
    )(hf_t, hb_t, proj, proj, proj, out_gain.astype(F32).reshape(1, hd))


def _moe_up_kernel(be_ref, nv_ref, st_ref, tok_ref, wg_ref, wu_ref, o_ref, xbuf, xb, sem, *, nj):
    b, j = pl.program_id(0), pl.program_id(1)
    nb = pl.num_programs(0)
    n_valid = nv_ref[b]
    per_step = MOE_BLK // nj

    def copy(blk, r):
        src = tok_ref.at[pl.ds(st_ref[blk * MOE_BLK + r], 1), :]
        return pltpu.make_async_copy(src, xbuf.at[pl.ds(r, 1), :], sem)

    @pl.when((b == 0) & (j == 0))
    def _():
        def body(r, carry):
            copy(0, r).start()
            return carry
        lax.fori_loop(0, MOE_BLK, body, 0)

    was_requested = (b == 0) | (nv_ref[jnp.maximum(b - 1, 0)] > 0)

    @pl.when((j == 0) & was_requested)
    def _():
        pltpu.make_async_copy(tok_ref.at[pl.ds(0, MOE_BLK), :], xbuf, sem).wait()
        xb[...] = xbuf[...].astype(BF16)

    def request_next():
        nxt = jnp.minimum(b + 1, nb - 1)
        for r in range(per_step):
            copy(nxt, j * per_step + r).start()

    def swiglu(sl):
        a = xb[sl, :]
        g = jnp.dot(a, _mxu_operand(wg_ref[...]), preferred_element_type=F32)
        u = jnp.dot(a, _mxu_operand(wu_ref[...]), preferred_element_type=F32)
        o_ref[sl, :] = (_silu(g) * u).astype(o_ref.dtype)

    @pl.when(n_valid == MOE_BLK)
    def _():
        request_next()
        swiglu(slice(None))

    @pl.when((n_valid > 0) & (n_valid < MOE_BLK))
    def _():
        request_next()
        for s in range(MOE_BLK // MOE_SLAB):
            sl = slice(s * MOE_SLAB, (s + 1) * MOE_SLAB)

            @pl.when(n_valid > s * MOE_SLAB)
            def _():
                swiglu(sl)

            @pl.when(n_valid <= s * MOE_SLAB)
            def _():
                o_ref[sl, :] = jnp.zeros((MOE_SLAB, o_ref.shape[1]), o_ref.dtype)

    @pl.when(n_valid == 0)
    def _():
        o_ref[...] = jnp.zeros_like(o_ref)

    @pl.when((b == nb - 1) & (j == nj - 1) & (n_valid > 0))
    def _():
        pltpu.make_async_copy(tok_ref.at[pl.ds(0, MOE_BLK), :], xbuf, sem).wait()


def _moe_up(tok, w_gate, w_up, lead, block_e, n_valid, slot_tok, *, tn):
    n_slots = slot_tok.shape[0]
    k = tok.shape[1]
    n = w_gate.shape[-1]
    nb, nj = n_slots // MOE_BLK, n // tn
    assert MOE_BLK % nj == 0

    def w_map(b, j, be, nv, st):
        return (lead, be[b], 0, jnp.where(nv[b] > 0, j, nj - 1))

    return pl.pallas_call(
        functools.partial(_moe_up_kernel, nj=nj),
        grid_spec=pltpu.PrefetchScalarGridSpec(
            num_scalar_prefetch=3,
            grid=(nb, nj),
            in_specs=[pl.BlockSpec(memory_space=pl.ANY),
                      pl.BlockSpec((None, None, k, tn), w_map),
                      pl.BlockSpec((None, None, k, tn), w_map)],
            out_specs=pl.BlockSpec((MOE_BLK, tn), lambda b, j, be, nv, st: (b, j)),
            scratch_shapes=[pltpu.VMEM((MOE_BLK, k), F32),
                            pltpu.VMEM((MOE_BLK, k), BF16),
                            pltpu.SemaphoreType.DMA(())]),
        out_shape=jax.ShapeDtypeStruct((n_slots, n), BF16),
        compiler_params=_cparams(("arbitrary", "arbitrary"),
                                 MOE_BLK * k * 6 + 2 * k * tn * (2 * w_gate.dtype.itemsize + 2)
                                 + 4 * MOE_BLK * tn * 4),
        name="moe_up",
    )(block_e, n_valid, slot_tok, tok, w_gate, w_up)


def _moe_down_kernel(be_ref, nv_ref, nu_ref, a_ref, w_ref, o_ref):
    n_valid = nv_ref[pl.program_id(0)]

    def down(sl):
        o_ref[sl, :] = jnp.dot(a_ref[sl, :], _mxu_operand(w_ref[...]), preferred_element_type=F32)

    @pl.when(n_valid == MOE_BLK)
    def _():
        down(slice(None))

    @pl.when((n_valid > 0) & (n_valid < MOE_BLK))
    def _():
        for s in range(MOE_BLK // MOE_SLAB):
            sl = slice(s * MOE_SLAB, (s + 1) * MOE_SLAB)

            @pl.when(n_valid > s * MOE_SLAB)
            def _():
                down(sl)

            @pl.when(n_valid <= s * MOE_SLAB)
            def _():
                o_ref[sl, :] = jnp.zeros((MOE_SLAB, o_ref.shape[1]), o_ref.dtype)

    @pl.when(n_valid == 0)
    def _():
        o_ref[...] = jnp.zeros_like(o_ref)


def _moe_down(a, w, lead, block_e, n_valid, n_used, *, tn):
    n_slots, k = a.shape
    n = w.shape[-1]
    nb, nj = n_slots // MOE_BLK, n // tn

    def a_map(b, j, be, nv, nu):
        return (jnp.minimum(b, nu[0] - 1), 0)

    def w_map(b, j, be, nv, nu):
        return (lead, be[b], 0, jnp.where(nv[b] > 0, j, nj - 1))

    return pl.pallas_call(
        _moe_down_kernel,
        grid_spec=pltpu.PrefetchScalarGridSpec(
            num_scalar_prefetch=3,
            grid=(nb, nj),
            in_specs=[pl.BlockSpec((MOE_BLK, k), a_map), pl.BlockSpec((None, None, k, tn), w_map)],
            out_specs=pl.BlockSpec((MOE_BLK, tn), lambda b, j, be, nv, nu: (b, j))),
        out_shape=jax.ShapeDtypeStruct((n_slots, n), F32),
        compiler_params=_cparams(("arbitrary", "arbitrary"),
                                 2 * MOE_BLK * k * 2 + k * tn * (2 * w.dtype.itemsize + 2) + 4 * MOE_BLK * tn * 4),
        name="moe_down",
    )(block_e, n_valid, n_used, a, w)


def _combine_kernel(d0_ref, d1_ref, y_ref, x_ref, w0_ref, w1_ref, g_ref, o_ref, buf, sem, *, rows):
    i = pl.program_id(0)
    n = pl.num_programs(0)
    slot = i % 2

    def gather(step, slot_):
        base = step * rows

        def body(r, carry):
            for which, d_ref in enumerate((d0_ref, d1_ref)):
                pltpu.make_async_copy(y_ref.at[pl.ds(d_ref[base + r], 1), :],
                                      buf.at[slot_, which, pl.ds(r, 1), :], sem.at[slot_]).start()
            return carry

        lax.fori_loop(0, rows, body, 0, unroll=8)

    @pl.when(i == 0)
    def _():
        gather(0, 0)

    @pl.when(i + 1 < n)
    def _():
        gather(i + 1, 1 - slot)

    for which in range(2):
        pltpu.make_async_copy(y_ref.at[pl.ds(0, rows), :], buf.at[slot, which], sem.at[slot]).wait()
    y = w0_ref[...] * buf[slot, 0] + w1_ref[...] * buf[slot, 1]
    o_ref[...] = x_ref[...] + g_ref[0:1, :] * y


def _moe_combine(y_slots, x, d0, d1, w0, w1, mods, gate_idx):
    n_tok, d = x.shape
    rows = GATHER_ROWS // 2
    return pl.pallas_call(
        functools.partial(_combine_kernel, rows=rows),
        grid_spec=pltpu.PrefetchScalarGridSpec(
            num_scalar_prefetch=2,
            grid=(n_tok // rows,),
            in_specs=[pl.BlockSpec(memory_space=pl.ANY),
                      pl.BlockSpec((rows, d), lambda i, a, b: (i, 0)),
                      pl.BlockSpec((rows, 1), lambda i, a, b: (i, 0)),
                      pl.BlockSpec((rows, 1), lambda i, a, b: (i, 0)),
                      pl.BlockSpec((8, d), lambda i, a, b: (0, gate_idx))],
            out_specs=pl.BlockSpec((rows, d), lambda i, a, b: (i, 0)),
            scratch_shapes=[pltpu.VMEM((2, 2, rows, d), F32), pltpu.SemaphoreType.DMA((2,))]),
        out_shape=jax.ShapeDtypeStruct((n_tok, d), F32),
        compiler_params=_cparams(("arbitrary",), 12 * rows * d * 4),
        name="moe_combine",
    )(d0, d1, y_slots, x, w0, w1, mods)


def _moe_routing(logits, n_tok):
    top_logit, top_idx = lax.top_k(logits, TOP_K)
    top_w = jax.nn.softmax(top_logit, axis=-1)
    flat_e = top_idx.reshape(-1).astype(jnp.int32)
    n_assign = n_tok * TOP_K
    onehot = (flat_e[:, None] == jnp.arange(N_EXPERTS, dtype=jnp.int32)[None, :]).astype(jnp.int32)
    csum = jnp.cumsum(onehot, axis=0)
    rank = jnp.take_along_axis(csum, flat_e[:, None], axis=1)[:, 0] - 1
    counts = csum[-1]
    padded = (counts + MOE_BLK - 1) // MOE_BLK * MOE_BLK
    pad_ends = jnp.cumsum(padded)
    pad_starts = pad_ends - padded
    dest = (pad_starts[flat_e] + rank).astype(jnp.int32)
    n_blocks = -(-(n_assign + N_EXPERTS * (MOE_BLK - 1)) // MOE_BLK)
    n_slots = n_blocks * MOE_BLK
    slot_tok = jnp.zeros((n_slots,), jnp.int32).at[dest].set(jnp.arange(n_assign, dtype=jnp.int32) // TOP_K)
    blk_start = jnp.arange(n_blocks, dtype=jnp.int32) * MOE_BLK
    block_e = jnp.minimum(jnp.searchsorted(pad_ends, blk_start, side="right"), N_EXPERTS - 1).astype(jnp.int32)
    slab_ends = pad_starts + (counts + MOE_SLAB - 1) // MOE_SLAB * MOE_SLAB
    n_valid = jnp.clip(slab_ends[block_e] - blk_start, 0, MOE_BLK).astype(jnp.int32)
    n_used = (pad_ends[-1] // MOE_BLK).astype(jnp.int32).reshape(1)
    dest2 = dest.reshape(n_tok, TOP_K)
    return dest2[:, 0], dest2[:, 1], top_w[:, 0:1], top_w[:, 1:2], slot_tok, block_e, n_valid, n_used


def _rope_tables(n_lat, n_ctx, dim):
    rows = n_lat // GRID_W
    row = np.repeat(np.arange(rows, dtype=np.float32), GRID_W)
    col = np.tile(np.arange(GRID_W, dtype=np.float32), rows)
    quarter = dim // 4
    inv = (np.float32(ROPE_THETA) ** (-np.arange(quarter, dtype=np.float32) / np.float32(quarter))).astype(np.float32)
    ang = np.concatenate([row[:, None] * inv, col[:, None] * inv], axis=-1).astype(np.float32)
    cos, sin = np.cos(ang), np.sin(ang)
    reps = LANES // dim
    c = np.tile(np.concatenate([cos, cos], axis=-1), (1, reps))
    s = np.tile(np.concatenate([-sin, sin], axis=-1), (1, reps))
    c = np.concatenate([c, np.ones((n_ctx, LANES), np.float32)], axis=0)
    s = np.concatenate([s, np.zeros((n_ctx, LANES), np.float32)], axis=0)
    return jnp.asarray(c, F32), jnp.asarray(s, F32)


def kernel(x, c, ctx, c_ctx, w_mod, b_mod, attn_norm, ffn_norm, w_in, w_out, diff_q_norm, diff_k_norm, diff_lambda, diff_out_norm, gqa_q_norm, gqa_k_norm, mlstm_conv_w, mlstm_conv_b, mlstm_gate_b, mlstm_out_norm, ffn_w_gate, ffn_w_up, ffn_w_down, router_w, router_b, exp_w_gate, exp_w_up, exp_w_down):
    batch, n_lat, d = x.shape
    n_ctx = ctx.shape[1]
    depth = w_in.shape[0]
    assert batch == 1 and n_ctx == ROW_TILE and n_lat % MLSTM_CHUNK == 0 and n_lat % GRID_W == 0
    t_all = n_lat + n_ctx
    hd = MLSTM_HEADS * MLSTM_DIM

    cc = jnp.zeros((8, d), F32).at[0].set(c[0]).at[1].set(c_ctx)
    mods_all = _mods(cc, w_mod, b_mod)
    rope_diff = _rope_tables(n_lat, n_ctx, DIFF_QK_DIM)
    rope_gqa = _rope_tables(n_lat, n_ctx, GQA_DIM)
    pos = jnp.arange(MLSTM_CHUNK)
    vis_fwd = (pos[None, :] <= pos[:, None]).astype(F32)
    vis_bwd = (pos[None, :] >= pos[:, None]).astype(F32)

    w_in_t = jnp.swapaxes(w_in, 1, 2)
    xa = None
    for l in range(depth):
        need_ctx = l < depth - 1
        rows = t_all if need_ctx else n_lat
        lam_init = 0.8 - 0.6 * math.exp(-0.3 * l)
        mods = mods_all[l]

        if l == 0:
            h, xa = _modnorm(x[0], attn_norm[l], mods, 0, n_lat, ctx_rows=ctx[0])
        else:
            h = _modnorm(xa, attn_norm[l], mods, 0, n_lat)
        proj = _matmul([h], [(w_in_t, l, 0, 0, d, 0)], rows=t_all, n=_MAIN_WIDTH, tm=_row_tile(t_all), tn=512,
                       nt=True, out_dtype=BF16, name="in_proj")
        gates = _matmul([h], [(w_in_t, l, 0, 0, d, 0)], rows=t_all, n=LANES, col0=_MG, tm=_row_tile(t_all), tn=LANES,
                        nt=True, valid_cols=w_in.shape[-1] - _MG, name="gate_proj")

        dq = _qkprep(proj, _DQ, DIFF_HEADS, diff_q_norm[l], *rope_diff, seg=DIFF_QK_DIM,
                     scale=DIFF_QK_DIM ** -0.5 * LOG2E, split=True)
        dk = _qkprep(proj, _DK, DIFF_HEADS, diff_k_norm[l], *rope_diff, seg=DIFF_QK_DIM, scale=1.0, split=False)
        mix_d = _attention(dq, dk, proj[:, _DV:_GQ], n_kv=DIFF_HEADS, n_q_heads=2,
                           n_lat=n_lat, n_ctx=n_ctx, with_ctx_queries=need_ctx, tq=DIFF_Q_TILE,
                           diff=(diff_lambda[l], diff_out_norm[l], lam_init))

        gq = _qkprep(proj, _GQ, GQA_HEADS, gqa_q_norm[l], *rope_gqa, seg=GQA_DIM, scale=GQA_DIM ** -0.5 * LOG2E,
                     split=False)
        gk = _qkprep(proj, _GK, GQA_KV_HEADS, gqa_k_norm[l], *rope_gqa, seg=GQA_DIM, scale=1.0, split=False)
        mix_g = _attention(gq, gk, proj[:, _GV:_MQ], n_kv=GQA_KV_HEADS,
                           n_q_heads=GQA_HEADS // GQA_KV_HEADS, n_lat=n_lat, n_ctx=n_ctx, with_ctx_queries=need_ctx,
                           tq=GQA_Q_TILE)

        qk = _mlstm_conv(proj, mlstm_conv_w[l], mlstm_conv_b[l], n_lat)
        vt_m = proj[:, _MV:_MO].T
        g_t = gates[:, :4 * MLSTM_HEADS].T.reshape(4, MLSTM_HEADS, t_all)
        g_t = jnp.pad(g_t, ((0, 0), (0, 16 - MLSTM_HEADS), (0, 0)))
        gb = jnp.pad(mlstm_gate_b[l].astype(F32), ((0, 0), (0, 16 - MLSTM_HEADS)))[:, :, None]
        h_f = _mlstm_direction(qk, vt_m, g_t[0], g_t[1], gb[0], gb[1], vis_fwd.T, n_lat=n_lat, reverse=False)
        h_b = _mlstm_direction(qk, vt_m, g_t[2], g_t[3], gb[2], gb[3], vis_bwd.T, n_lat=n_lat, reverse=True)
        mix_m = _mlstm_finish(h_f, h_b, proj, mlstm_out_norm[l], rows)

        kb = 512
        w_blocks = [(w_out, l, ai, off, kb, (row0 + off) // kb)
                    for ai, (row0, width) in enumerate(((0, 1024), (1024, 1536), (2560, 1536)))
                    for off in range(0, width, kb)]
        x1 = _matmul([mix_d, mix_g, mix_m], w_blocks, rows=rows, n=d, tm=_row_tile(rows), tn=256, kind="resid",
                     resid=(xa, mods, 2), n_lat=n_lat, name="out_proj")

        if l % 2 == 0:
            tok = _modnorm(x1, ffn_norm[l], mods, 3, n_lat)
            d_ff = ffn_w_gate.shape[-1]
            hid = _matmul([tok], [(ffn_w_gate, l // 2, 0, 0, d, 0), (ffn_w_up, l // 2, 0, 0, d, 0)], rows=rows,
                          n=d_ff, tm=_row_tile(rows), tn=256, kind="swiglu", out_dtype=BF16, name="ffn_up")
            w_down = ffn_w_down[l // 2].astype(BF16)[None]
            xa = _matmul([hid], [(w_down, 0, 0, 0, d_ff, 0)], rows=rows, n=d, tm=_row_tile(rows, (768, 512, 256)),
                         tn=256, kind="resid", resid=(x1, mods, 5), n_lat=n_lat, name="ffn_down")
        else:
            assert not need_ctx
            rw = jnp.pad(router_w[l // 2].astype(F32), ((0, 0), (0, LANES - N_EXPERTS)))
            rb = jnp.pad(router_b[l // 2].astype(F32), (0, LANES - N_EXPERTS)).reshape(1, LANES)
            tok, logits = _modnorm(x1, ffn_norm[l], mods, 3, n_lat, router=(rw, rb), out_dtype=F32)
            d0, d1, w0, w1, slot_tok, block_e, n_valid, n_used = _moe_routing(logits[:, :N_EXPERTS], rows)
            hid = _moe_up(tok, exp_w_gate, exp_w_up, l // 2, block_e, n_valid, slot_tok, tn=256)
            ys = _moe_down(hid, exp_w_down, l // 2, block_e, n_valid, n_used, tn=512)
            xa = _moe_combine(ys, x1, d0, d1, w0, w1, mods, 5)
    return xa[:n_lat].reshape(batch, n_lat, d)
```

```python
import functools
import math

import jax
import jax.numpy as jnp
import numpy as np
from jax import lax
from jax.experimental import pallas as pl
from jax.experimental.pallas import tpu as pltpu

F32 = jnp.float32
BF16 = jnp.bfloat16
HIGHEST = lax.Precision.HIGHEST

GRID_W = 64
ROPE_THETA = 10000.0
NORM_EPS = 1e-6
LOG2E = math.log2(math.e)

DIFF_HEADS = 8
DIFF_QK_DIM = 64
GQA_HEADS = 12
GQA_KV_HEADS = 4
GQA_DIM = 128
MLSTM_HEADS = 12
MLSTM_DIM = 128
N_EXPERTS = 8
TOP_K = 2

LANES = 128
ROW_TILE = 256
NORM_ROWS = 16
ONES_ROWS = 16
DIFF_Q_TILE = 512
GQA_Q_TILE = 256
CONV_ROW_TILE = 768
MLSTM_CHUNK = 256
MOE_BLK = 1024
MOE_SLAB = 256
GATHER_ROWS = 256
VMEM_CAP = 60 * 1024 * 1024

_DQ, _DK, _DV = 0, 1024, 2048
_GQ, _GK, _GV = 3072, 4608, 5120
_MQ, _MK, _MV, _MO, _MG = 5632, 7168, 8704, 10240, 11776
_MAIN_WIDTH = 11776


def _cparams(sem, vmem_bytes):
    limit = int(min(VMEM_CAP, max(32 * 1024 * 1024, vmem_bytes + (8 << 20))))
    return pltpu.CompilerParams(dimension_semantics=sem, vmem_limit_bytes=limit)


def _row_tile(m, cands=(1408, 1024, 768, 512, 256)):
    for t in cands:
        if m % t == 0:
            return t
    raise ValueError(f"no row tile for {m}")


def _log_sigmoid(x):
    return jnp.minimum(x, 0.0) - jnp.log1p(jnp.exp(-jnp.abs(x)))


def _silu(x):
    return x * jax.nn.sigmoid(x)


def _mod_kernel(c_ref, w_ref, b_ref, o_ref):
    s = _silu(c_ref[...])
    o_ref[0] = jnp.dot(s, w_ref[0], preferred_element_type=F32) + b_ref[0]


def _mods(cc, w_mod, b_mod):
    n_layers, d, n6 = w_mod.shape
    tn = 512
    return pl.pallas_call(
        _mod_kernel,
        grid=(n_layers, n6 // tn),
        in_specs=[pl.BlockSpec((8, d), lambda l, j: (0, 0)),
                  pl.BlockSpec((1, d, tn), lambda l, j: (l, 0, j)),
                  pl.BlockSpec((1, 1, tn), lambda l, j: (l, 0, j))],
        out_specs=pl.BlockSpec((1, 8, tn), lambda l, j: (l, 0, j)),
        out_shape=jax.ShapeDtypeStruct((n_layers, 8, n6), F32),
        compiler_params=_cparams(("parallel", "parallel"), 2 * d * tn * 4),
        name="adaln_mods",
    )(cc, w_mod, b_mod.reshape(n_layers, 1, n6))


def _modnorm_kernel(*refs, n_lat, tm, with_router, split_input):
    n_x = 2 if split_input else 1
    x_refs, (g_ref, sh_ref, sc_ref), rest = refs[:n_x], refs[n_x:n_x + 3], refs[n_x + 3:]
    is_ctx_tile = pl.program_id(0) * tm >= n_lat
    mod_row = jnp.where(is_ctx_tile, 1, 0)
    gain = g_ref[...]
    scale1 = 1.0 + sc_ref[pl.ds(mod_row, 1), :]
    shift = sh_ref[pl.ds(mod_row, 1), :]
    o_ref = rest[2] if with_router else rest[0]

    def normalise(x_ref):
        def body(r, carry):
            rows = pl.ds(pl.multiple_of(r * NORM_ROWS, NORM_ROWS), NORM_ROWS)
            x = x_ref[rows, :]
            ms = jnp.mean(x * x, axis=-1, keepdims=True)
            o_ref[rows, :] = (x * lax.rsqrt(ms + NORM_EPS) * gain * scale1 + shift).astype(o_ref.dtype)
            if split_input:
                rest[1][rows, :] = x
            return carry

        lax.fori_loop(0, tm // NORM_ROWS, body, 0, unroll=4)

    if split_input:
        pl.when(is_ctx_tile)(lambda: normalise(x_refs[1]))
        pl.when(jnp.logical_not(is_ctx_tile))(lambda: normalise(x_refs[0]))
    else:
        normalise(x_refs[0])
    if with_router:
        rw_ref, rb_ref, _, lg_ref = rest
        h, w = o_ref[...], rw_ref[...]
        h_hi, w_hi = h.astype(BF16), w.astype(BF16)
        h_lo, w_lo = (h - h_hi.astype(F32)).astype(BF16), (w - w_hi.astype(F32)).astype(BF16)
        lg_ref[...] = (jnp.dot(h_hi, w_hi, preferred_element_type=F32) + jnp.dot(h_lo, w_hi, preferred_element_type=F32)
                       + jnp.dot(h_hi, w_lo, preferred_element_type=F32) + rb_ref[...])


def _modnorm(x, gain, mods, shift_idx, n_lat, router=None, out_dtype=BF16, ctx_rows=None):
    d = x.shape[1]
    tm = ROW_TILE
    m = x.shape[0] + (0 if ctx_rows is None else ctx_rows.shape[0])
    if ctx_rows is None:
        in_specs, args = [pl.BlockSpec((tm, d), lambda i: (i, 0))], [x]
    else:
        assert router is None and x.shape[0] == n_lat and ctx_rows.shape[0] == tm
        last = n_lat // tm - 1
        in_specs = [pl.BlockSpec((tm, d), lambda i: (jnp.minimum(i, last), 0)), pl.BlockSpec((tm, d), lambda i: (0, 0))]
        args = [x, ctx_rows]
    in_specs += [pl.BlockSpec((1, d), lambda i: (0, 0)),
                 pl.BlockSpec((8, d), lambda i: (0, shift_idx)),
                 pl.BlockSpec((8, d), lambda i: (0, shift_idx + 1))]
    args += [gain.reshape(1, d), mods, mods]
    out_specs = pl.BlockSpec((tm, d), lambda i: (i, 0))
    out_shape = jax.ShapeDtypeStruct((m, d), out_dtype)
    if router is not None:
        rw, rb = router
        in_specs += [pl.BlockSpec((d, LANES), lambda i: (0, 0)),
                     pl.BlockSpec((1, LANES), lambda i: (0, 0))]
        args += [rw, rb]
        out_specs = [out_specs, pl.BlockSpec((tm, LANES), lambda i: (i, 0))]
        out_shape = [out_shape, jax.ShapeDtypeStruct((m, LANES), F32)]
    elif ctx_rows is not None:
        out_specs = [out_specs, pl.BlockSpec((tm, d), lambda i: (i, 0))]
        out_shape = [out_shape, jax.ShapeDtypeStruct((m, d), x.dtype)]
    return pl.pallas_call(
        functools.partial(_modnorm_kernel, n_lat=n_lat, tm=tm, with_router=router is not None,
                          split_input=ctx_rows is not None),
        grid=(m // tm,),
        in_specs=in_specs, out_specs=out_specs, out_shape=out_shape,
        compiler_params=_cparams(("parallel",), 2 * tm * d * 6 + 6 * tm * d * 4 + d * LANES * 8),
        name="modnorm",
    )(*args)


def _mxu_operand(w):
    return w if w.dtype == BF16 else w.astype(BF16)


def _mm_kernel(*refs, a_of_w, kind, n_lat, tm, nt=False, valid_cols=None):
    n_a = 1 + max(ai for ai, _ in a_of_w)
    n_w = len(a_of_w)
    a_refs, w_refs, rest = refs[:n_a], refs[n_a:n_a + n_w], refs[n_a + n_w:]
    col_axis, k_axis = (0, 1) if nt else (1, 0)

    def lhs(i):
        ai, off = a_of_w[i]
        return a_refs[ai][:, off:off + w_refs[i].shape[k_axis]]

    def rhs(i):
        w = w_refs[i][...]
        if valid_cols is not None:
            w = jnp.where(lax.broadcasted_iota(jnp.int32, w.shape, col_axis) < valid_cols, w, 0.0)
        return _mxu_operand(w)

    def mm(i):
        return lax.dot_general(lhs(i), rhs(i), (((1,), (k_axis,)), ((), ())), preferred_element_type=F32)

    if kind == "swiglu":
        g, u = mm(0), mm(1)
        (o_ref,) = rest
        o_ref[...] = (_silu(g) * u).astype(o_ref.dtype)
        return
    acc = mm(0)
    for i in range(1, n_w):
        acc = acc + mm(i)
    if kind == "plain":
        (o_ref,) = rest
        o_ref[...] = acc.astype(o_ref.dtype)
    else:
        x_ref, g_ref, o_ref = rest
        rows = pl.program_id(0) * tm + lax.broadcasted_iota(jnp.int32, (tm, 1), 0)
        gate = jnp.where(rows >= n_lat, g_ref[1:2, :], g_ref[0:1, :])
        o_ref[...] = x_ref[...] + gate * acc


def _matmul(a_list, w_list, *, rows, n, tm, tn, col0=0, kind="plain", out_dtype=F32, resid=None, n_lat=0,
            nt=False, valid_cols=None, name="matmul"):
    assert col0 % tn == 0
    cb = col0 // tn
    in_specs = [pl.BlockSpec((tm, a.shape[1]), lambda i, j: (i, 0)) for a in a_list]
    args = list(a_list)
    vmem = sum(2 * tm * a.shape[1] * 2 for a in a_list) + 6 * tm * tn * 4
    for w3d, lead, _, _, k_rows, k_blk in w_list:
        if nt:
            in_specs.append(pl.BlockSpec((None, tn, k_rows), lambda i, j, lead=lead, k_blk=k_blk: (lead, cb + j, k_blk)))
        else:
            in_specs.append(pl.BlockSpec((None, k_rows, tn), lambda i, j, lead=lead, k_blk=k_blk: (lead, k_blk, cb + j)))
        args.append(w3d)
        vmem += k_rows * tn * (2 * w3d.dtype.itemsize + 2)
    if kind == "resid":
        x, mods, gate_idx = resid
        d = x.shape[1]
        in_specs += [pl.BlockSpec((tm, tn), lambda i, j: (i, j)),
                     pl.BlockSpec((8, tn), lambda i, j: (0, gate_idx * (d // tn) + j))]
        args += [x, mods]
    a_of_w = tuple((a_idx, a_off) for _, _, a_idx, a_off, _, _ in w_list)
    return pl.pallas_call(
        functools.partial(_mm_kernel, a_of_w=a_of_w, kind=kind, n_lat=n_lat, tm=tm, nt=nt, valid_cols=valid_cols),
        grid=(rows // tm, n // tn),
        in_specs=in_specs,
        out_specs=pl.BlockSpec((tm, tn), lambda i, j: (i, j)),
        out_shape=jax.ShapeDtypeStruct((rows, n), out_dtype),
        compiler_params=_cparams(("parallel", "arbitrary"), vmem),
        name=name,
    )(*args)


def _qkprep_kernel(x_ref, g_ref, c_ref, s_ref, o_ref, *, n_tiles, seg, scale, split):
    lane = lax.broadcasted_iota(jnp.int32, c_ref.shape, 1)
    gain, cos, sin = g_ref[...], c_ref[...], s_ref[...]
    half = seg // 2
    ow = 2 * LANES if split else LANES
    for j in range(n_tiles):
        x = x_ref[:, j * LANES:(j + 1) * LANES].astype(F32)
        x2 = x * x
        if seg == LANES:
            ms = jnp.sum(x2, axis=-1, keepdims=True) * (1.0 / seg)
        else:
            lo = lane < seg
            s_lo = jnp.sum(jnp.where(lo, x2, 0.0), axis=-1, keepdims=True)
            s_hi = jnp.sum(jnp.where(lo, 0.0, x2), axis=-1, keepdims=True)
            ms = jnp.where(lo, s_lo, s_hi) * (1.0 / seg)
        y = x * lax.rsqrt(ms + NORM_EPS) * gain
        if seg == LANES:
            partner = pltpu.roll(y, half, 1)
        else:
            partner = jnp.where((lane % seg) < half, pltpu.roll(y, LANES - half, 1), pltpu.roll(y, half, 1))
        y = (y * cos + partner * sin) * scale
        if split:
            o_ref[:, j * ow:j * ow + LANES] = jnp.where(lane < seg, y, 0.0).astype(o_ref.dtype)
            o_ref[:, j * ow + LANES:(j + 1) * ow] = jnp.where(lane < seg, 0.0, y).astype(o_ref.dtype)
        else:
            o_ref[:, j * ow:(j + 1) * ow] = y.astype(o_ref.dtype)


def _qkprep(proj, col0, n_tiles, gain, rope_c, rope_s, *, seg, scale, split):
    t = proj.shape[0]
    tm = ROW_TILE
    width = n_tiles * LANES
    assert col0 % width == 0
    cb = col0 // width
    g = jnp.tile(gain.astype(F32), LANES // seg).reshape(1, LANES)
    ow = (2 if split else 1) * width
    return pl.pallas_call(
        functools.partial(_qkprep_kernel, n_tiles=n_tiles, seg=seg, scale=scale, split=split),
        grid=(t // tm,),
        in_specs=[pl.BlockSpec((tm, width), lambda i: (i, cb)),
                  pl.BlockSpec((1, LANES), lambda i: (0, 0)),
                  pl.BlockSpec((tm, LANES), lambda i: (i, 0)),
                  pl.BlockSpec((tm, LANES), lambda i: (i, 0))],
        out_specs=pl.BlockSpec((tm, ow), lambda i: (i, 0)),
        out_shape=jax.ShapeDtypeStruct((t, ow), BF16),
        compiler_params=_cparams(("parallel",), 0),
        name="qk_prep",
    )(proj, g, rope_c, rope_s)


def _attn_kernel(q_ref, k_ref, vt_ref, *rest, n_q_heads, tq, tk, lam_init):
    n_keys = k_ref.shape[0]
    qs = [q_ref[:, g * LANES:(g + 1) * LANES] for g in range(n_q_heads)]

    def scores(kc):
        return tuple(lax.dot_general(kc, q, (((1,), (1,)), ((), ())), preferred_element_type=F32) for q in qs)

    def update(ss, vtc, carry):
        new = []
        for s, (m, acc) in zip(ss, carry):
            m_new = jnp.maximum(m, jnp.max(s, axis=0, keepdims=True))
            p = jnp.exp2(s - m_new).astype(BF16)
            new.append((m_new, jnp.exp2(m - m_new) * acc + jnp.dot(vtc, p, preferred_element_type=F32)))
        return tuple(new)

    carry = tuple((jnp.full((1, tq), -jnp.inf, F32), jnp.zeros((vt_ref.shape[0], tq), F32))
                  for _ in range(n_q_heads))
    ss = scores(k_ref[0:tk, :])
    for c in range(n_keys // tk):
        ss_next = scores(k_ref[(c + 1) * tk:(c + 2) * tk, :]) if (c + 1) * tk < n_keys else None
        carry = update(ss, vt_ref[:, c * tk:(c + 1) * tk], carry)
        ss = ss_next
    outs_t = [acc[:LANES] / acc[LANES:LANES + 1] for _, acc in carry]
    if lam_init is None:
        (o_ref,) = rest
        for g in range(n_q_heads):
            o_ref[:, g * LANES:(g + 1) * LANES] = outs_t[g].T.astype(o_ref.dtype)
    else:
        lam_ref, g_ref, o_ref = rest
        lp = lam_ref[...]
        lam = (jnp.exp(jnp.sum(lp[0:1, :] * lp[1:2, :], axis=-1, keepdims=True))
               - jnp.exp(jnp.sum(lp[2:3, :] * lp[3:4, :], axis=-1, keepdims=True)) + lam_init)
        o = (outs_t[0] - lam * outs_t[1]).T
        ms = jnp.mean(o * o, axis=-1, keepdims=True)
        o_ref[...] = (o * lax.rsqrt(ms + NORM_EPS) * g_ref[...] * (1.0 - lam_init)).astype(o_ref.dtype)


def _attention(q, k, v, *, n_kv, n_q_heads, n_lat, n_ctx, with_ctx_queries, tq, diff=None):
    t = k.shape[0]
    qw = n_q_heads * LANES
    ow = LANES if diff is not None else qw
    hv = LANES + ONES_ROWS
    vt = jnp.concatenate([v.T.reshape(n_kv, LANES, t), jnp.ones((n_kv, ONES_ROWS, t), v.dtype)], axis=1)
    vt = vt.reshape(n_kv * hv, t)
    extra_specs, extra_args, lam_init = [], [], None
    if diff is not None:
        lam_p, gain, lam_init = diff
        extra_specs = [pl.BlockSpec((4, DIFF_QK_DIM), lambda h, i: (0, 0)), pl.BlockSpec((1, LANES), lambda h, i: (0, 0))]
        extra_args = [lam_p.astype(F32), gain.astype(F32).reshape(1, LANES)]

    def call(q_rows, q_row0, n_keys, key0, tq, tk):
        qb0, kb0 = q_row0 // tq, key0 // n_keys
        vmem = 2 * 2 * n_keys * LANES * 2 + 4 * tq * qw * 4 + 8 * n_q_heads * tq * tk * 4
        return pl.pallas_call(
            functools.partial(_attn_kernel, n_q_heads=n_q_heads, tq=tq, tk=tk, lam_init=lam_init),
            grid=(n_kv, q_rows // tq),
            in_specs=[pl.BlockSpec((tq, qw), lambda h, i: (qb0 + i, h)),
                      pl.BlockSpec((n_keys, LANES), lambda h, i: (kb0, h)),
                      pl.BlockSpec((hv, n_keys), lambda h, i: (h, kb0))] + extra_specs,
            out_specs=pl.BlockSpec((tq, ow), lambda h, i: (i, h)),
            out_shape=jax.ShapeDtypeStruct((q_rows, n_kv * ow), BF16),
            compiler_params=_cparams(("parallel", "arbitrary"), vmem),
            name="attention",
        )(q, k, vt, *extra_args)

    tq = math.gcd(tq, n_lat)
    out = call(n_lat, 0, t, 0, tq, next(c for c in (768, 512, 256) if t % c == 0))
    if with_ctx_queries:
        assert n_lat % n_ctx == 0
        out = jnp.concatenate([out, call(n_ctx, n_lat, n_ctx, n_lat, n_ctx, n_ctx)], axis=0)
    return out


def _conv_kernel(x_ref, xp_ref, xn_ref, w_ref, b_ref, o_ref, *, n_lat, t_all, tm, k_col_tile):
    i, j = pl.program_id(0), pl.program_id(1)
    x = x_ref[...].astype(F32)
    row = lax.broadcasted_iota(jnp.int32, (tm, 1), 0)
    grow = i * tm + row
    prev = jnp.where(row == 0, xp_ref[15:16, :].astype(F32), pltpu.roll(x, 1, 0))
    nxt = jnp.where(row == tm - 1, xn_ref[0:1, :].astype(F32), pltpu.roll(x, tm - 1, 0))
    prev = jnp.where((grow == 0) | (grow == n_lat), 0.0, prev)
    nxt = jnp.where((grow == n_lat - 1) | (grow == t_all - 1), 0.0, nxt)
    y = b_ref[...] + prev * w_ref[0:1, :] + x * w_ref[1:2, :] + nxt * w_ref[2:3, :]
    y = _silu(y) * jnp.where(j >= k_col_tile, MLSTM_DIM ** -0.5, 1.0)
    o_ref[...] = y.astype(o_ref.dtype)


def _mlstm_conv(proj, conv_w, conv_b, n_lat):
    t = proj.shape[0]
    tm, tc, hb = (CONV_ROW_TILE if t % CONV_ROW_TILE == 0 else ROW_TILE), 512, 16
    width = 2 * MLSTM_HEADS * MLSTM_DIM
    cb = _MQ // tc
    nrb = t // hb
    return pl.pallas_call(
        functools.partial(_conv_kernel, n_lat=n_lat, t_all=t, tm=tm, k_col_tile=(width // 2) // tc),
        grid=(t // tm, width // tc),
        in_specs=[pl.BlockSpec((tm, tc), lambda i, j: (i, cb + j)),
                  pl.BlockSpec((hb, tc), lambda i, j: (jnp.maximum(i * (tm // hb) - 1, 0), cb + j)),
                  pl.BlockSpec((hb, tc), lambda i, j: (jnp.minimum((i + 1) * (tm // hb), nrb - 1), cb + j)),
                  pl.BlockSpec((3, tc), lambda i, j: (0, j)),
                  pl.BlockSpec((1, tc), lambda i, j: (0, j))],
        out_specs=pl.BlockSpec((tm, tc), lambda i, j: (i, j)),
        out_shape=jax.ShapeDtypeStruct((t, width), BF16),
        compiler_params=_cparams(("parallel", "arbitrary"), 0),
        name="mlstm_conv",
    )(proj, proj, proj, conv_w.astype(F32), conv_b.astype(F32).reshape(1, width))


def _mlstm_kernel(q_ref, k_ref, vt_ref, gi_ref, gf_ref, bi_ref, bf_ref, mask_ref, ht_ref, ct_sc, n_sc, m_sc, *, n_heads):
    @pl.when(pl.program_id(0) == 0)
    def _():
        ct_sc[...] = jnp.zeros_like(ct_sc)
        n_sc[...] = jnp.zeros_like(n_sc)
        m_sc[...] = jnp.zeros_like(m_sc)

    vis_t = mask_ref[...]
    li_all = gi_ref[...] + bi_ref[...]
    lf_all = _log_sigmoid(gf_ref[...] + bf_ref[...])
    br_all = jnp.dot(lf_all, vis_t, preferred_element_type=F32, precision=HIGHEST)
    bend_all = jnp.sum(lf_all, axis=1, keepdims=True)
    c_cols = (br_all - li_all).T
    d = MLSTM_DIM
    nt = (((1,), (1,)), ((), ()))
    for h in range(n_heads):
        lir, br = li_all[h:h + 1, :], br_all[h:h + 1, :]
        b_end = bend_all[h:h + 1, :]
        m_prev = m_sc[h:h + 1, 0:1]
        lw = b_end - br + lir
        m_new = jnp.maximum(b_end + m_prev, jnp.max(lw, axis=1, keepdims=True))
        w_end = jnp.exp(lw - m_new)
        decay = jnp.exp(b_end + m_prev - m_new)
        log_d = jnp.where(vis_t > 0.0, br - c_cols[:, h:h + 1], -jnp.inf)
        m_t = jnp.maximum(br + m_prev, jnp.max(log_d, axis=0, keepdims=True))
        dmat = jnp.exp(log_d - m_t)
        inter = jnp.exp(br + m_prev - m_t)
        qh = q_ref[:, h * d:(h + 1) * d]
        kh = k_ref[:, h * d:(h + 1) * d]
        vth = vt_ref[h * d:(h + 1) * d, :]
        ct_prev = ct_sc[h]
        n_prev = n_sc[h:h + 1, :]
        sc = lax.dot_general(kh, qh, nt, preferred_element_type=F32) * dmat
        num = (jnp.dot(vth, sc.astype(BF16), preferred_element_type=F32)
               + inter * lax.dot_general(ct_prev.astype(BF16), qh, nt, preferred_element_type=F32))
        n8 = jnp.broadcast_to(n_prev, (8, d)).astype(BF16)
        den = (jnp.sum(sc, axis=0, keepdims=True)
               + inter * lax.dot_general(n8, qh, nt, preferred_element_type=F32)[0:1, :])
        ht_ref[h * d:(h + 1) * d, :] = num / jnp.maximum(jnp.abs(den), jnp.exp(-m_t))
        vtw = (vth.astype(F32) * w_end).astype(BF16)
        ct_sc[h] = decay * ct_prev + jnp.dot(vtw, kh, preferred_element_type=F32)
        w8 = jnp.broadcast_to(w_end, (8, w_end.shape[1])).astype(BF16)
        n_sc[h:h + 1, :] = decay * n_prev + jnp.dot(w8, kh, preferred_element_type=F32)[0:1, :]
        m_sc[h:h + 1, :] = jnp.broadcast_to(m_new, (1, LANES))


def _mlstm_direction(qk, vt, gi, gf, bi, bf, vis_t, *, n_lat, reverse):
    t = qk.shape[0]
    lc = MLSTM_CHUNK
    n_chunks = t // lc
    ctx_blk = n_lat // lc
    hd = MLSTM_HEADS * MLSTM_DIM

    def blk(c):
        lat = (n_chunks - 1 - c) if reverse else (c - 1)
        return jnp.where(c == 0, ctx_blk, lat)

    return pl.pallas_call(
        functools.partial(_mlstm_kernel, n_heads=MLSTM_HEADS),
        grid=(n_chunks,),
        in_specs=[pl.BlockSpec((lc, hd), lambda c: (blk(c), 0)),
                  pl.BlockSpec((lc, hd), lambda c: (blk(c), 1)),
                  pl.BlockSpec((hd, lc), lambda c: (0, blk(c))),
                  pl.BlockSpec((16, lc), lambda c: (0, blk(c))),
                  pl.BlockSpec((16, lc), lambda c: (0, blk(c))),
                  pl.BlockSpec((16, 1), lambda c: (0, 0)),
                  pl.BlockSpec((16, 1), lambda c: (0, 0)),
                  pl.BlockSpec((lc, lc), lambda c: (0, 0))],
        out_specs=pl.BlockSpec((hd, lc), lambda c: (0, blk(c))),
        out_shape=jax.ShapeDtypeStruct((hd, t), F32),
        scratch_shapes=[pltpu.VMEM((MLSTM_HEADS, MLSTM_DIM, MLSTM_DIM), F32),
                        pltpu.VMEM((16, LANES), F32),
                        pltpu.VMEM((16, LANES), F32)],
        compiler_params=_cparams(("arbitrary",), 16 << 20),
        name="mlstm_bwd" if reverse else "mlstm_fwd",
    )(qk, qk, vt, gi, gf, bi, bf, vis_t)


def _mlstm_finish_kernel(hf_ref, hb_ref, o0_ref, o1_ref, o2_ref, g_ref, out_ref, *, n_heads):
    d = MLSTM_DIM
    o_refs = (o0_ref, o1_ref, o2_ref)
    per_blk = o0_ref.shape[1] // d
    for h in range(n_heads):
        sl = slice(h * d, (h + 1) * d)
        s = hf_ref[sl, :] + hb_ref[sl, :]
        ms = jnp.mean(s * s, axis=0, keepdims=True)
        y = (s * lax.rsqrt(ms + NORM_EPS)).T * g_ref[:, sl]
        o = o_refs[h // per_blk][:, (h % per_blk) * d:(h % per_blk + 1) * d]
        out_ref[:, sl] = (jax.nn.sigmoid(o.astype(F32)) * y).astype(out_ref.dtype)


def _mlstm_finish(hf_t, hb_t, proj, out_gain, rows):
    hd = MLSTM_HEADS * MLSTM_DIM
    tm = ROW_TILE
    ow = hd // 3
    assert _MO % ow == 0
    return pl.pallas_call(
        functools.partial(_mlstm_finish_kernel, n_heads=MLSTM_HEADS),
        grid=(rows // tm,),
        in_specs=[pl.BlockSpec((hd, tm), lambda i: (0, i)),
                  pl.BlockSpec((hd, tm), lambda i: (0, i)),
                  pl.BlockSpec((tm, ow), lambda i: (i, _MO // ow)),
                  pl.BlockSpec((tm, ow), lambda i: (i, _MO // ow + 1)),
                  pl.BlockSpec((tm, ow), lambda i: (i, _MO // ow + 2)),
                  pl.BlockSpec((1, hd), lambda i: (0, 0))],
        out_specs=pl.BlockSpec((tm, hd), lambda i: (i, 0)),
        out_shape=jax.ShapeDtypeStruct((rows, hd), BF16),
        compiler_params=_cparams(("parallel",), 0),
        name="mlstm_finish",
    )(hf_t, hb_t, proj, proj, proj, out_gain.astype(F32).reshape(1, hd))


def _moe_up_kernel(be_ref, nv_ref, st_ref, tok_ref, wg_ref, wu_ref, o_ref, xbuf, xb, sem, *, nj):
    b, j = pl.program_id(0), pl.program_id(1)
    nb = pl.num_programs(0)
    n_valid = nv_ref[b]
    per_step = MOE_BLK // nj

    def copy(blk, r):
        src = tok_ref.at[pl.ds(st_ref[blk * MOE_BLK + r], 1), :]
        return pltpu.make_async_copy(src, xbuf.at[pl.ds(r, 1), :], sem)

    @pl.when((b == 0) & (j == 0))
    def _():
        def body(r, carry):
            copy(0, r).start()
            return carry
        lax.fori_loop(0, MOE_BLK, body, 0)

    was_requested = (b == 0) | (nv_ref[jnp.maximum(b - 1, 0)] > 0)

    @pl.when((j == 0) & was_requested)
    def _():
        pltpu.make_async_copy(tok_ref.at[pl.ds(0, MOE_BLK), :], xbuf, sem).wait()
        xb[...] = xbuf[...].astype(BF16)

    def request_next():
        nxt = jnp.minimum(b + 1, nb - 1)
        for r in range(per_step):
            copy(nxt, j * per_step + r).start()

    def swiglu(sl):
        a = xb[sl, :]
        g = jnp.dot(a, _mxu_operand(wg_ref[...]), preferred_element_type=F32)
        u = jnp.dot(a, _mxu_operand(wu_ref[...]), preferred_element_type=F32)
        o_ref[sl, :] = (_silu(g) * u).astype(o_ref.dtype)

    @pl.when(n_valid == MOE_BLK)
    def _():
        request_next()
        swiglu(slice(None))

    @pl.when((n_valid > 0) & (n_valid < MOE_BLK))
    def _():
        request_next()
        for s in range(MOE_BLK // MOE_SLAB):
            sl = slice(s * MOE_SLAB, (s + 1) * MOE_SLAB)

            @pl.when(n_valid > s * MOE_SLAB)
            def _():
                swiglu(sl)

            @pl.when(n_valid <= s * MOE_SLAB)
            def _():
                o_ref[sl, :] = jnp.zeros((MOE_SLAB, o_ref.shape[1]), o_ref.dtype)

    @pl.when(n_valid == 0)
    def _():
        o_ref[...] = jnp.zeros_like(o_ref)

    @pl.when((b == nb - 1) & (j == nj - 1) & (n_valid > 0))
    def _():
        pltpu.make_async_copy(tok_ref.at[pl.ds(0, MOE_BLK), :], xbuf, sem).wait()


def _moe_up(tok, w_gate, w_up, lead, block_e, n_valid, slot_tok, *, tn):
    n_slots = slot_tok.shape[0]
    k = tok.shape[1]
    n = w_gate.shape[-1]
    nb, nj = n_slots // MOE_BLK, n // tn
    assert MOE_BLK % nj == 0

    def w_map(b, j, be, nv, st):
        return (lead, be[b], 0, jnp.where(nv[b] > 0, j, nj - 1))

    return pl.pallas_call(
        functools.partial(_moe_up_kernel, nj=nj),
        grid_spec=pltpu.PrefetchScalarGridSpec(
            num_scalar_prefetch=3,
            grid=(nb, nj),
            in_specs=[pl.BlockSpec(memory_space=pl.ANY),
                      pl.BlockSpec((None, None, k, tn), w_map),
                      pl.BlockSpec((None, None, k, tn), w_map)],
            out_specs=pl.BlockSpec((MOE_BLK, tn), lambda b, j, be, nv, st: (b, j)),
            scratch_shapes=[pltpu.VMEM((MOE_BLK, k), F32),
                            pltpu.VMEM((MOE_BLK, k), BF16),
                            pltpu.SemaphoreType.DMA(())]),
        out_shape=jax.ShapeDtypeStruct((n_slots, n), BF16),
        compiler_params=_cparams(("arbitrary", "arbitrary"),
                                 MOE_BLK * k * 6 + 2 * k * tn * (2 * w_gate.dtype.itemsize + 2)
                                 + 4 * MOE_BLK * tn * 4),
        name="moe_up",
    )(block_e, n_valid, slot_tok, tok, w_gate, w_up)


def _moe_down_kernel(be_ref, nv_ref, nu_ref, a_ref, w_ref, o_ref):
    n_valid = nv_ref[pl.program_id(0)]

    def down(sl):
        o_ref[sl, :] = jnp.dot(a_ref[sl, :], _mxu_operand(w_ref[...]), preferred_element_type=F32)

    @pl.when(n_valid == MOE_BLK)
    def _():
        down(slice(None))

    @pl.when((n_valid > 0) & (n_valid < MOE_BLK))
    def _():
        for s in range(MOE_BLK // MOE_SLAB):
            sl = slice(s * MOE_SLAB, (s + 1) * MOE_SLAB)

            @pl.when(n_valid > s * MOE_SLAB)
            def _():
                down(sl)

            @pl.when(n_valid <= s * MOE_SLAB)
            def _():
                o_ref[sl, :] = jnp.zeros((MOE_SLAB, o_ref.shape[1]), o_ref.dtype)

    @pl.when(n_valid == 0)
    def _():
        o_ref[...] = jnp.zeros_like(o_ref)


def _moe_down(a, w, lead, block_e, n_valid, n_used, *, tn):
    n_slots, k = a.shape
    n = w.shape[-1]
    nb, nj = n_slots // MOE_BLK, n // tn

    def a_map(b, j, be, nv, nu):
        return (jnp.minimum(b, nu[0] - 1), 0)

    def w_map(b, j, be, nv, nu):
        return (lead, be[b], 0, jnp.where(nv[b] > 0, j, nj - 1))

    return pl.pallas_call(
        _moe_down_kernel,
        grid_spec=pltpu.PrefetchScalarGridSpec(
            num_scalar_prefetch=3,
            grid=(nb, nj),
            in_specs=[pl.BlockSpec((MOE_BLK, k), a_map), pl.BlockSpec((None, None, k, tn), w_map)],
            out_specs=pl.BlockSpec((MOE_BLK, tn), lambda b, j, be, nv, nu: (b, j))),
        out_shape=jax.ShapeDtypeStruct((n_slots, n), F32),
        compiler_params=_cparams(("arbitrary", "arbitrary"),
                                 2 * MOE_BLK * k * 2 + k * tn * (2 * w.dtype.itemsize + 2) + 4 * MOE_BLK * tn * 4),
        name="moe_down",
    )(block_e, n_valid, n_used, a, w)


def _combine_kernel(d0_ref, d1_ref, y_ref, x_ref, w0_ref, w1_ref, g_ref, o_ref, buf, sem, *, rows):
    i = pl.program_id(0)
    n = pl.num_programs(0)
    slot = i % 2

    def gather(step, slot_):
        base = step * rows

        def body(r, carry):
            for which, d_ref in enumerate((d0_ref, d1_ref)):
                pltpu.make_async_copy(y_ref.at[pl.ds(d_ref[base + r], 1), :],
                                      buf.at[slot_, which, pl.ds(r, 1), :], sem.at[slot_]).start()
            return carry

        lax.fori_loop(0, rows, body, 0, unroll=8)

    @pl.when(i == 0)
    def _():
        gather(0, 0)

    @pl.when(i + 1 < n)
    def _():
        gather(i + 1, 1 - slot)

    for which in range(2):
        pltpu.make_async_copy(y_ref.at[pl.ds(0, rows), :], buf.at[slot, which], sem.at[slot]).wait()
    y = w0_ref[...] * buf[slot, 0] + w1_ref[...] * buf[slot, 1]
    o_ref[...] = x_ref[...] + g_ref[0:1, :] * y


def _moe_combine(y_slots, x, d0, d1, w0, w1, mods, gate_idx):
    n_tok, d = x.shape
    rows = GATHER_ROWS // 2
    return pl.pallas_call(
        functools.partial(_combine_kernel, rows=rows),
        grid_spec=pltpu.PrefetchScalarGridSpec(
            num_scalar_prefetch=2,
            grid=(n_tok // rows,),
            in_specs=[pl.BlockSpec(memory_space=pl.ANY),
                      pl.BlockSpec((rows, d), lambda i, a, b: (i, 0)),
                      pl.BlockSpec((rows, 1), lambda i, a, b: (i, 0)),
                      pl.BlockSpec((rows, 1), lambda i, a, b: (i, 0)),
                      pl.BlockSpec((8, d), lambda i, a, b: (0, gate_idx))],
            out_specs=pl.BlockSpec((rows, d), lambda i, a, b: (i, 0)),
            scratch_shapes=[pltpu.VMEM((2, 2, rows, d), F32), pltpu.SemaphoreType.DMA((2,))]),
        out_shape=jax.ShapeDtypeStruct((n_tok, d), F32),
        compiler_params=_cparams(("arbitrary",), 12 * rows * d * 4),
        name="moe_combine",
    )(d0, d1, y_slots, x, w0, w1, mods)


def _moe_routing(logits, n_tok):
    top_logit, top_idx = lax.top_k(logits, TOP_K)
    top_w = jax.nn.softmax(top_logit, axis=-1)
    flat_e = top_idx.reshape(-1).astype(jnp.int32)
    n_assign = n_tok * TOP_K
    onehot = (flat_e[:, None] == jnp.arange(N_EXPERTS, dtype=jnp.int32)[None, :]).astype(jnp.int32)
    csum = jnp.cumsum(onehot, axis=0)
    rank = jnp.take_along_axis(csum, flat_e[:, None], axis=1)[:, 0] - 1
    counts = csum[-1]
    padded = (counts + MOE_BLK - 1) // MOE_BLK * MOE_BLK
    pad_ends = jnp.cumsum(padded)
    pad_starts = pad_ends - padded
    dest = (pad_starts[flat_e] + rank).astype(jnp.int32)
    n_blocks = -(-(n_assign + N_EXPERTS * (MOE_BLK - 1)) // MOE_BLK)
    n_slots = n_blocks * MOE_BLK
    slot_tok = jnp.zeros((n_slots,), jnp.int32).at[dest].set(jnp.arange(n_assign, dtype=jnp.int32) // TOP_K)
    blk_start = jnp.arange(n_blocks, dtype=jnp.int32) * MOE_BLK
    block_e = jnp.minimum(jnp.searchsorted(pad_ends, blk_start, side="right"), N_EXPERTS - 1).astype(jnp.int32)
    slab_ends = pad_starts + (counts + MOE_SLAB - 1) // MOE_SLAB * MOE_SLAB
    n_valid = jnp.clip(slab_ends[block_e] - blk_start, 0, MOE_BLK).astype(jnp.int32)
    n_used = (pad_ends[-1] // MOE_BLK).astype(jnp.int32).reshape(1)
    dest2 = dest.reshape(n_tok, TOP_K)
    return dest2[:, 0], dest2[:, 1], top_w[:, 0:1], top_w[:, 1:2], slot_tok, block_e, n_valid, n_used


def _rope_tables(n_lat, n_ctx, dim):
    rows = n_lat // GRID_W
    row = np.repeat(np.arange(rows, dtype=np.float32), GRID_W)
    col = np.tile(np.arange(GRID_W, dtype=np.float32), rows)
    quarter = dim // 4
    inv = (np.float32(ROPE_THETA) ** (-np.arange(quarter, dtype=np.float32) / np.float32(quarter))).astype(np.float32)
    ang = np.concatenate([row[:, None] * inv, col[:, None] * inv], axis=-1).astype(np.float32)
    cos, sin = np.cos(ang), np.sin(ang)
    reps = LANES // dim
    c = np.tile(np.concatenate([cos, cos], axis=-1), (1, reps))
    s = np.tile(np.concatenate([-sin, sin], axis=-1), (1, reps))
    c = np.concatenate([c, np.ones((n_ctx, LANES), np.float32)], axis=0)
    s = np.concatenate([s, np.zeros((n_ctx, LANES), np.float32)], axis=0)
    return jnp.asarray(c, F32), jnp.asarray(s, F32)


def kernel(x, c, ctx, c_ctx, w_mod, b_mod, attn_norm, ffn_norm, w_in, w_out, diff_q_norm, diff_k_norm, diff_lambda, diff_out_norm, gqa_q_norm, gqa_k_norm, mlstm_conv_w, mlstm_conv_b, mlstm_gate_b, mlstm_out_norm, ffn_w_gate, ffn_w_up, ffn_w_down, router_w, router_b, exp_w_gate, exp_w_up, exp_w_down):
    batch, n_lat, d = x.shape
    n_ctx = ctx.shape[1]
    depth = w_in.shape[0]
    assert batch == 1 and n_ctx == ROW_TILE and n_lat % MLSTM_CHUNK == 0 and n_lat % GRID_W == 0
    t_all = n_lat + n_ctx
    hd = MLSTM_HEADS * MLSTM_DIM

    cc = jnp.zeros((8, d), F32).at[0].set(c[0]).at[1].set(c_ctx)
    mods_all = _mods(cc, w_mod, b_mod)
    rope_diff = _rope_tables(n_lat, n_ctx, DIFF_QK_DIM)
    rope_gqa = _rope_tables(n_lat, n_ctx, GQA_DIM)
    pos = jnp.arange(MLSTM_CHUNK)
    vis_fwd = (pos[None, :] <= pos[:, None]).astype(F32)
    vis_bwd = (pos[None, :] >= pos[:, None]).astype(F32)

    w_in_t = jnp.swapaxes(w_in, 1, 2)
    xa = None
    for l in range(depth):
        need_ctx = l < depth - 1
        rows = t_all if need_ctx else n_lat
        lam_init = 0.8 - 0.6 * math.exp(-0.3 * l)
        mods = mods_all[l]

        if l == 0:
            h, xa = _modnorm(x[0], attn_norm[l], mods, 0, n_lat, ctx_rows=ctx[0])
        else:
            h = _modnorm(xa, attn_norm[l], mods, 0, n_lat)
        proj = _matmul([h], [(w_in_t, l, 0, 0, d, 0)], rows=t_all, n=_MAIN_WIDTH, tm=_row_tile(t_all), tn=512,
                       nt=True, out_dtype=BF16, name="in_proj")
        gates = _matmul([h], [(w_in_t, l, 0, 0, d, 0)], rows=t_all, n=LANES, col0=_MG, tm=_row_tile(t_all), tn=LANES,
                        nt=True, valid_cols=w_in.shape[-1] - _MG, name="gate_proj")

        dq = _qkprep(proj, _DQ, DIFF_HEADS, diff_q_norm[l], *rope_diff, seg=DIFF_QK_DIM,
                     scale=DIFF_QK_DIM ** -0.5 * LOG2E, split=True)
        dk = _qkprep(proj, _DK, DIFF_HEADS, diff_k_norm[l], *rope_diff, seg=DIFF_QK_DIM, scale=1.0, split=False)
        mix_d = _attention(dq, dk, proj[:, _DV:_GQ], n_kv=DIFF_HEADS, n_q_heads=2,
                           n_lat=n_lat, n_ctx=n_ctx, with_ctx_queries=need_ctx, tq=DIFF_Q_TILE,
                           diff=(diff_lambda[l], diff_out_norm[l], lam_init))

        gq = _qkprep(proj, _GQ, GQA_HEADS, gqa_q_norm[l], *rope_gqa, seg=GQA_DIM, scale=GQA_DIM ** -0.5 * LOG2E,
                     split=False)
        gk = _qkprep(proj, _GK, GQA_KV_HEADS, gqa_k_norm[l], *rope_gqa, seg=GQA_DIM, scale=1.0, split=False)
        mix_g = _attention(gq, gk, proj[:, _GV:_MQ], n_kv=GQA_KV_HEADS,
                           n_q_heads=GQA_HEADS // GQA_KV_HEADS, n_lat=n_lat, n_ctx=n_ctx, with_ctx_queries=need_ctx,
                           tq=GQA_Q_TILE)

        qk = _mlstm_conv(proj, mlstm_conv_w[l], mlstm_conv_b[l], n_lat)
        vt_m = proj[:, _MV:_MO].T
        g_t = gates[:, :4 * MLSTM_HEADS].T.reshape(4, MLSTM_HEADS, t_all)
        g_t = jnp.pad(g_t, ((0, 0), (0, 16 - MLSTM_HEADS), (0, 0)))
        gb = jnp.pad(mlstm_gate_b[l].astype(F32), ((0, 0), (0, 16 - MLSTM_HEADS)))[:, :, None]
        h_f = _mlstm_direction(qk, vt_m, g_t[0], g_t[1], gb[0], gb[1], vis_fwd.T, n_lat=n_lat, reverse=False)
        h_b = _mlstm_direction(qk, vt_m, g_t[2], g_t[3], gb[2], gb[3], vis_bwd.T, n_lat=n_lat, reverse=True)
        mix_m = _mlstm_finish(h_f, h_b, proj, mlstm_out_norm[l], rows)

        kb = 512
        w_blocks = [(w_out, l, ai, off, kb, (row0 + off) // kb)
                    for ai, (row0, width) in enumerate(((0, 1024), (1024, 1536), (2560, 1536)))
                    for off in range(0, width, kb)]
        x1 = _matmul([mix_d, mix_g, mix_m], w_blocks, rows=rows, n=d, tm=_row_tile(rows), tn=256, kind="resid",
                     resid=(xa, mods, 2), n_lat=n_lat, name="out_proj")

        if l % 2 == 0:
            tok = _modnorm(x1, ffn_norm[l], mods, 3, n_lat)
            d_ff = ffn_w_gate.shape[-1]
            hid = _matmul([tok], [(ffn_w_gate, l // 2, 0, 0, d, 0), (ffn_w_up, l // 2, 0, 0, d, 0)], rows=rows,
                          n=d_ff, tm=_row_tile(rows), tn=256, kind="swiglu", out_dtype=BF16, name="ffn_up")
            w_down = ffn_w_down[l // 2].astype(BF16)[None]
            xa = _matmul([hid], [(w_down, 0, 0, 0, d_ff, 0)], rows=rows, n=d, tm=_row_tile(rows, (768, 512, 256)),
                         tn=256, kind="resid", resid=(x1, mods, 5), n_lat=n_lat, name="ffn_down")
        else:
            assert not need_ctx
            rw = jnp.pad(router_w[l // 2].astype(F32), ((0, 0), (0, LANES - N_EXPERTS)))
            rb = jnp.pad(router_b[l // 2].astype(F32), (0, LANES - N_EXPERTS)).reshape(1, LANES)
            tok, logits = _modnorm(x1, ffn_norm[l], mods, 3, n_lat, router=(rw, rb), out_dtype=F32)
            d0, d1, w0, w1, slot_tok, block_e, n_valid, n_used = _moe_routing(logits[:, :N_EXPERTS], rows)
            hid = _moe_up(tok, exp_w_gate, exp_w_up, l // 2, block_e, n_valid, slot_tok, tn=256)
            ys = _moe_down(hid, exp_w_down, l // 2, block_e, n_valid, n_used, tn=512)
            xa = _moe_combine(ys, x1, d0, d1, w0, w1, mods, 5)
    return xa[:n_lat].reshape(batch, n_lat, d)
```

```python
import functools
import math

import jax
import jax.numpy as jnp
import numpy as np
from jax import lax
from jax.experimental import pallas as pl
from jax.experimental.pallas import tpu as pltpu

F32 = jnp.float32
BF16 = jnp.bfloat16
HIGHEST = lax.Precision.HIGHEST

GRID_W = 64
ROPE_THETA = 10000.0
NORM_EPS = 1e-6
LOG2E = math.log2(math.e)

DIFF_HEADS = 8
DIFF_QK_DIM = 64
GQA_HEADS = 12
GQA_KV_HEADS = 4
GQA_DIM = 128
MLSTM_HEADS = 12
MLSTM_DIM = 128
N_EXPERTS = 8
TOP_K = 2

LANES = 128
ROW_TILE = 256
NORM_ROWS = 16
ONES_ROWS = 16
DIFF_Q_TILE = 512
GQA_Q_TILE = 256
CONV_ROW_TILE = 768
MLSTM_CHUNK = 256
MOE_BLK = 1024
MOE_SLAB = 256
GATHER_ROWS = 512
VMEM_CAP = 60 * 1024 * 1024

_DQ, _DK, _DV = 0, 1024, 2048
_GQ, _GK, _GV = 3072, 4608, 5120
_MQ, _MK, _MV, _MO, _MG = 5632, 7168, 8704, 10240, 11776
_MAIN_WIDTH = 11776


def _cparams(sem, vmem_bytes):
    limit = int(min(VMEM_CAP, max(32 * 1024 * 1024, vmem_bytes + (8 << 20))))
    return pltpu.CompilerParams(dimension_semantics=sem, vmem_limit_bytes=limit)


def _row_tile(m, cands=(1408, 1024, 768, 512, 256)):
    for t in cands:
        if m % t == 0:
            return t
    raise ValueError(f"no row tile for {m}")


def _log_sigmoid(x):
    return jnp.minimum(x, 0.0) - jnp.log1p(jnp.exp(-jnp.abs(x)))


def _silu(x):
    return x * jax.nn.sigmoid(x)


def _mod_kernel(c_ref, w_ref, b_ref, o_ref):
    s = _silu(c_ref[...])
    o_ref[0] = jnp.dot(s, w_ref[0], preferred_element_type=F32) + b_ref[0]


def _mods(cc, w_mod, b_mod):
    n_layers, d, n6 = w_mod.shape
    tn = 512
    return pl.pallas_call(
        _mod_kernel,
        grid=(n_layers, n6 // tn),
        in_specs=[pl.BlockSpec((8, d), lambda l, j: (0, 0)),
                  pl.BlockSpec((1, d, tn), lambda l, j: (l, 0, j)),
                  pl.BlockSpec((1, 1, tn), lambda l, j: (l, 0, j))],
        out_specs=pl.BlockSpec((1, 8, tn), lambda l, j: (l, 0, j)),
        out_shape=jax.ShapeDtypeStruct((n_layers, 8, n6), F32),
        compiler_params=_cparams(("parallel", "parallel"), 2 * d * tn * 4),
        name="adaln_mods",
    )(cc, w_mod, b_mod.reshape(n_layers, 1, n6))


def _modnorm_kernel(*refs, n_lat, tm, with_router, split_input):
    n_x = 2 if split_input else 1
    x_refs, (g_ref, sh_ref, sc_ref), rest = refs[:n_x], refs[n_x:n_x + 3], refs[n_x + 3:]
    is_ctx_tile = pl.program_id(0) * tm >= n_lat
    mod_row = jnp.where(is_ctx_tile, 1, 0)
    gain = g_ref[...]
    scale1 = 1.0 + sc_ref[pl.ds(mod_row, 1), :]
    shift = sh_ref[pl.ds(mod_row, 1), :]
    o_ref = rest[2] if with_router else rest[0]

    def normalise(x_ref):
        def body(r, carry):
            rows = pl.ds(pl.multiple_of(r * NORM_ROWS, NORM_ROWS), NORM_ROWS)
            x = x_ref[rows, :]
            ms = jnp.mean(x * x, axis=-1, keepdims=True)
            o_ref[rows, :] = (x * lax.rsqrt(ms + NORM_EPS) * gain * scale1 + shift).astype(o_ref.dtype)
            if split_input:
                rest[1][rows, :] = x
            return carry

        lax.fori_loop(0, tm // NORM_ROWS, body, 0, unroll=4)

    if split_input:
        pl.when(is_ctx_tile)(lambda: normalise(x_refs[1]))
        pl.when(jnp.logical_not(is_ctx_tile))(lambda: normalise(x_refs[0]))
    else:
        normalise(x_refs[0])
    if with_router:
        rw_ref, rb_ref, _, lg_ref = rest
        h, w = o_ref[...], rw_ref[...]
        h_hi, w_hi = h.astype(BF16), w.astype(BF16)
        h_lo, w_lo = (h - h_hi.astype(F32)).astype(BF16), (w - w_hi.astype(F32)).astype(BF16)
        lg_ref[...] = (jnp.dot(h_hi, w_hi, preferred_element_type=F32) + jnp.dot(h_lo, w_hi, preferred_element_type=F32)
                       + jnp.dot(h_hi, w_lo, preferred_element_type=F32) + rb_ref[...])


def _modnorm(x, gain, mods, shift_idx, n_lat, router=None, out_dtype=BF16, ctx_rows=None):
    d = x.shape[1]
    tm = ROW_TILE
    m = x.shape[0] + (0 if ctx_rows is None else ctx_rows.shape[0])
    if ctx_rows is None:
        in_specs, args = [pl.BlockSpec((tm, d), lambda i: (i, 0))], [x]
    else:
        assert router is None and x.shape[0] == n_lat and ctx_rows.shape[0] == tm
        last = n_lat // tm - 1
        in_specs = [pl.BlockSpec((tm, d), lambda i: (jnp.minimum(i, last), 0)), pl.BlockSpec((tm, d), lambda i: (0, 0))]
        args = [x, ctx_rows]
    in_specs += [pl.BlockSpec((1, d), lambda i: (0, 0)),
                 pl.BlockSpec((8, d), lambda i: (0, shift_idx)),
                 pl.BlockSpec((8, d), lambda i: (0, shift_idx + 1))]
    args += [gain.reshape(1, d), mods, mods]
    out_specs = pl.BlockSpec((tm, d), lambda i: (i, 0))
    out_shape = jax.ShapeDtypeStruct((m, d), out_dtype)
    if router is not None:
        rw, rb = router
        in_specs += [pl.BlockSpec((d, LANES), lambda i: (0, 0)),
                     pl.BlockSpec((1, LANES), lambda i: (0, 0))]
        args += [rw, rb]
        out_specs = [out_specs, pl.BlockSpec((tm, LANES), lambda i: (i, 0))]
        out_shape = [out_shape, jax.ShapeDtypeStruct((m, LANES), F32)]
    elif ctx_rows is not None:
        out_specs = [out_specs, pl.BlockSpec((tm, d), lambda i: (i, 0))]
        out_shape = [out_shape, jax.ShapeDtypeStruct((m, d), x.dtype)]
    return pl.pallas_call(
        functools.partial(_modnorm_kernel, n_lat=n_lat, tm=tm, with_router=router is not None,
                          split_input=ctx_rows is not None),
        grid=(m // tm,),
        in_specs=in_specs, out_specs=out_specs, out_shape=out_shape,
        compiler_params=_cparams(("parallel",), 2 * tm * d * 6 + 6 * tm * d * 4 + d * LANES * 8),
        name="modnorm",
    )(*args)


def _mxu_operand(w):
    return w if w.dtype == BF16 else w.astype(BF16)


def _mm_kernel(*refs, a_of_w, kind, n_lat, tm, nt=False, valid_cols=None):
    n_a = 1 + max(ai for ai, _ in a_of_w)
    n_w = len(a_of_w)
    a_refs, w_refs, rest = refs[:n_a], refs[n_a:n_a + n_w], refs[n_a + n_w:]
    col_axis, k_axis = (0, 1) if nt else (1, 0)

    def lhs(i):
        ai, off = a_of_w[i]
        return a_refs[ai][:, off:off + w_refs[i].shape[k_axis]]

    def rhs(i):
        w = w_refs[i][...]
        if valid_cols is not None:
            w = jnp.where(lax.broadcasted_iota(jnp.int32, w.shape, col_axis) < valid_cols, w, 0.0)
        return _mxu_operand(w)

    def mm(i):
        return lax.dot_general(lhs(i), rhs(i), (((1,), (k_axis,)), ((), ())), preferred_element_type=F32)

    if kind == "swiglu":
        g, u = mm(0), mm(1)
        (o_ref,) = rest
        o_ref[...] = (_silu(g) * u).astype(o_ref.dtype)
        return
    acc = mm(0)
    for i in range(1, n_w):
        acc = acc + mm(i)
    if kind == "plain":
        (o_ref,) = rest
        o_ref[...] = acc.astype(o_ref.dtype)
    else:
        x_ref, g_ref, o_ref = rest
        rows = pl.program_id(0) * tm + lax.broadcasted_iota(jnp.int32, (tm, 1), 0)
        gate = jnp.where(rows >= n_lat, g_ref[1:2, :], g_ref[0:1, :])
        o_ref[...] = x_ref[...] + gate * acc


def _matmul(a_list, w_list, *, rows, n, tm, tn, col0=0, kind="plain", out_dtype=F32, resid=None, n_lat=0,
            nt=False, valid_cols=None, name="matmul"):
    assert col0 % tn == 0
    cb = col0 // tn
    in_specs = [pl.BlockSpec((tm, a.shape[1]), lambda i, j: (i, 0)) for a in a_list]
    args = list(a_list)
    vmem = sum(2 * tm * a.shape[1] * 2 for a in a_list) + 6 * tm * tn * 4
    for w3d, lead, _, _, k_rows, k_blk in w_list:
        if nt:
            in_specs.append(pl.BlockSpec((None, tn, k_rows), lambda i, j, lead=lead, k_blk=k_blk: (lead, cb + j, k_blk)))
        else:
            in_specs.append(pl.BlockSpec((None, k_rows, tn), lambda i, j, lead=lead, k_blk=k_blk: (lead, k_blk, cb + j)))
        args.append(w3d)
        vmem += k_rows * tn * (2 * w3d.dtype.itemsize + 2)
    if kind == "resid":
        x, mods, gate_idx = resid
        d = x.shape[1]
        in_specs += [pl.BlockSpec((tm, tn), lambda i, j: (i, j)),
                     pl.BlockSpec((8, tn), lambda i, j: (0, gate_idx * (d // tn) + j))]
        args += [x, mods]
    a_of_w = tuple((a_idx, a_off) for _, _, a_idx, a_off, _, _ in w_list)
    return pl.pallas_call(
        functools.partial(_mm_kernel, a_of_w=a_of_w, kind=kind, n_lat=n_lat, tm=tm, nt=nt, valid_cols=valid_cols),
        grid=(rows // tm, n // tn),
        in_specs=in_specs,
        out_specs=pl.BlockSpec((tm, tn), lambda i, j: (i, j)),
        out_shape=jax.ShapeDtypeStruct((rows, n), out_dtype),
        compiler_params=_cparams(("parallel", "arbitrary"), vmem),
        name=name,
    )(*args)


def _qkprep_kernel(x_ref, g_ref, c_ref, s_ref, o_ref, *, n_tiles, seg, scale, split):
    lane = lax.broadcasted_iota(jnp.int32, c_ref.shape, 1)
    gain, cos, sin = g_ref[...], c_ref[...], s_ref[...]
    half = seg // 2
    ow = 2 * LANES if split else LANES
    for j in range(n_tiles):
        x = x_ref[:, j * LANES:(j + 1) * LANES].astype(F32)
        x2 = x * x
        if seg == LANES:
            ms = jnp.sum(x2, axis=-1, keepdims=True) * (1.0 / seg)
        else:
            lo = lane < seg
            s_lo = jnp.sum(jnp.where(lo, x2, 0.0), axis=-1, keepdims=True)
            s_hi = jnp.sum(jnp.where(lo, 0.0, x2), axis=-1, keepdims=True)
            ms = jnp.where(lo, s_lo, s_hi) * (1.0 / seg)
        y = x * lax.rsqrt(ms + NORM_EPS) * gain
        if seg == LANES:
            partner = pltpu.roll(y, half, 1)
        else:
            partner = jnp.where((lane % seg) < half, pltpu.roll(y, LANES - half, 1), pltpu.roll(y, half, 1))
        y = (y * cos + partner * sin) * scale
        if split:
            o_ref[:, j * ow:j * ow + LANES] = jnp.where(lane < seg, y, 0.0).astype(o_ref.dtype)
            o_ref[:, j * ow + LANES:(j + 1) * ow] = jnp.where(lane < seg, 0.0, y).astype(o_ref.dtype)
        else:
            o_ref[:, j * ow:(j + 1) * ow] = y.astype(o_ref.dtype)


def _qkprep(proj, col0, n_tiles, gain, rope_c, rope_s, *, seg, scale, split):
    t = proj.shape[0]
    tm = ROW_TILE
    width = n_tiles * LANES
    assert col0 % width == 0
    cb = col0 // width
    g = jnp.tile(gain.astype(F32), LANES // seg).reshape(1, LANES)
    ow = (2 if split else 1) * width
    return pl.pallas_call(
        functools.partial(_qkprep_kernel, n_tiles=n_tiles, seg=seg, scale=scale, split=split),
        grid=(t // tm,),
        in_specs=[pl.BlockSpec((tm, width), lambda i: (i, cb)),
                  pl.BlockSpec((1, LANES), lambda i: (0, 0)),
                  pl.BlockSpec((tm, LANES), lambda i: (i, 0)),
                  pl.BlockSpec((tm, LANES), lambda i: (i, 0))],
        out_specs=pl.BlockSpec((tm, ow), lambda i: (i, 0)),
        out_shape=jax.ShapeDtypeStruct((t, ow), BF16),
        compiler_params=_cparams(("parallel",), 0),
        name="qk_prep",
    )(proj, g, rope_c, rope_s)


def _attn_kernel(q_ref, k_ref, vt_ref, *rest, n_q_heads, tq, tk, lam_init):
    n_keys = k_ref.shape[0]
    qs = [q_ref[:, g * LANES:(g + 1) * LANES] for g in range(n_q_heads)]

    def scores(kc):
        return tuple(lax.dot_general(kc, q, (((1,), (1,)), ((), ())), preferred_element_type=F32) for q in qs)

    def update(ss, vtc, carry):
        new = []
        for s, (m, acc) in zip(ss, carry):
            m_new = jnp.maximum(m, jnp.max(s, axis=0, keepdims=True))
            p = jnp.exp2(s - m_new).astype(BF16)
            new.append((m_new, jnp.exp2(m - m_new) * acc + jnp.dot(vtc, p, preferred_element_type=F32)))
        return tuple(new)

    carry = tuple((jnp.full((1, tq), -jnp.inf, F32), jnp.zeros((vt_ref.shape[0], tq), F32))
                  for _ in range(n_q_heads))
    ss = scores(k_ref[0:tk, :])
    for c in range(n_keys // tk):
        ss_next = scores(k_ref[(c + 1) * tk:(c + 2) * tk, :]) if (c + 1) * tk < n_keys else None
        carry = update(ss, vt_ref[:, c * tk:(c + 1) * tk], carry)
        ss = ss_next
    outs_t = [acc[:LANES] / acc[LANES:LANES + 1] for _, acc in carry]
    if lam_init is None:
        (o_ref,) = rest
        for g in range(n_q_heads):
            o_ref[:, g * LANES:(g + 1) * LANES] = outs_t[g].T.astype(o_ref.dtype)
    else:
        lam_ref, g_ref, o_ref = rest
        lp = lam_ref[...]
        lam = (jnp.exp(jnp.sum(lp[0:1, :] * lp[1:2, :], axis=-1, keepdims=True))
               - jnp.exp(jnp.sum(lp[2:3, :] * lp[3:4, :], axis=-1, keepdims=True)) + lam_init)
        o = (outs_t[0] - lam * outs_t[1]).T
        ms = jnp.mean(o * o, axis=-1, keepdims=True)
        o_ref[...] = (o * lax.rsqrt(ms + NORM_EPS) * g_ref[...] * (1.0 - lam_init)).astype(o_ref.dtype)


def _attention(q, k, v, *, n_kv, n_q_heads, n_lat, n_ctx, with_ctx_queries, tq, diff=None):
    t = k.shape[0]
    qw = n_q_heads * LANES
    ow = LANES if diff is not None else qw
    hv = LANES + ONES_ROWS
    vt = jnp.concatenate([v.T.reshape(n_kv, LANES, t), jnp.ones((n_kv, ONES_ROWS, t), v.dtype)], axis=1)
    vt = vt.reshape(n_kv * hv, t)
    extra_specs, extra_args, lam_init = [], [], None
    if diff is not None:
        lam_p, gain, lam_init = diff
        extra_specs = [pl.BlockSpec((4, DIFF_QK_DIM), lambda h, i: (0, 0)), pl.BlockSpec((1, LANES), lambda h, i: (0, 0))]
        extra_args = [lam_p.astype(F32), gain.astype(F32).reshape(1, LANES)]

    def call(q_rows, q_row0, n_keys, key0, tq, tk):
        qb0, kb0 = q_row0 // tq, key0 // n_keys
        vmem = 2 * 2 * n_keys * LANES * 2 + 4 * tq * qw * 4 + 8 * n_q_heads * tq * tk * 4
        return pl.pallas_call(
            functools.partial(_attn_kernel, n_q_heads=n_q_heads, tq=tq, tk=tk, lam_init=lam_init),
            grid=(n_kv, q_rows // tq),
            in_specs=[pl.BlockSpec((tq, qw), lambda h, i: (qb0 + i, h)),
                      pl.BlockSpec((n_keys, LANES), lambda h, i: (kb0, h)),
                      pl.BlockSpec((hv, n_keys), lambda h, i: (h, kb0))] + extra_specs,
            out_specs=pl.BlockSpec((tq, ow), lambda h, i: (i, h)),
            out_shape=jax.ShapeDtypeStruct((q_rows, n_kv * ow), BF16),
            compiler_params=_cparams(("parallel", "arbitrary"), vmem),
            name="attention",
        )(q, k, vt, *extra_args)

    tq = math.gcd(tq, n_lat)
    out = call(n_lat, 0, t, 0, tq, next(c for c in (768, 512, 256) if t % c == 0))
    if with_ctx_queries:
        assert n_lat % n_ctx == 0
        out = jnp.concatenate([out, call(n_ctx, n_lat, n_ctx, n_lat, n_ctx, n_ctx)], axis=0)
    return out


def _conv_kernel(x_ref, xp_ref, xn_ref, w_ref, b_ref, o_ref, *, n_lat, t_all, tm, k_col_tile):
    i, j = pl.program_id(0), pl.program_id(1)
    x = x_ref[...].astype(F32)
    row = lax.broadcasted_iota(jnp.int32, (tm, 1), 0)
    grow = i * tm + row
    prev = jnp.where(row == 0, xp_ref[15:16, :].astype(F32), pltpu.roll(x, 1, 0))
    nxt = jnp.where(row == tm - 1, xn_ref[0:1, :].astype(F32), pltpu.roll(x, tm - 1, 0))
    prev = jnp.where((grow == 0) | (grow == n_lat), 0.0, prev)
    nxt = jnp.where((grow == n_lat - 1) | (grow == t_all - 1), 0.0, nxt)
    y = b_ref[...] + prev * w_ref[0:1, :] + x * w_ref[1:2, :] + nxt * w_ref[2:3, :]
    y = _silu(y) * jnp.where(j >= k_col_tile, MLSTM_DIM ** -0.5, 1.0)
    o_ref[...] = y.astype(o_ref.dtype)


def _mlstm_conv(proj, conv_w, conv_b, n_lat):
    t = proj.shape[0]
    tm, tc, hb = (CONV_ROW_TILE if t % CONV_ROW_TILE == 0 else ROW_TILE), 512, 16
    width = 2 * MLSTM_HEADS * MLSTM_DIM
    cb = _MQ // tc
    nrb = t // hb
    return pl.pallas_call(
        functools.partial(_conv_kernel, n_lat=n_lat, t_all=t, tm=tm, k_col_tile=(width // 2) // tc),
        grid=(t // tm, width // tc),
        in_specs=[pl.BlockSpec((tm, tc), lambda i, j: (i, cb + j)),
                  pl.BlockSpec((hb, tc), lambda i, j: (jnp.maximum(i * (tm // hb) - 1, 0), cb + j)),
                  pl.BlockSpec((hb, tc), lambda i, j: (jnp.minimum((i + 1) * (tm // hb), nrb - 1), cb + j)),
                  pl.BlockSpec((3, tc), lambda i, j: (0, j)),
                  pl.BlockSpec((1, tc), lambda i, j: (0, j))],
        out_specs=pl.BlockSpec((tm, tc), lambda i, j: (i, j)),
        out_shape=jax.ShapeDtypeStruct((t, width), BF16),
        compiler_params=_cparams(("parallel", "arbitrary"), 0),
        name="mlstm_conv",
    )(proj, proj, proj, conv_w.astype(F32), conv_b.astype(F32).reshape(1, width))


def _mlstm_kernel(q_ref, k_ref, vt_ref, gi_ref, gf_ref, bi_ref, bf_ref, mask_ref, ht_ref, ct_sc, n_sc, m_sc, *, n_heads):
    @pl.when(pl.program_id(0) == 0)
    def _():
        ct_sc[...] = jnp.zeros_like(ct_sc)
        n_sc[...] = jnp.zeros_like(n_sc)
        m_sc[...] = jnp.zeros_like(m_sc)

    vis_t = mask_ref[...]
    li_all = gi_ref[...] + bi_ref[...]
    lf_all = _log_sigmoid(gf_ref[...] + bf_ref[...])
    br_all = jnp.dot(lf_all, vis_t, preferred_element_type=F32, precision=HIGHEST)
    bend_all = jnp.sum(lf_all, axis=1, keepdims=True)
    c_cols = (br_all - li_all).T
    d = MLSTM_DIM
    nt = (((1,), (1,)), ((), ()))
    for h in range(n_heads):
        lir, br = li_all[h:h + 1, :], br_all[h:h + 1, :]
        b_end = bend_all[h:h + 1, :]
        m_prev = m_sc[h:h + 1, 0:1]
        lw = b_end - br + lir
        m_new = jnp.maximum(b_end + m_prev, jnp.max(lw, axis=1, keepdims=True))
        w_end = jnp.exp(lw - m_new)
        decay = jnp.exp(b_end + m_prev - m_new)
        log_d = jnp.where(vis_t > 0.0, br - c_cols[:, h:h + 1], -jnp.inf)
        m_t = jnp.maximum(br + m_prev, jnp.max(log_d, axis=0, keepdims=True))
        dmat = jnp.exp(log_d - m_t)
        inter = jnp.exp(br + m_prev - m_t)
        qh = q_ref[:, h * d:(h + 1) * d]
        kh = k_ref[:, h * d:(h + 1) * d]
        vth = vt_ref[h * d:(h + 1) * d, :]
        ct_prev = ct_sc[h]
        n_prev = n_sc[h:h + 1, :]
        sc = lax.dot_general(kh, qh, nt, preferred_element_type=F32) * dmat
        num = (jnp.dot(vth, sc.astype(BF16), preferred_element_type=F32)
               + inter * lax.dot_general(ct_prev.astype(BF16), qh, nt, preferred_element_type=F32))
        n8 = jnp.broadcast_to(n_prev, (8, d)).astype(BF16)
        den = (jnp.sum(sc, axis=0, keepdims=True)
               + inter * lax.dot_general(n8, qh, nt, preferred_element_type=F32)[0:1, :])
        ht_ref[h * d:(h + 1) * d, :] = num / jnp.maximum(jnp.abs(den), jnp.exp(-m_t))
        vtw = (vth.astype(F32) * w_end).astype(BF16)
        ct_sc[h] = decay * ct_prev + jnp.dot(vtw, kh, preferred_element_type=F32)
        w8 = jnp.broadcast_to(w_end, (8, w_end.shape[1])).astype(BF16)
        n_sc[h:h + 1, :] = decay * n_prev + jnp.dot(w8, kh, preferred_element_type=F32)[0:1, :]
        m_sc[h:h + 1, :] = jnp.broadcast_to(m_new, (1, LANES))


def _mlstm_direction(qk, vt, gi, gf, bi, bf, vis_t, *, n_lat, reverse):
    t = qk.shape[0]
    lc = MLSTM_CHUNK
    n_chunks = t // lc
    ctx_blk = n_lat // lc
    hd = MLSTM_HEADS * MLSTM_DIM

    def blk(c):
        lat = (n_chunks - 1 - c) if reverse else (c - 1)
        return jnp.where(c == 0, ctx_blk, lat)

    return pl.pallas_call(
        functools.partial(_mlstm_kernel, n_heads=MLSTM_HEADS),
        grid=(n_chunks,),
        in_specs=[pl.BlockSpec((lc, hd), lambda c: (blk(c), 0)),
                  pl.BlockSpec((lc, hd), lambda c: (blk(c), 1)),
                  pl.BlockSpec((hd, lc), lambda c: (0, blk(c))),
                  pl.BlockSpec((16, lc), lambda c: (0, blk(c))),
                  pl.BlockSpec((16, lc), lambda c: (0, blk(c))),
                  pl.BlockSpec((16, 1), lambda c: (0, 0)),
                  pl.BlockSpec((16, 1), lambda c: (0, 0)),
                  pl.BlockSpec((lc, lc), lambda c: (0, 0))],
        out_specs=pl.BlockSpec((hd, lc), lambda c: (0, blk(c))),
        out_shape=jax.ShapeDtypeStruct((hd, t), F32),
        scratch_shapes=[pltpu.VMEM((MLSTM_HEADS, MLSTM_DIM, MLSTM_DIM), F32),
                        pltpu.VMEM((16, LANES), F32),
                        pltpu.VMEM((16, LANES), F32)],
        compiler_params=_cparams(("arbitrary",), 16 << 20),
        name="mlstm_bwd" if reverse else "mlstm_fwd",
    )(qk, qk, vt, gi, gf, bi, bf, vis_t)


def _mlstm_finish_kernel(hf_ref, hb_ref, o0_ref, o1_ref, o2_ref, g_ref, out_ref, *, n_heads):
    d = MLSTM_DIM
    o_refs = (o0_ref, o1_ref, o2_ref)
    per_blk = o0_ref.shape[1] // d
    for h in range(n_heads):
        sl = slice(h * d, (h + 1) * d)
        s = hf_ref[sl, :] + hb_ref[sl, :]
        ms = jnp.mean(s * s, axis=0, keepdims=True)
        y = (s * lax.rsqrt(ms + NORM_EPS)).T * g_ref[:, sl]
        o = o_refs[h // per_blk][:, (h % per_blk) * d:(h % per_blk + 1) * d]
        out_ref[:, sl] = (jax.nn.sigmoid(o.astype(F32)) * y).astype(out_ref.dtype)


def _mlstm_finish(hf_t, hb_t, proj, out_gain, rows):
    hd = MLSTM_HEADS * MLSTM_DIM
    tm = ROW_TILE
    ow = hd // 3
    assert _MO % ow == 0
    return pl.pallas_call(
        functools.partial(_mlstm_finish_kernel, n_heads=MLSTM_HEADS),
        grid=(rows // tm,),
        in_specs=[pl.BlockSpec((hd, tm), lambda i: (0, i)),
                  pl.BlockSpec((hd, tm), lambda i: (0, i)),
                  pl.BlockSpec((tm, ow), lambda i: (i, _MO // ow)),
                  pl.BlockSpec((tm, ow), lambda i: (i, _MO // ow + 1)),
                  pl.BlockSpec((tm, ow), lambda i: (i, _MO // ow + 2)),
                  pl.BlockSpec((1, hd), lambda i: (0, 0))],
        out_specs=pl.BlockSpec((tm, hd), lambda i: (i, 0)),
        out_shape=jax.ShapeDtypeStruct((rows, hd), BF16),
        compiler_params=_cparams(("parallel",), 0),
        name="mlstm_finish",
    )(hf_t, hb_t, proj, proj, proj, out_gain.astype(F32).reshape(1, hd))


def _moe_up_kernel(be_ref, nv_ref, st_ref, tok_ref, wg_ref, wu_ref, o_ref, xbuf, xb, sem, *, nj):
    b, j = pl.program_id(0), pl.program_id(1)
    nb = pl.num_programs(0)
    n_valid = nv_ref[b]
    per_step = MOE_BLK // nj

    def copy(blk, r):
        src = tok_ref.at[pl.ds(st_ref[blk * MOE_BLK + r], 1), :]
        return pltpu.make_async_copy(src, xbuf.at[pl.ds(r, 1), :], sem)

    @pl.when((b == 0) & (j == 0))
    def _():
        def body(r, carry):
            copy(0, r).start()
            return carry
        lax.fori_loop(0, MOE_BLK, body, 0)

    was_requested = (b == 0) | (nv_ref[jnp.maximum(b - 1, 0)] > 0)

    @pl.when((j == 0) & was_requested)
    def _():
        pltpu.make_async_copy(tok_ref.at[pl.ds(0, MOE_BLK), :], xbuf, sem).wait()
        xb[...] = xbuf[...].astype(BF16)

    def request_next():
        nxt = jnp.minimum(b + 1, nb - 1)
        for r in range(per_step):
            copy(nxt, j * per_step + r).start()

    def swiglu(sl):
        a = xb[sl, :]
        g = jnp.dot(a, _mxu_operand(wg_ref[...]), preferred_element_type=F32)
        u = jnp.dot(a, _mxu_operand(wu_ref[...]), preferred_element_type=F32)
        o_ref[sl, :] = (_silu(g) * u).astype(o_ref.dtype)

    @pl.when(n_valid == MOE_BLK)
    def _():
        request_next()
        swiglu(slice(None))

    @pl.when((n_valid > 0) & (n_valid < MOE_BLK))
    def _():
        request_next()
        for s in range(MOE_BLK // MOE_SLAB):
            sl = slice(s * MOE_SLAB, (s + 1) * MOE_SLAB)

            @pl.when(n_valid > s * MOE_SLAB)
            def _():
                swiglu(sl)

            @pl.when(n_valid <= s * MOE_SLAB)
            def _():
                o_ref[sl, :] = jnp.zeros((MOE_SLAB, o_ref.shape[1]), o_ref.dtype)

    @pl.when(n_valid == 0)
    def _():
        o_ref[...] = jnp.zeros_like(o_ref)

    @pl.when((b == nb - 1) & (j == nj - 1) & (n_valid > 0))
    def _():
        pltpu.make_async_copy(tok_ref.at[pl.ds(0, MOE_BLK), :], xbuf, sem).wait()


def _moe_up(tok, w_gate, w_up, lead, block_e, n_valid, slot_tok, *, tn):
    n_slots = slot_tok.shape[0]
    k = tok.shape[1]
    n = w_gate.shape[-1]
    nb, nj = n_slots // MOE_BLK, n // tn
    assert MOE_BLK % nj == 0

    def w_map(b, j, be, nv, st):
        return (lead, be[b], 0, jnp.where(nv[b] > 0, j, nj - 1))

    return pl.pallas_call(
        functools.partial(_moe_up_kernel, nj=nj),
        grid_spec=pltpu.PrefetchScalarGridSpec(
            num_scalar_prefetch=3,
            grid=(nb, nj),
            in_specs=[pl.BlockSpec(memory_space=pl.ANY),
                      pl.BlockSpec((None, None, k, tn), w_map),
                      pl.BlockSpec((None, None, k, tn), w_map)],
            out_specs=pl.BlockSpec((MOE_BLK, tn), lambda b, j, be, nv, st: (b, j)),
            scratch_shapes=[pltpu.VMEM((MOE_BLK, k), F32),
                            pltpu.VMEM((MOE_BLK, k), BF16),
                            pltpu.SemaphoreType.DMA(())]),
        out_shape=jax.ShapeDtypeStruct((n_slots, n), BF16),
        compiler_params=_cparams(("arbitrary", "arbitrary"),
                                 MOE_BLK * k * 6 + 2 * k * tn * (2 * w_gate.dtype.itemsize + 2)
                                 + 4 * MOE_BLK * tn * 4),
        name="moe_up",
    )(block_e, n_valid, slot_tok, tok, w_gate, w_up)


def _moe_down_kernel(be_ref, nv_ref, nu_ref, a_ref, w_ref, o_ref):
    n_valid = nv_ref[pl.program_id(0)]

    def down(sl):
        o_ref[sl, :] = jnp.dot(a_ref[sl, :], _mxu_operand(w_ref[...]), preferred_element_type=F32)

    @pl.when(n_valid == MOE_BLK)
    def _():
        down(slice(None))

    @pl.when((n_valid > 0) & (n_valid < MOE_BLK))
    def _():
        for s in range(MOE_BLK // MOE_SLAB):
            sl = slice(s * MOE_SLAB, (s + 1) * MOE_SLAB)

            @pl.when(n_valid > s * MOE_SLAB)
            def _():
                down(sl)

            @pl.when(n_valid <= s * MOE_SLAB)
            def _():
                o_ref[sl, :] = jnp.zeros((MOE_SLAB, o_ref.shape[1]), o_ref.dtype)

    @pl.when(n_valid == 0)
    def _():
        o_ref[...] = jnp.zeros_like(o_ref)


def _moe_down(a, w, lead, block_e, n_valid, n_used, *, tn):
    n_slots, k = a.shape
    n = w.shape[-1]
    nb, nj = n_slots // MOE_BLK, n // tn

    def a_map(b, j, be, nv, nu):
        return (jnp.minimum(b, nu[0] - 1), 0)

    def w_map(b, j, be, nv, nu):
        return (lead, be[b], 0, jnp.where(nv[b] > 0, j, nj - 1))

    return pl.pallas_call(
        _moe_down_kernel,
        grid_spec=pltpu.PrefetchScalarGridSpec(
            num_scalar_prefetch=3,
            grid=(nb, nj),
            in_specs=[pl.BlockSpec((MOE_BLK, k), a_map), pl.BlockSpec((None, None, k, tn), w_map)],
            out_specs=pl.BlockSpec((MOE_BLK, tn), lambda b, j, be, nv, nu: (b, j))),
        out_shape=jax.ShapeDtypeStruct((n_slots, n), F32),
        compiler_params=_cparams(("arbitrary", "arbitrary"),
                                 2 * MOE_BLK * k * 2 + k * tn * (2 * w.dtype.itemsize + 2) + 4 * MOE_BLK * tn * 4),
        name="moe_down",
    )(block_e, n_valid, n_used, a, w)


def _combine_kernel(d0_ref, d1_ref, y_ref, x_ref, w0_ref, w1_ref, g_ref, o_ref, buf, sem, *, rows):
    i = pl.program_id(0)
    n = pl.num_programs(0)
    slot = i % 2

    def gather(step, slot_):
        base = step * rows

        def body(r, carry):
            for which, d_ref in enumerate((d0_ref, d1_ref)):
                pltpu.make_async_copy(y_ref.at[pl.ds(d_ref[base + r], 1), :],
                                      buf.at[slot_, which, pl.ds(r, 1), :], sem.at[slot_]).start()
            return carry

        lax.fori_loop(0, rows, body, 0, unroll=8)

    @pl.when(i == 0)
    def _():
        gather(0, 0)

    @pl.when(i + 1 < n)
    def _():
        gather(i + 1, 1 - slot)

    for which in range(2):
        pltpu.make_async_copy(y_ref.at[pl.ds(0, rows), :], buf.at[slot, which], sem.at[slot]).wait()
    y = w0_ref[...] * buf[slot, 0] + w1_ref[...] * buf[slot, 1]
    o_ref[...] = x_ref[...] + g_ref[0:1, :] * y


def _moe_combine(y_slots, x, d0, d1, w0, w1, mods, gate_idx):
    n_tok, d = x.shape
    rows = GATHER_ROWS // 2
    return pl.pallas_call(
        functools.partial(_combine_kernel, rows=rows),
        grid_spec=pltpu.PrefetchScalarGridSpec(
            num_scalar_prefetch=2,
            grid=(n_tok // rows,),
            in_specs=[pl.BlockSpec(memory_space=pl.ANY),
                      pl.BlockSpec((rows, d), lambda i, a, b: (i, 0)),
                      pl.BlockSpec((rows, 1), lambda i, a, b: (i, 0)),
                      pl.BlockSpec((rows, 1), lambda i, a, b: (i, 0)),
                      pl.BlockSpec((8, d), lambda i, a, b: (0, gate_idx))],
            out_specs=pl.BlockSpec((rows, d), lambda i, a, b: (i, 0)),
            scratch_shapes=[pltpu.VMEM((2, 2, rows, d), F32), pltpu.SemaphoreType.DMA((2,))]),
        out_shape=jax.ShapeDtypeStruct((n_tok, d), F32),
        compiler_params=_cparams(("arbitrary",), 12 * rows * d * 4),
        name="moe_combine",
    )(d0, d1, y_slots, x, w0, w1, mods)


def _moe_routing(logits, n_tok):
    top_logit, top_idx = lax.top_k(logits, TOP_K)
    top_w = jax.nn.softmax(top_logit, axis=-1)
    flat_e = top_idx.reshape(-1).astype(jnp.int32)
    n_assign = n_tok * TOP_K
    onehot = (flat_e[:, None] == jnp.arange(N_EXPERTS, dtype=jnp.int32)[None, :]).astype(jnp.int32)
    csum = jnp.cumsum(onehot, axis=0)
    rank = jnp.take_along_axis(csum, flat_e[:, None], axis=1)[:, 0] - 1
    counts = csum[-1]
    padded = (counts + MOE_BLK - 1) // MOE_BLK * MOE_BLK
    pad_ends = jnp.cumsum(padded)
    pad_starts = pad_ends - padded
    dest = (pad_starts[flat_e] + rank).astype(jnp.int32)
    n_blocks = -(-(n_assign + N_EXPERTS * (MOE_BLK - 1)) // MOE_BLK)
    n_slots = n_blocks * MOE_BLK
    slot_tok = jnp.zeros((n_slots,), jnp.int32).at[dest].set(jnp.arange(n_assign, dtype=jnp.int32) // TOP_K)
    blk_start = jnp.arange(n_blocks, dtype=jnp.int32) * MOE_BLK
    block_e = jnp.minimum(jnp.searchsorted(pad_ends, blk_start, side="right"), N_EXPERTS - 1).astype(jnp.int32)
    slab_ends = pad_starts + (counts + MOE_SLAB - 1) // MOE_SLAB * MOE_SLAB
    n_valid = jnp.clip(slab_ends[block_e] - blk_start, 0, MOE_BLK).astype(jnp.int32)
    n_used = (pad_ends[-1] // MOE_BLK).astype(jnp.int32).reshape(1)
    dest2 = dest.reshape(n_tok, TOP_K)
    return dest2[:, 0], dest2[:, 1], top_w[:, 0:1], top_w[:, 1:2], slot_tok, block_e, n_valid, n_used


def _rope_tables(n_lat, n_ctx, dim):
    rows = n_lat // GRID_W
    row = np.repeat(np.arange(rows, dtype=np.float32), GRID_W)
    col = np.tile(np.arange(GRID_W, dtype=np.float32), rows)
    quarter = dim // 4
    inv = (np.float32(ROPE_THETA) ** (-np.arange(quarter, dtype=np.float32) / np.float32(quarter))).astype(np.float32)
    ang = np.concatenate([row[:, None] * inv, col[:, None] * inv], axis=-1).astype(np.float32)
    cos, sin = np.cos(ang), np.sin(ang)
    reps = LANES // dim
    c = np.tile(np.concatenate([cos, cos], axis=-1), (1, reps))
    s = np.tile(np.concatenate([-sin, sin], axis=-1), (1, reps))
    c = np.concatenate([c, np.ones((n_ctx, LANES), np.float32)], axis=0)
    s = np.concatenate([s, np.zeros((n_ctx, LANES), np.float32)], axis=0)
    return jnp.asarray(c, F32), jnp.asarray(s, F32)


def kernel(x, c, ctx, c_ctx, w_mod, b_mod, attn_norm, ffn_norm, w_in, w_out, diff_q_norm, diff_k_norm, diff_lambda, diff_out_norm, gqa_q_norm, gqa_k_norm, mlstm_conv_w, mlstm_conv_b, mlstm_gate_b, mlstm_out_norm, ffn_w_gate, ffn_w_up, ffn_w_down, router_w, router_b, exp_w_gate, exp_w_up, exp_w_down):
    batch, n_lat, d = x.shape
    n_ctx = ctx.shape[1]
    depth = w_in.shape[0]
    assert batch == 1 and n_ctx == ROW_TILE and n_lat % MLSTM_CHUNK == 0 and n_lat % GRID_W == 0
    t_all = n_lat + n_ctx
    hd = MLSTM_HEADS * MLSTM_DIM

    cc = jnp.zeros((8, d), F32).at[0].set(c[0]).at[1].set(c_ctx)
    mods_all = _mods(cc, w_mod, b_mod)
    rope_diff = _rope_tables(n_lat, n_ctx, DIFF_QK_DIM)
    rope_gqa = _rope_tables(n_lat, n_ctx, GQA_DIM)
    pos = jnp.arange(MLSTM_CHUNK)
    vis_fwd = (pos[None, :] <= pos[:, None]).astype(F32)
    vis_bwd = (pos[None, :] >= pos[:, None]).astype(F32)

    w_in_t = jnp.swapaxes(w_in, 1, 2)
    xa = None
    for l in range(depth):
        need_ctx = l < depth - 1
        rows = t_all if need_ctx else n_lat
        lam_init = 0.8 - 0.6 * math.exp(-0.3 * l)
        mods = mods_all[l]

        if l == 0:
            h, xa = _modnorm(x[0], attn_norm[l], mods, 0, n_lat, ctx_rows=ctx[0])
        else:
            h = _modnorm(xa, attn_norm[l], mods, 0, n_lat)
        proj = _matmul([h], [(w_in_t, l, 0, 0, d, 0)], rows=t_all, n=_MAIN_WIDTH, tm=_row_tile(t_all), tn=512,
                       nt=True, out_dtype=BF16, name="in_proj")
        gates = _matmul([h], [(w_in_t, l, 0, 0, d, 0)], rows=t_all, n=LANES, col0=_MG, tm=_row_tile(t_all), tn=LANES,
                        nt=True, valid_cols=w_in.shape[-1] - _MG, name="gate_proj")

        dq = _qkprep(proj, _DQ, DIFF_HEADS, diff_q_norm[l], *rope_diff, seg=DIFF_QK_DIM,
                     scale=DIFF_QK_DIM ** -0.5 * LOG2E, split=True)
        dk = _qkprep(proj, _DK, DIFF_HEADS, diff_k_norm[l], *rope_diff, seg=DIFF_QK_DIM, scale=1.0, split=False)
        mix_d = _attention(dq, dk, proj[:, _DV:_GQ], n_kv=DIFF_HEADS, n_q_heads=2,
                           n_lat=n_lat, n_ctx=n_ctx, with_ctx_queries=need_ctx, tq=DIFF_Q_TILE,
                           diff=(diff_lambda[l], diff_out_norm[l], lam_init))

        gq = _qkprep(proj, _GQ, GQA_HEADS, gqa_q_norm[l], *rope_gqa, seg=GQA_DIM, scale=GQA_DIM ** -0.5 * LOG2E,
                     split=False)
        gk = _qkprep(proj, _GK, GQA_KV_HEADS, gqa_k_norm[l], *rope_gqa, seg=GQA_DIM, scale=1.0, split=False)
        mix_g = _attention(gq, gk, proj[:, _GV:_MQ], n_kv=GQA_KV_HEADS,
                           n_q_heads=GQA_HEADS // GQA_KV_HEADS, n_lat=n_lat, n_ctx=n_ctx, with_ctx_queries=need_ctx,
                           tq=GQA_Q_TILE)

        qk = _mlstm_conv(proj, mlstm_conv_w[l], mlstm_conv_b[l], n_lat)
        vt_m = proj[:, _MV:_MO].T
        g_t = gates[:, :4 * MLSTM_HEADS].T.reshape(4, MLSTM_HEADS, t_all)
        g_t = jnp.pad(g_t, ((0, 0), (0, 16 - MLSTM_HEADS), (0, 0)))
        gb = jnp.pad(mlstm_gate_b[l].astype(F32), ((0, 0), (0, 16 - MLSTM_HEADS)))[:, :, None]
        h_f = _mlstm_direction(qk, vt_m, g_t[0], g_t[1], gb[0], gb[1], vis_fwd.T, n_lat=n_lat, reverse=False)
        h_b = _mlstm_direction(qk, vt_m, g_t[2], g_t[3], gb[2], gb[3], vis_bwd.T, n_lat=n_lat, reverse=True)
        mix_m = _mlstm_finish(h_f, h_b, proj, mlstm_out_norm[l], rows)

        kb = 512
        w_blocks = [(w_out, l, ai, off, kb, (row0 + off) // kb)
                    for ai, (row0, width) in enumerate(((0, 1024), (1024, 1536), (2560, 1536)))
                    for off in range(0, width, kb)]
        x1 = _matmul([mix_d, mix_g, mix_m], w_blocks, rows=rows, n=d, tm=_row_tile(rows), tn=256, kind="resid",
                     resid=(xa, mods, 2), n_lat=n_lat, name="out_proj")

        if l % 2 == 0:
            tok = _modnorm(x1, ffn_norm[l], mods, 3, n_lat)
            d_ff = ffn_w_gate.shape[-1]
            hid = _matmul([tok], [(ffn_w_gate, l // 2, 0, 0, d, 0), (ffn_w_up, l // 2, 0, 0, d, 0)], rows=rows,
                          n=d_ff, tm=_row_tile(rows), tn=256, kind="swiglu", out_dtype=BF16, name="ffn_up")
            w_down = ffn_w_down[l // 2].astype(BF16)[None]
            xa = _matmul([hid], [(w_down, 0, 0, 0, d_ff, 0)], rows=rows, n=d, tm=_row_tile(rows, (768, 512, 256)),
                         tn=256, kind="resid", resid=(x1, mods, 5), n_lat=n_lat, name="ffn_down")
        else:
            assert not need_ctx
            rw = jnp.pad(router_w[l // 2].astype(F32), ((0, 0), (0, LANES - N_EXPERTS)))
            rb = jnp.pad(router_b[l // 2].astype(F32), (0, LANES - N_EXPERTS)).reshape(1, LANES)
            tok, logits = _modnorm(x1, ffn_norm[l], mods, 3, n_lat, router=(rw, rb), out_dtype=F32)
            d0, d1, w0, w1, slot_tok, block_e, n_valid, n_used = _moe_routing(logits[:, :N_EXPERTS], rows)
            hid = _moe_up(tok, exp_w_gate, exp_w_up, l // 2, block_e, n_valid, slot_tok, tn=256)
            ys = _moe_down(hid, exp_w_down, l // 2, block_e, n_valid, n_used, tn=512)
            xa = _moe_combine(ys, x1, d0, d1, w0, w1, mods, 5)
    return xa[:n_lat].reshape(batch, n_lat, d)
```

```python
import functools
import math

import jax
import jax.numpy as jnp
import numpy as np
from jax import lax
from jax.experimental import pallas as pl
from jax.experimental.pallas import tpu as pltpu

F32 = jnp.float32
BF16 = jnp.bfloat16
HIGHEST = lax.Precision.HIGHEST

GRID_W = 64
ROPE_THETA = 10000.0
NORM_EPS = 1e-6
LOG2E = math.log2(math.e)

DIFF_HEADS = 8
DIFF_QK_DIM = 64
GQA_HEADS = 12
GQA_KV_HEADS = 4
GQA_DIM = 128
MLSTM_HEADS = 12
MLSTM_DIM = 128
N_EXPERTS = 8
TOP_K = 2

LANES = 128
ROW_TILE = 256
NORM_ROWS = 16
ONES_ROWS = 16
DIFF_Q_TILE = 512
GQA_Q_TILE = 256
CONV_ROW_TILE = 768
MLSTM_CHUNK = 256
MOE_BLK = 1024
MOE_SLAB = 256
GATHER_ROWS = 256
VMEM_CAP = 60 * 1024 * 1024

_DQ, _DK, _DV = 0, 1024, 2048
_GQ, _GK, _GV = 3072, 4608, 5120
_MQ, _MK, _MV, _MO, _MG = 5632, 7168, 8704, 10240, 11776
_MAIN_WIDTH = 11776


def _cparams(sem, vmem_bytes):
    limit = int(min(VMEM_CAP, max(32 * 1024 * 1024, vmem_bytes + (8 << 20))))
    return pltpu.CompilerParams(dimension_semantics=sem, vmem_limit_bytes=limit)


def _row_tile(m, cands=(1408, 1024, 768, 512, 256)):
    for t in cands:
        if m % t == 0:
            return t
    raise ValueError(f"no row tile for {m}")


def _log_sigmoid(x):
    return jnp.minimum(x, 0.0) - jnp.log1p(jnp.exp(-jnp.abs(x)))


def _silu(x):
    return x * jax.nn.sigmoid(x)


def _mod_kernel(c_ref, w_ref, b_ref, o_ref):
    s = _silu(c_ref[...])
    o_ref[0] = jnp.dot(s, w_ref[0], preferred_element_type=F32) + b_ref[0]


def _mods(cc, w_mod, b_mod):
    n_layers, d, n6 = w_mod.shape
    tn = 512
    return pl.pallas_call(
        _mod_kernel,
        grid=(n_layers, n6 // tn),
        in_specs=[pl.BlockSpec((8, d), lambda l, j: (0, 0)),
                  pl.BlockSpec((1, d, tn), lambda l, j: (l, 0, j)),
                  pl.BlockSpec((1, 1, tn), lambda l, j: (l, 0, j))],
        out_specs=pl.BlockSpec((1, 8, tn), lambda l, j: (l, 0, j)),
        out_shape=jax.ShapeDtypeStruct((n_layers, 8, n6), F32),
        compiler_params=_cparams(("parallel", "parallel"), 2 * d * tn * 4),
        name="adaln_mods",
    )(cc, w_mod, b_mod.reshape(n_layers, 1, n6))


def _modnorm_kernel(*refs, n_lat, tm, with_router, split_input):
    n_x = 2 if split_input else 1
    x_refs, (g_ref, sh_ref, sc_ref), rest = refs[:n_x], refs[n_x:n_x + 3], refs[n_x + 3:]
    is_ctx_tile = pl.program_id(0) * tm >= n_lat
    mod_row = jnp.where(is_ctx_tile, 1, 0)
    gain = g_ref[...]
    scale1 = 1.0 + sc_ref[pl.ds(mod_row, 1), :]
    shift = sh_ref[pl.ds(mod_row, 1), :]
    o_ref = rest[2] if with_router else rest[0]

    def normalise(x_ref):
        def body(r, carry):
            rows = pl.ds(pl.multiple_of(r * NORM_ROWS, NORM_ROWS), NORM_ROWS)
            x = x_ref[rows, :]
            ms = jnp.mean(x * x, axis=-1, keepdims=True)
            o_ref[rows, :] = (x * lax.rsqrt(ms + NORM_EPS) * gain * scale1 + shift).astype(o_ref.dtype)
            if split_input:
                rest[1][rows, :] = x
            return carry

        lax.fori_loop(0, tm // NORM_ROWS, body, 0, unroll=4)

    if split_input:
        pl.when(is_ctx_tile)(lambda: normalise(x_refs[1]))
        pl.when(jnp.logical_not(is_ctx_tile))(lambda: normalise(x_refs[0]))
    else:
        normalise(x_refs[0])
    if with_router:
        rw_ref, rb_ref, _, lg_ref = rest
        lg_ref[...] = jnp.dot(o_ref[...], rw_ref[...], preferred_element_type=F32, precision=HIGHEST) + rb_ref[...]


def _modnorm(x, gain, mods, shift_idx, n_lat, router=None, out_dtype=BF16, ctx_rows=None):
    d = x.shape[1]
    tm = ROW_TILE
    m = x.shape[0] + (0 if ctx_rows is None else ctx_rows.shape[0])
    if ctx_rows is None:
        in_specs, args = [pl.BlockSpec((tm, d), lambda i: (i, 0))], [x]
    else:
        assert router is None and x.shape[0] == n_lat and ctx_rows.shape[0] == tm
        last = n_lat // tm - 1
        in_specs = [pl.BlockSpec((tm, d), lambda i: (jnp.minimum(i, last), 0)), pl.BlockSpec((tm, d), lambda i: (0, 0))]
        args = [x, ctx_rows]
    in_specs += [pl.BlockSpec((1, d), lambda i: (0, 0)),
                 pl.BlockSpec((8, d), lambda i: (0, shift_idx)),
                 pl.BlockSpec((8, d), lambda i: (0, shift_idx + 1))]
    args += [gain.reshape(1, d), mods, mods]
    out_specs = pl.BlockSpec((tm, d), lambda i: (i, 0))
    out_shape = jax.ShapeDtypeStruct((m, d), out_dtype)
    if router is not None:
        rw, rb = router
        in_specs += [pl.BlockSpec((d, LANES), lambda i: (0, 0)),
                     pl.BlockSpec((1, LANES), lambda i: (0, 0))]
        args += [rw, rb]
        out_specs = [out_specs, pl.BlockSpec((tm, LANES), lambda i: (i, 0))]
        out_shape = [out_shape, jax.ShapeDtypeStruct((m, LANES), F32)]
    elif ctx_rows is not None:
        out_specs = [out_specs, pl.BlockSpec((tm, d), lambda i: (i, 0))]
        out_shape = [out_shape, jax.ShapeDtypeStruct((m, d), x.dtype)]
    return pl.pallas_call(
        functools.partial(_modnorm_kernel, n_lat=n_lat, tm=tm, with_router=router is not None,
                          split_input=ctx_rows is not None),
        grid=(m // tm,),
        in_specs=in_specs, out_specs=out_specs, out_shape=out_shape,
        compiler_params=_cparams(("parallel",), 2 * tm * d * 6 + 6 * tm * d * 4 + d * LANES * 8),
        name="modnorm",
    )(*args)


def _mxu_operand(w):
    return w if w.dtype == BF16 else w.astype(BF16)


def _mm_kernel(*refs, a_of_w, kind, n_lat, tm, nt=False, valid_cols=None):
    n_a = 1 + max(ai for ai, _ in a_of_w)
    n_w = len(a_of_w)
    a_refs, w_refs, rest = refs[:n_a], refs[n_a:n_a + n_w], refs[n_a + n_w:]
    col_axis, k_axis = (0, 1) if nt else (1, 0)

    def lhs(i):
        ai, off = a_of_w[i]
        return a_refs[ai][:, off:off + w_refs[i].shape[k_axis]]

    def rhs(i):
        w = w_refs[i][...]
        if valid_cols is not None:
            w = jnp.where(lax.broadcasted_iota(jnp.int32, w.shape, col_axis) < valid_cols, w, 0.0)
        return _mxu_operand(w)

    def mm(i):
        return lax.dot_general(lhs(i), rhs(i), (((1,), (k_axis,)), ((), ())), preferred_element_type=F32)

    if kind == "swiglu":
        g, u = mm(0), mm(1)
        (o_ref,) = rest
        o_ref[...] = (_silu(g) * u).astype(o_ref.dtype)
        return
    acc = mm(0)
    for i in range(1, n_w):
        acc = acc + mm(i)
    if kind == "plain":
        (o_ref,) = rest
        o_ref[...] = acc.astype(o_ref.dtype)
    else:
        x_ref, g_ref, o_ref = rest
        rows = pl.program_id(0) * tm + lax.broadcasted_iota(jnp.int32, (tm, 1), 0)
        gate = jnp.where(rows >= n_lat, g_ref[1:2, :], g_ref[0:1, :])
        o_ref[...] = x_ref[...] + gate * acc


def _matmul(a_list, w_list, *, rows, n, tm, tn, col0=0, kind="plain", out_dtype=F32, resid=None, n_lat=0,
            nt=False, valid_cols=None, name="matmul"):
    assert col0 % tn == 0
    cb = col0 // tn
    in_specs = [pl.BlockSpec((tm, a.shape[1]), lambda i, j: (i, 0)) for a in a_list]
    args = list(a_list)
    vmem = sum(2 * tm * a.shape[1] * 2 for a in a_list) + 6 * tm * tn * 4
    for w3d, lead, _, _, k_rows, k_blk in w_list:
        if nt:
            in_specs.append(pl.BlockSpec((None, tn, k_rows), lambda i, j, lead=lead, k_blk=k_blk: (lead, cb + j, k_blk)))
        else:
            in_specs.append(pl.BlockSpec((None, k_rows, tn), lambda i, j, lead=lead, k_blk=k_blk: (lead, k_blk, cb + j)))
        args.append(w3d)
        vmem += k_rows * tn * (2 * w3d.dtype.itemsize + 2)
    if kind == "resid":
        x, mods, gate_idx = resid
        d = x.shape[1]
        in_specs += [pl.BlockSpec((tm, tn), lambda i, j: (i, j)),
                     pl.BlockSpec((8, tn), lambda i, j: (0, gate_idx * (d // tn) + j))]
        args += [x, mods]
    a_of_w = tuple((a_idx, a_off) for _, _, a_idx, a_off, _, _ in w_list)
    return pl.pallas_call(
        functools.partial(_mm_kernel, a_of_w=a_of_w, kind=kind, n_lat=n_lat, tm=tm, nt=nt, valid_cols=valid_cols),
        grid=(rows // tm, n // tn),
        in_specs=in_specs,
        out_specs=pl.BlockSpec((tm, tn), lambda i, j: (i, j)),
        out_shape=jax.ShapeDtypeStruct((rows, n), out_dtype),
        compiler_params=_cparams(("parallel", "arbitrary"), vmem),
        name=name,
    )(*args)


def _qkprep_kernel(x_ref, g_ref, c_ref, s_ref, o_ref, *, n_tiles, seg, scale, split):
    lane = lax.broadcasted_iota(jnp.int32, c_ref.shape, 1)
    gain, cos, sin = g_ref[...], c_ref[...], s_ref[...]
    half = seg // 2
    ow = 2 * LANES if split else LANES
    for j in range(n_tiles):
        x = x_ref[:, j * LANES:(j + 1) * LANES].astype(F32)
        x2 = x * x
        if seg == LANES:
            ms = jnp.sum(x2, axis=-1, keepdims=True) * (1.0 / seg)
        else:
            lo = lane < seg
            s_lo = jnp.sum(jnp.where(lo, x2, 0.0), axis=-1, keepdims=True)
            s_hi = jnp.sum(jnp.where(lo, 0.0, x2), axis=-1, keepdims=True)
            ms = jnp.where(lo, s_lo, s_hi) * (1.0 / seg)
        y = x * lax.rsqrt(ms + NORM_EPS) * gain
        if seg == LANES:
            partner = pltpu.roll(y, half, 1)
        else:
            partner = jnp.where((lane % seg) < half, pltpu.roll(y, LANES - half, 1), pltpu.roll(y, half, 1))
        y = (y * cos + partner * sin) * scale
        if split:
            o_ref[:, j * ow:j * ow + LANES] = jnp.where(lane < seg, y, 0.0).astype(o_ref.dtype)
            o_ref[:, j * ow + LANES:(j + 1) * ow] = jnp.where(lane < seg, 0.0, y).astype(o_ref.dtype)
        else:
            o_ref[:, j * ow:(j + 1) * ow] = y.astype(o_ref.dtype)


def _qkprep(proj, col0, n_tiles, gain, rope_c, rope_s, *, seg, scale, split):
    t = proj.shape[0]
    tm = ROW_TILE
    width = n_tiles * LANES
    assert col0 % width == 0
    cb = col0 // width
    g = jnp.tile(gain.astype(F32), LANES // seg).reshape(1, LANES)
    ow = (2 if split else 1) * width
    return pl.pallas_call(
        functools.partial(_qkprep_kernel, n_tiles=n_tiles, seg=seg, scale=scale, split=split),
        grid=(t // tm,),
        in_specs=[pl.BlockSpec((tm, width), lambda i: (i, cb)),
                  pl.BlockSpec((1, LANES), lambda i: (0, 0)),
                  pl.BlockSpec((tm, LANES), lambda i: (i, 0)),
                  pl.BlockSpec((tm, LANES), lambda i: (i, 0))],
        out_specs=pl.BlockSpec((tm, ow), lambda i: (i, 0)),
        out_shape=jax.ShapeDtypeStruct((t, ow), BF16),
        compiler_params=_cparams(("parallel",), 0),
        name="qk_prep",
    )(proj, g, rope_c, rope_s)


def _attn_kernel(q_ref, k_ref, vt_ref, *rest, n_q_heads, tq, tk, lam_init):
    n_keys = k_ref.shape[0]
    qs = [q_ref[:, g * LANES:(g + 1) * LANES] for g in range(n_q_heads)]

    def scores(kc):
        return tuple(lax.dot_general(kc, q, (((1,), (1,)), ((), ())), preferred_element_type=F32) for q in qs)

    def update(ss, vtc, carry):
        new = []
        for s, (m, acc) in zip(ss, carry):
            m_new = jnp.maximum(m, jnp.max(s, axis=0, keepdims=True))
            p = jnp.exp2(s - m_new).astype(BF16)
            new.append((m_new, jnp.exp2(m - m_new) * acc + jnp.dot(vtc, p, preferred_element_type=F32)))
        return tuple(new)

    carry = tuple((jnp.full((1, tq), -jnp.inf, F32), jnp.zeros((vt_ref.shape[0], tq), F32))
                  for _ in range(n_q_heads))
    ss = scores(k_ref[0:tk, :])
    for c in range(n_keys // tk):
        ss_next = scores(k_ref[(c + 1) * tk:(c + 2) * tk, :]) if (c + 1) * tk < n_keys else None
        carry = update(ss, vt_ref[:, c * tk:(c + 1) * tk], carry)
        ss = ss_next
    outs_t = [acc[:LANES] / acc[LANES:LANES + 1] for _, acc in carry]
    if lam_init is None:
        (o_ref,) = rest
        for g in range(n_q_heads):
            o_ref[:, g * LANES:(g + 1) * LANES] = outs_t[g].T.astype(o_ref.dtype)
    else:
        lam_ref, g_ref, o_ref = rest
        lp = lam_ref[...]
        lam = (jnp.exp(jnp.sum(lp[0:1, :] * lp[1:2, :], axis=-1, keepdims=True))
               - jnp.exp(jnp.sum(lp[2:3, :] * lp[3:4, :], axis=-1, keepdims=True)) + lam_init)
        o = (outs_t[0] - lam * outs_t[1]).T
        ms = jnp.mean(o * o, axis=-1, keepdims=True)
        o_ref[...] = (o * lax.rsqrt(ms + NORM_EPS) * g_ref[...] * (1.0 - lam_init)).astype(o_ref.dtype)


def _attention(q, k, v, *, n_kv, n_q_heads, n_lat, n_ctx, with_ctx_queries, tq, diff=None):
    t = k.shape[0]
    qw = n_q_heads * LANES
    ow = LANES if diff is not None else qw
    hv = LANES + ONES_ROWS
    vt = jnp.concatenate([v.T.reshape(n_kv, LANES, t), jnp.ones((n_kv, ONES_ROWS, t), v.dtype)], axis=1)
    vt = vt.reshape(n_kv * hv, t)
    extra_specs, extra_args, lam_init = [], [], None
    if diff is not None:
        lam_p, gain, lam_init = diff
        extra_specs = [pl.BlockSpec((4, DIFF_QK_DIM), lambda h, i: (0, 0)), pl.BlockSpec((1, LANES), lambda h, i: (0, 0))]
        extra_args = [lam_p.astype(F32), gain.astype(F32).reshape(1, LANES)]

    def call(q_rows, q_row0, n_keys, key0, tq, tk):
        qb0, kb0 = q_row0 // tq, key0 // n_keys
        vmem = 2 * 2 * n_keys * LANES * 2 + 4 * tq * qw * 4 + 8 * n_q_heads * tq * tk * 4
        return pl.pallas_call(
            functools.partial(_attn_kernel, n_q_heads=n_q_heads, tq=tq, tk=tk, lam_init=lam_init),
            grid=(n_kv, q_rows // tq),
            in_specs=[pl.BlockSpec((tq, qw), lambda h, i: (qb0 + i, h)),
                      pl.BlockSpec((n_keys, LANES), lambda h, i: (kb0, h)),
                      pl.BlockSpec((hv, n_keys), lambda h, i: (h, kb0))] + extra_specs,
            out_specs=pl.BlockSpec((tq, ow), lambda h, i: (i, h)),
            out_shape=jax.ShapeDtypeStruct((q_rows, n_kv * ow), BF16),
            compiler_params=_cparams(("parallel", "arbitrary"), vmem),
            name="attention",
        )(q, k, vt, *extra_args)

    tq = math.gcd(tq, n_lat)
    out = call(n_lat, 0, t, 0, tq, next(c for c in (768, 512, 256) if t % c == 0))
    if with_ctx_queries:
        assert n_lat % n_ctx == 0
        out = jnp.concatenate([out, call(n_ctx, n_lat, n_ctx, n_lat, n_ctx, n_ctx)], axis=0)
    return out


def _conv_kernel(x_ref, xp_ref, xn_ref, w_ref, b_ref, o_ref, *, n_lat, t_all, tm, k_col_tile):
    i, j = pl.program_id(0), pl.program_id(1)
    x = x_ref[...].astype(F32)
    row = lax.broadcasted_iota(jnp.int32, (tm, 1), 0)
    grow = i * tm + row
    prev = jnp.where(row == 0, xp_ref[15:16, :].astype(F32), pltpu.roll(x, 1, 0))
    nxt = jnp.where(row == tm - 1, xn_ref[0:1, :].astype(F32), pltpu.roll(x, tm - 1, 0))
    prev = jnp.where((grow == 0) | (grow == n_lat), 0.0, prev)
    nxt = jnp.where((grow == n_lat - 1) | (grow == t_all - 1), 0.0, nxt)
    y = b_ref[...] + prev * w_ref[0:1, :] + x * w_ref[1:2, :] + nxt * w_ref[2:3, :]
    y = _silu(y) * jnp.where(j >= k_col_tile, MLSTM_DIM ** -0.5, 1.0)
    o_ref[...] = y.astype(o_ref.dtype)


def _mlstm_conv(proj, conv_w, conv_b, n_lat):
    t = proj.shape[0]
    tm, tc, hb = (CONV_ROW_TILE if t % CONV_ROW_TILE == 0 else ROW_TILE), 512, 16
    width = 2 * MLSTM_HEADS * MLSTM_DIM
    cb = _MQ // tc
    nrb = t // hb
    return pl.pallas_call(
        functools.partial(_conv_kernel, n_lat=n_lat, t_all=t, tm=tm, k_col_tile=(width // 2) // tc),
        grid=(t // tm, width // tc),
        in_specs=[pl.BlockSpec((tm, tc), lambda i, j: (i, cb + j)),
                  pl.BlockSpec((hb, tc), lambda i, j: (jnp.maximum(i * (tm // hb) - 1, 0), cb + j)),
                  pl.BlockSpec((hb, tc), lambda i, j: (jnp.minimum((i + 1) * (tm // hb), nrb - 1), cb + j)),
                  pl.BlockSpec((3, tc), lambda i, j: (0, j)),
                  pl.BlockSpec((1, tc), lambda i, j: (0, j))],
        out_specs=pl.BlockSpec((tm, tc), lambda i, j: (i, j)),
        out_shape=jax.ShapeDtypeStruct((t, width), BF16),
        compiler_params=_cparams(("parallel", "arbitrary"), 0),
        name="mlstm_conv",
    )(proj, proj, proj, conv_w.astype(F32), conv_b.astype(F32).reshape(1, width))


def _mlstm_kernel(q_ref, k_ref, vt_ref, gi_ref, gf_ref, bi_ref, bf_ref, mask_ref, ht_ref, ct_sc, n_sc, m_sc, *, n_heads):
    @pl.when(pl.program_id(0) == 0)
    def _():
        ct_sc[...] = jnp.zeros_like(ct_sc)
        n_sc[...] = jnp.zeros_like(n_sc)
        m_sc[...] = jnp.zeros_like(m_sc)

    vis_t = mask_ref[...]
    li_all = gi_ref[...] + bi_ref[...]
    lf_all = _log_sigmoid(gf_ref[...] + bf_ref[...])
    br_all = jnp.dot(lf_all, vis_t, preferred_element_type=F32, precision=HIGHEST)
    bend_all = jnp.sum(lf_all, axis=1, keepdims=True)
    c_cols = (br_all - li_all).T
    d = MLSTM_DIM
    nt = (((1,), (1,)), ((), ()))
    for h in range(n_heads):
        lir, br = li_all[h:h + 1, :], br_all[h:h + 1, :]
        b_end = bend_all[h:h + 1, :]
        m_prev = m_sc[h:h + 1, 0:1]
        lw = b_end - br + lir
        m_new = jnp.maximum(b_end + m_prev, jnp.max(lw, axis=1, keepdims=True))
        w_end = jnp.exp(lw - m_new)
        decay = jnp.exp(b_end + m_prev - m_new)
        log_d = jnp.where(vis_t > 0.0, br - c_cols[:, h:h + 1], -jnp.inf)
        m_t = jnp.maximum(br + m_prev, jnp.max(log_d, axis=0, keepdims=True))
        dmat = jnp.exp(log_d - m_t)
        inter = jnp.exp(br + m_prev - m_t)
        qh = q_ref[:, h * d:(h + 1) * d]
        kh = k_ref[:, h * d:(h + 1) * d]
        vth = vt_ref[h * d:(h + 1) * d, :]
        ct_prev = ct_sc[h]
        n_prev = n_sc[h:h + 1, :]
        sc = lax.dot_general(kh, qh, nt, preferred_element_type=F32) * dmat
        num = (jnp.dot(vth, sc.astype(BF16), preferred_element_type=F32)
               + inter * lax.dot_general(ct_prev.astype(BF16), qh, nt, preferred_element_type=F32))
        n8 = jnp.broadcast_to(n_prev, (8, d)).astype(BF16)
        den = (jnp.sum(sc, axis=0, keepdims=True)
               + inter * lax.dot_general(n8, qh, nt, preferred_element_type=F32)[0:1, :])
        ht_ref[h * d:(h + 1) * d, :] = num / jnp.maximum(jnp.abs(den), jnp.exp(-m_t))
        vtw = (vth.astype(F32) * w_end).astype(BF16)
        ct_sc[h] = decay * ct_prev + jnp.dot(vtw, kh, preferred_element_type=F32)
        w8 = jnp.broadcast_to(w_end, (8, w_end.shape[1])).astype(BF16)
        n_sc[h:h + 1, :] = decay * n_prev + jnp.dot(w8, kh, preferred_element_type=F32)[0:1, :]
        m_sc[h:h + 1, :] = jnp.broadcast_to(m_new, (1, LANES))


def _mlstm_direction(qk, vt, gi, gf, bi, bf, vis_t, *, n_lat, reverse):
    t = qk.shape[0]
    lc = MLSTM_CHUNK
    n_chunks = t // lc
    ctx_blk = n_lat // lc
    hd = MLSTM_HEADS * MLSTM_DIM

    def blk(c):
        lat = (n_chunks - 1 - c) if reverse else (c - 1)
        return jnp.where(c == 0, ctx_blk, lat)

    return pl.pallas_call(
        functools.partial(_mlstm_kernel, n_heads=MLSTM_HEADS),
        grid=(n_chunks,),
        in_specs=[pl.BlockSpec((lc, hd), lambda c: (blk(c), 0)),
                  pl.BlockSpec((lc, hd), lambda c: (blk(c), 1)),
                  pl.BlockSpec((hd, lc), lambda c: (0, blk(c))),
                  pl.BlockSpec((16, lc), lambda c: (0, blk(c))),
                  pl.BlockSpec((16, lc), lambda c: (0, blk(c))),
                  pl.BlockSpec((16, 1), lambda c: (0, 0)),
                  pl.BlockSpec((16, 1), lambda c: (0, 0)),
                  pl.BlockSpec((lc, lc), lambda c: (0, 0))],
        out_specs=pl.BlockSpec((hd, lc), lambda c: (0, blk(c))),
        out_shape=jax.ShapeDtypeStruct((hd, t), F32),
        scratch_shapes=[pltpu.VMEM((MLSTM_HEADS, MLSTM_DIM, MLSTM_DIM), F32),
                        pltpu.VMEM((16, LANES), F32),
                        pltpu.VMEM((16, LANES), F32)],
        compiler_params=_cparams(("arbitrary",), 16 << 20),
        name="mlstm_bwd" if reverse else "mlstm_fwd",
    )(qk, qk, vt, gi, gf, bi, bf, vis_t)


def _mlstm_finish_kernel(hf_ref, hb_ref, o0_ref, o1_ref, o2_ref, g_ref, out_ref, *, n_heads):
    d = MLSTM_DIM
    o_refs = (o0_ref, o1_ref, o2_ref)
    per_blk = o0_ref.shape[1] // d
    for h in range(n_heads):
        sl = slice(h * d, (h + 1) * d)
        s = hf_ref[sl, :] + hb_ref[sl, :]
        ms = jnp.mean(s * s, axis=0, keepdims=True)
        y = (s * lax.rsqrt(ms + NORM_EPS)).T * g_ref[:, sl]
        o = o_refs[h // per_blk][:, (h % per_blk) * d:(h % per_blk + 1) * d]
        out_ref[:, sl] = (jax.nn.sigmoid(o.astype(F32)) * y).astype(out_ref.dtype)


def _mlstm_finish(hf_t, hb_t, proj, out_gain, rows):
    hd = MLSTM_HEADS * MLSTM_DIM
    tm = ROW_TILE
    ow = hd // 3
    assert _MO % ow == 0
    return pl.pallas_call(
        functools.partial(_mlstm_finish_kernel, n_heads=MLSTM_HEADS),
        grid=(rows // tm,),
        in_specs=[pl.BlockSpec((hd, tm), lambda i: (0, i)),
                  pl.BlockSpec((hd, tm), lambda i: (0, i)),
                  pl.BlockSpec((tm, ow), lambda i: (i, _MO // ow)),
                  pl.BlockSpec((tm, ow), lambda i: (i, _MO // ow + 1)),
                  pl.BlockSpec((tm, ow), lambda i: (i, _MO // ow + 2)),
                  pl.BlockSpec((1, hd), lambda i: (0, 0))],
        out_specs=pl.BlockSpec((tm, hd), lambda i: (i, 0)),
        out_shape=jax.ShapeDtypeStruct((rows, hd), BF16),
        compiler_params=_cparams(("parallel",), 0),
        name="mlstm_finish",
    )(hf_t, hb_t, proj, proj, proj, out_gain.astype(F32).reshape(1, hd))


def _moe_up_kernel(be_ref, nv_ref, st_ref, tok_ref, wg_ref, wu_ref, o_ref, xbuf, xb, sem, *, nj):
    b, j = pl.program_id(0), pl.program_id(1)
    nb = pl.num_programs(0)
    n_valid = nv_ref[b]
    per_step = MOE_BLK // nj

    def copy(blk, r):
        src = tok_ref.at[pl.ds(st_ref[blk * MOE_BLK + r], 1), :]
        return pltpu.make_async_copy(src, xbuf.at[pl.ds(r, 1), :], sem)

    @pl.when((b == 0) & (j == 0))
    def _():
        def body(r, carry):
            copy(0, r).start()
            return carry
        lax.fori_loop(0, MOE_BLK, body, 0)

    was_requested = (b == 0) | (nv_ref[jnp.maximum(b - 1, 0)] > 0)

    @pl.when((j == 0) & was_requested)
    def _():
        pltpu.make_async_copy(tok_ref.at[pl.ds(0, MOE_BLK), :], xbuf, sem).wait()
        xb[...] = xbuf[...].astype(BF16)

    def request_next():
        nxt = jnp.minimum(b + 1, nb - 1)
        for r in range(per_step):
            copy(nxt, j * per_step + r).start(priority=r % 2)

    def swiglu(sl):
        a = xb[sl, :]
        g = jnp.dot(a, _mxu_operand(wg_ref[...]), preferred_element_type=F32)
        u = jnp.dot(a, _mxu_operand(wu_ref[...]), preferred_element_type=F32)
        o_ref[sl, :] = (_silu(g) * u).astype(o_ref.dtype)

    @pl.when(n_valid == MOE_BLK)
    def _():
        request_next()
        swiglu(slice(None))

    @pl.when((n_valid > 0) & (n_valid < MOE_BLK))
    def _():
        request_next()
        for s in range(MOE_BLK // MOE_SLAB):
            sl = slice(s * MOE_SLAB, (s + 1) * MOE_SLAB)

            @pl.when(n_valid > s * MOE_SLAB)
            def _():
                swiglu(sl)

            @pl.when(n_valid <= s * MOE_SLAB)
            def _():
                o_ref[sl, :] = jnp.zeros((MOE_SLAB, o_ref.shape[1]), o_ref.dtype)

    @pl.when(n_valid == 0)
    def _():
        o_ref[...] = jnp.zeros_like(o_ref)

    @pl.when((b == nb - 1) & (j == nj - 1) & (n_valid > 0))
    def _():
        pltpu.make_async_copy(tok_ref.at[pl.ds(0, MOE_BLK), :], xbuf, sem).wait()


def _moe_up(tok, w_gate, w_up, lead, block_e, n_valid, slot_tok, *, tn):
    n_slots = slot_tok.shape[0]
    k = tok.shape[1]
    n = w_gate.shape[-1]
    nb, nj = n_slots // MOE_BLK, n // tn
    assert MOE_BLK % nj == 0

    def w_map(b, j, be, nv, st):
        return (lead, be[b], 0, jnp.where(nv[b] > 0, j, nj - 1))

    return pl.pallas_call(
        functools.partial(_moe_up_kernel, nj=nj),
        grid_spec=pltpu.PrefetchScalarGridSpec(
            num_scalar_prefetch=3,
            grid=(nb, nj),
            in_specs=[pl.BlockSpec(memory_space=pl.ANY),
                      pl.BlockSpec((None, None, k, tn), w_map),
                      pl.BlockSpec((None, None, k, tn), w_map)],
            out_specs=pl.BlockSpec((MOE_BLK, tn), lambda b, j, be, nv, st: (b, j)),
            scratch_shapes=[pltpu.VMEM((MOE_BLK, k), F32),
                            pltpu.VMEM((MOE_BLK, k), BF16),
                            pltpu.SemaphoreType.DMA(())]),
        out_shape=jax.ShapeDtypeStruct((n_slots, n), BF16),
        compiler_params=_cparams(("arbitrary", "arbitrary"),
                                 MOE_BLK * k * 6 + 2 * k * tn * (2 * w_gate.dtype.itemsize + 2)
                                 + 4 * MOE_BLK * tn * 4),
        name="moe_up",
    )(block_e, n_valid, slot_tok, tok, w_gate, w_up)


def _moe_down_kernel(be_ref, nv_ref, nu_ref, a_ref, w_ref, o_ref):
    n_valid = nv_ref[pl.program_id(0)]

    def down(sl):
        o_ref[sl, :] = jnp.dot(a_ref[sl, :], _mxu_operand(w_ref[...]), preferred_element_type=F32)

    @pl.when(n_valid == MOE_BLK)
    def _():
        down(slice(None))

    @pl.when((n_valid > 0) & (n_valid < MOE_BLK))
    def _():
        for s in range(MOE_BLK // MOE_SLAB):
            sl = slice(s * MOE_SLAB, (s + 1) * MOE_SLAB)

            @pl.when(n_valid > s * MOE_SLAB)
            def _():
                down(sl)

            @pl.when(n_valid <= s * MOE_SLAB)
            def _():
                o_ref[sl, :] = jnp.zeros((MOE_SLAB, o_ref.shape[1]), o_ref.dtype)

    @pl.when(n_valid == 0)
    def _():
        o_ref[...] = jnp.zeros_like(o_ref)


def _moe_down(a, w, lead, block_e, n_valid, n_used, *, tn):
    n_slots, k = a.shape
    n = w.shape[-1]
    nb, nj = n_slots // MOE_BLK, n // tn

    def a_map(b, j, be, nv, nu):
        return (jnp.minimum(b, nu[0] - 1), 0)

    def w_map(b, j, be, nv, nu):
        return (lead, be[b], 0, jnp.where(nv[b] > 0, j, nj - 1))

    return pl.pallas_call(
        _moe_down_kernel,
        grid_spec=pltpu.PrefetchScalarGridSpec(
            num_scalar_prefetch=3,
            grid=(nb, nj),
            in_specs=[pl.BlockSpec((MOE_BLK, k), a_map), pl.BlockSpec((None, None, k, tn), w_map)],
            out_specs=pl.BlockSpec((MOE_BLK, tn), lambda b, j, be, nv, nu: (b, j))),
        out_shape=jax.ShapeDtypeStruct((n_slots, n), F32),
        compiler_params=_cparams(("arbitrary", "arbitrary"),
                                 2 * MOE_BLK * k * 2 + k * tn * (2 * w.dtype.itemsize + 2) + 4 * MOE_BLK * tn * 4),
        name="moe_down",
    )(block_e, n_valid, n_used, a, w)


def _combine_kernel(d0_ref, d1_ref, y_ref, x_ref, w0_ref, w1_ref, g_ref, o_ref, buf, sem, *, rows):
    i = pl.program_id(0)
    n = pl.num_programs(0)
    slot = i % 2

    def gather(step, slot_):
        base = step * rows

        def body(r, carry):
            for which, d_ref in enumerate((d0_ref, d1_ref)):
                pltpu.make_async_copy(y_ref.at[pl.ds(d_ref[base + r], 1), :],
                                      buf.at[slot_, which, pl.ds(r, 1), :], sem.at[slot_]).start(priority=which)
            return carry

        lax.fori_loop(0, rows, body, 0, unroll=8)

    @pl.when(i == 0)
    def _():
        gather(0, 0)

    @pl.when(i + 1 < n)
    def _():
        gather(i + 1, 1 - slot)

    for which in range(2):
        pltpu.make_async_copy(y_ref.at[pl.ds(0, rows), :], buf.at[slot, which], sem.at[slot]).wait()
    y = w0_ref[...] * buf[slot, 0] + w1_ref[...] * buf[slot, 1]
    o_ref[...] = x_ref[...] + g_ref[0:1, :] * y


def _moe_combine(y_slots, x, d0, d1, w0, w1, mods, gate_idx):
    n_tok, d = x.shape
    rows = GATHER_ROWS // 2
    return pl.pallas_call(
        functools.partial(_combine_kernel, rows=rows),
        grid_spec=pltpu.PrefetchScalarGridSpec(
            num_scalar_prefetch=2,
            grid=(n_tok // rows,),
            in_specs=[pl.BlockSpec(memory_space=pl.ANY),
                      pl.BlockSpec((rows, d), lambda i, a, b: (i, 0)),
                      pl.BlockSpec((rows, 1), lambda i, a, b: (i, 0)),
                      pl.BlockSpec((rows, 1), lambda i, a, b: (i, 0)),
                      pl.BlockSpec((8, d), lambda i, a, b: (0, gate_idx))],
            out_specs=pl.BlockSpec((rows, d), lambda i, a, b: (i, 0)),
            scratch_shapes=[pltpu.VMEM((2, 2, rows, d), F32), pltpu.SemaphoreType.DMA((2,))]),
        out_shape=jax.ShapeDtypeStruct((n_tok, d), F32),
        compiler_params=_cparams(("arbitrary",), 12 * rows * d * 4),
        name="moe_combine",
    )(d0, d1, y_slots, x, w0, w1, mods)


def _moe_routing(logits, n_tok):
    top_logit, top_idx = lax.top_k(logits, TOP_K)
    top_w = jax.nn.softmax(top_logit, axis=-1)
    flat_e = top_idx.reshape(-1).astype(jnp.int32)
    n_assign = n_tok * TOP_K
    onehot = (flat_e[:, None] == jnp.arange(N_EXPERTS, dtype=jnp.int32)[None, :]).astype(jnp.int32)
    csum = jnp.cumsum(onehot, axis=0)
    rank = jnp.take_along_axis(csum, flat_e[:, None], axis=1)[:, 0] - 1
    counts = csum[-1]
    padded = (counts + MOE_BLK - 1) // MOE_BLK * MOE_BLK
    pad_ends = jnp.cumsum(padded)
    pad_starts = pad_ends - padded
    dest = (pad_starts[flat_e] + rank).astype(jnp.int32)
    n_blocks = -(-(n_assign + N_EXPERTS * (MOE_BLK - 1)) // MOE_BLK)
    n_slots = n_blocks * MOE_BLK
    slot_tok = jnp.zeros((n_slots,), jnp.int32).at[dest].set(jnp.arange(n_assign, dtype=jnp.int32) // TOP_K)
    blk_start = jnp.arange(n_blocks, dtype=jnp.int32) * MOE_BLK
    block_e = jnp.minimum(jnp.searchsorted(pad_ends, blk_start, side="right"), N_EXPERTS - 1).astype(jnp.int32)
    slab_ends = pad_starts + (counts + MOE_SLAB - 1) // MOE_SLAB * MOE_SLAB
    n_valid = jnp.clip(slab_ends[block_e] - blk_start, 0, MOE_BLK).astype(jnp.int32)
    n_used = (pad_ends[-1] // MOE_BLK).astype(jnp.int32).reshape(1)
    dest2 = dest.reshape(n_tok, TOP_K)
    return dest2[:, 0], dest2[:, 1], top_w[:, 0:1], top_w[:, 1:2], slot_tok, block_e, n_valid, n_used


def _rope_tables(n_lat, n_ctx, dim):
    rows = n_lat // GRID_W
    row = np.repeat(np.arange(rows, dtype=np.float32), GRID_W)
    col = np.tile(np.arange(GRID_W, dtype=np.float32), rows)
    quarter = dim // 4
    inv = (np.float32(ROPE_THETA) ** (-np.arange(quarter, dtype=np.float32) / np.float32(quarter))).astype(np.float32)
    ang = np.concatenate([row[:, None] * inv, col[:, None] * inv], axis=-1).astype(np.float32)
    cos, sin = np.cos(ang), np.sin(ang)
    reps = LANES // dim
    c = np.tile(np.concatenate([cos, cos], axis=-1), (1, reps))
    s = np.tile(np.concatenate([-sin, sin], axis=-1), (1, reps))
    c = np.concatenate([c, np.ones((n_ctx, LANES), np.float32)], axis=0)
    s = np.concatenate([s, np.zeros((n_ctx, LANES), np.float32)], axis=0)
    return jnp.asarray(c, F32), jnp.asarray(s, F32)


def kernel(x, c, ctx, c_ctx, w_mod, b_mod, attn_norm, ffn_norm, w_in, w_out, diff_q_norm, diff_k_norm, diff_lambda, diff_out_norm, gqa_q_norm, gqa_k_norm, mlstm_conv_w, mlstm_conv_b, mlstm_gate_b, mlstm_out_norm, ffn_w_gate, ffn_w_up, ffn_w_down, router_w, router_b, exp_w_gate, exp_w_up, exp_w_down):
    batch, n_lat, d = x.shape
    n_ctx = ctx.shape[1]
    depth = w_in.shape[0]
    assert batch == 1 and n_ctx == ROW_TILE and n_lat % MLSTM_CHUNK == 0 and n_lat % GRID_W == 0
    t_all = n_lat + n_ctx
    hd = MLSTM_HEADS * MLSTM_DIM

    cc = jnp.zeros((8, d), F32).at[0].set(c[0]).at[1].set(c_ctx)
    mods_all = _mods(cc, w_mod, b_mod)
    rope_diff = _rope_tables(n_lat, n_ctx, DIFF_QK_DIM)
    rope_gqa = _rope_tables(n_lat, n_ctx, GQA_DIM)
    pos = jnp.arange(MLSTM_CHUNK)
    vis_fwd = (pos[None, :] <= pos[:, None]).astype(F32)
    vis_bwd = (pos[None, :] >= pos[:, None]).astype(F32)

    w_in_t = jnp.swapaxes(w_in, 1, 2)
    xa = None
    for l in range(depth):
        need_ctx = l < depth - 1
        rows = t_all if need_ctx else n_lat
        lam_init = 0.8 - 0.6 * math.exp(-0.3 * l)
        mods = mods_all[l]

        if l == 0:
            h, xa = _modnorm(x[0], attn_norm[l], mods, 0, n_lat, ctx_rows=ctx[0])
        else:
            h = _modnorm(xa, attn_norm[l], mods, 0, n_lat)
        proj = _matmul([h], [(w_in_t, l, 0, 0, d, 0)], rows=t_all, n=_MAIN_WIDTH, tm=_row_tile(t_all), tn=512,
                       nt=True, out_dtype=BF16, name="in_proj")
        gates = _matmul([h], [(w_in_t, l, 0, 0, d, 0)], rows=t_all, n=LANES, col0=_MG, tm=_row_tile(t_all), tn=LANES,
                        nt=True, valid_cols=w_in.shape[-1] - _MG, name="gate_proj")

        dq = _qkprep(proj, _DQ, DIFF_HEADS, diff_q_norm[l], *rope_diff, seg=DIFF_QK_DIM,
                     scale=DIFF_QK_DIM ** -0.5 * LOG2E, split=True)
        dk = _qkprep(proj, _DK, DIFF_HEADS, diff_k_norm[l], *rope_diff, seg=DIFF_QK_DIM, scale=1.0, split=False)
        mix_d = _attention(dq, dk, proj[:, _DV:_GQ], n_kv=DIFF_HEADS, n_q_heads=2,
                           n_lat=n_lat, n_ctx=n_ctx, with_ctx_queries=need_ctx, tq=DIFF_Q_TILE,
                           diff=(diff_lambda[l], diff_out_norm[l], lam_init))

        gq = _qkprep(proj, _GQ, GQA_HEADS, gqa_q_norm[l], *rope_gqa, seg=GQA_DIM, scale=GQA_DIM ** -0.5 * LOG2E,
                     split=False)
        gk = _qkprep(proj, _GK, GQA_KV_HEADS, gqa_k_norm[l], *rope_gqa, seg=GQA_DIM, scale=1.0, split=False)
        mix_g = _attention(gq, gk, proj[:, _GV:_MQ], n_kv=GQA_KV_HEADS,
                           n_q_heads=GQA_HEADS // GQA_KV_HEADS, n_lat=n_lat, n_ctx=n_ctx, with_ctx_queries=need_ctx,
                           tq=GQA_Q_TILE)

        qk = _mlstm_conv(proj, mlstm_conv_w[l], mlstm_conv_b[l], n_lat)
        vt_m = proj[:, _MV:_MO].T
        g_t = gates[:, :4 * MLSTM_HEADS].T.reshape(4, MLSTM_HEADS, t_all)
        g_t = jnp.pad(g_t, ((0, 0), (0, 16 - MLSTM_HEADS), (0, 0)))
        gb = jnp.pad(mlstm_gate_b[l].astype(F32), ((0, 0), (0, 16 - MLSTM_HEADS)))[:, :, None]
        h_f = _mlstm_direction(qk, vt_m, g_t[0], g_t[1], gb[0], gb[1], vis_fwd.T, n_lat=n_lat, reverse=False)
        h_b = _mlstm_direction(qk, vt_m, g_t[2], g_t[3], gb[2], gb[3], vis_bwd.T, n_lat=n_lat, reverse=True)
        mix_m = _mlstm_finish(h_f, h_b, proj, mlstm_out_norm[l], rows)

        kb = 512
        w_blocks = [(w_out, l, ai, off, kb, (row0 + off) // kb)
                    for ai, (row0, width) in enumerate(((0, 1024), (1024, 1536), (2560, 1536)))
                    for off in range(0, width, kb)]
        x1 = _matmul([mix_d, mix_g, mix_m], w_blocks, rows=rows, n=d, tm=_row_tile(rows), tn=256, kind="resid",
                     resid=(xa, mods, 2), n_lat=n_lat, name="out_proj")

        if l % 2 == 0:
            tok = _modnorm(x1, ffn_norm[l], mods, 3, n_lat)
            d_ff = ffn_w_gate.shape[-1]
            hid = _matmul([tok], [(ffn_w_gate, l // 2, 0, 0, d, 0), (ffn_w_up, l // 2, 0, 0, d, 0)], rows=rows,
                          n=d_ff, tm=_row_tile(rows), tn=256, kind="swiglu", out_dtype=BF16, name="ffn_up")
            w_down = ffn_w_down[l // 2].astype(BF16)[None]
            xa = _matmul([hid], [(w_down, 0, 0, 0, d_ff, 0)], rows=rows, n=d, tm=_row_tile(rows, (768, 512, 256)),
                         tn=256, kind="resid", resid=(x1, mods, 5), n_lat=n_lat, name="ffn_down")
        else:
            assert not need_ctx
            rw = jnp.pad(router_w[l // 2].astype(F32), ((0, 0), (0, LANES - N_EXPERTS)))
            rb = jnp.pad(router_b[l // 2].astype(F32), (0, LANES - N_EXPERTS)).reshape(1, LANES)
            tok, logits = _modnorm(x1, ffn_norm[l], mods, 3, n_lat, router=(rw, rb), out_dtype=F32)
            d0, d1, w0, w1, slot_tok, block_e, n_valid, n_used = _moe_routing(logits[:, :N_EXPERTS], rows)
            hid = _moe_up(tok, exp_w_gate, exp_w_up, l // 2, block_e, n_valid, slot_tok, tn=256)
            ys = _moe_down(hid, exp_w_down, l // 2, block_e, n_valid, n_used, tn=512)
            xa = _moe_combine(ys, x1, d0, d1, w0, w1, mods, 5)
    return xa[:n_lat].reshape(batch, n_lat, d)
```
